```python
import jax
import jax.numpy as jnp
from jax import lax
import numpy as np

D_MODEL = 1024
BATCH = 4
SEQ = 8192
DEPTH = 2

HEAD_DIM = 64
NSA_HEADS = 8
NSA_KV_HEADS = 2
NSA_HPG = NSA_HEADS // NSA_KV_HEADS
CMP_STRIDE = 16
CMP_LEN = 2 * CMP_STRIDE
CMP_HIDDEN = 128
SLC_BLOCK = 64
SLC_TOPK = 16
WINDOW = 512
Q_BLOCK = 128
FORCE = 1e4
RWKV_HEADS = 8
RWKV_DIM = RWKV_HEADS * HEAD_DIM
DECAY_LORA = 64
AAA_LORA = 64
GATE_LORA = 160
DECAY_SCALE = 0.6065306597126334
RWKV_GN_EPS = 64e-5
SGU_CHUNK = 128
SGU_GROUPS = 8
SGU_DIM = 2048
FFN_DIM = 2816
CONV_WIDTH = 3
NORM_EPS = 1e-6
LN_EPS = 1e-5
NEG_INF = -1e30
NSA_Q_DIM = NSA_HEADS * HEAD_DIM
NSA_KV_DIM = NSA_KV_HEADS * HEAD_DIM
NSA_GATE_DIM = NSA_HEADS * 3
NSA_COLS = NSA_Q_DIM + 6 * NSA_KV_DIM + NSA_GATE_DIM
RWKV_COLS = 3 * RWKV_DIM + DECAY_LORA + AAA_LORA + GATE_LORA
AB_COLS = NSA_COLS + RWKV_COLS
AB_OUT = NSA_Q_DIM + RWKV_DIM
N_EVEN = (DEPTH + 1) // 2
N_ODD = DEPTH // 2

kernel_name = 'hybrid_nsa_rwkv7_sgu_block'


def rms_norm(x, g):
    xf = x.astype(jnp.float32)
    y = xf * lax.rsqrt(jnp.mean(xf * xf, axis=-1, keepdims=True) + NORM_EPS)
    return (y * g.astype(jnp.float32)).astype(x.dtype)


def layer_norm(x, w, b, eps):
    xf = x.astype(jnp.float32)
    mu = jnp.mean(xf, axis=-1, keepdims=True)
    var = jnp.mean(jnp.square(xf - mu), axis=-1, keepdims=True)
    y = (xf - mu) * lax.rsqrt(var + eps) * w.astype(jnp.float32) + b.astype(jnp.float32)
    return y.astype(x.dtype)


def masked_softmax(s, valid):
    p = jax.nn.softmax(jnp.where(valid, s, NEG_INF), axis=-1)
    return jnp.where(valid, p, 0.0)


def shift_right(z):
    return jnp.pad(z, ((0, 0), (1, 0), (0, 0)))[:, :-1]


def split_cols(z, sizes):
    return jnp.split(z, np.cumsum(sizes)[:-1].tolist(), axis=-1)


def causal_dwconv(z, w, b):
    C = z.shape[-1]
    y = lax.conv_general_dilated(z, w[:, None, :].astype(z.dtype), window_strides=(1,),
                                 padding=[(CONV_WIDTH - 1, 0)],
                                 dimension_numbers=('NWC', 'WIO', 'NWC'),
                                 feature_group_count=C)
    return y + b


def conv_glu(h, w_gate, w_up, conv_w, conv_b, w_down):
    a = causal_dwconv(h @ w_gate, conv_w, conv_b)
    return (jax.nn.gelu(a) * (h @ w_up)) @ w_down


def nsa_attention(q, k_cmp, v_cmp, k_slc, v_slc, k_win, v_win, gates, cmp_pe, cmp_w1, cmp_w2):
    B, T, _ = q.shape
    G, HPG, dk = NSA_KV_HEADS, NSA_HPG, HEAD_DIM
    n_cmp = T // CMP_STRIDE - 1
    n_slc = T // SLC_BLOCK
    top_k = min(SLC_TOPK, n_slc)

    def kv(z):
        return z.reshape(B, T, G, dk)

    def compress(src, i):
        ch = kv(src).reshape(B, T // CMP_STRIDE, CMP_STRIDE, G, dk)
        blk = jnp.concatenate([ch[:, :-1], ch[:, 1:]], axis=2) + cmp_pe[i][None, None, :, None, :]
        blk = blk.transpose(0, 3, 1, 2, 4).reshape(B, G, n_cmp, CMP_LEN * dk)
        return jax.nn.gelu(blk @ cmp_w1[i]) @ cmp_w2[i]

    kc, vc = compress(k_cmp, 0), compress(v_cmp, 1)

    def to_blocks(z):
        return kv(z).reshape(B, n_slc, SLC_BLOCK, G, dk).transpose(0, 3, 1, 2, 4)

    ks, vs = to_blocks(k_slc), to_blocks(v_slc)

    def pad_win(z):
        return jnp.pad(kv(z).transpose(0, 2, 1, 3), ((0, 0), (0, 0), (WINDOW, 0), (0, 0)))

    kw, vw = pad_win(k_win), pad_win(v_win)
    qh = (q * HEAD_DIM ** -0.5).reshape(B, T, G, HPG, dk).transpose(0, 2, 3, 1, 4)
    gh = gates.reshape(B, T, G, HPG, 3).transpose(0, 2, 3, 1, 4)
    slopes = jnp.exp2(-8.0 * jnp.arange(1, NSA_HEADS + 1, dtype=jnp.float32) / NSA_HEADS).reshape(1, G, HPG, 1, 1)
    cmp_start = jnp.arange(n_cmp) * CMP_STRIDE
    cmp_end = cmp_start + CMP_LEN - 1
    slc_start = jnp.arange(n_slc) * SLC_BLOCK
    agg = ((cmp_start[:, None] < slc_start[None, :] + SLC_BLOCK) &
           (cmp_end[:, None] >= slc_start[None, :])).astype(jnp.float32)
    b_idx = jnp.arange(B)[:, None, None, None]
    g_idx = jnp.arange(G)[None, :, None, None]
    blk_id = jnp.arange(n_slc)

    def query_block(n):
        t0 = n * Q_BLOCK
        t = t0 + jnp.arange(Q_BLOCK)
        qb = lax.dynamic_slice_in_dim(qh, t0, Q_BLOCK, axis=3)
        gb = lax.dynamic_slice_in_dim(gh, t0, Q_BLOCK, axis=3).astype(jnp.float32)
        dist_c = t[:, None] - cmp_end[None, :]
        s_c = jnp.einsum('bghqd,bgcd->bghqc', qb, kc).astype(jnp.float32) - slopes * dist_c
        p_c = masked_softmax(s_c, dist_c >= 0)
        o_c = jnp.einsum('bghqc,bgcd->bghqd', p_c.astype(vc.dtype), vc)
        imp = jnp.einsum('bghqc,cs->bgqs', p_c, agg)
        jt = (t // SLC_BLOCK)[:, None]
        forced = (blk_id == 0) | (blk_id == jt) | (blk_id == jt - 1)
        imp = jnp.where(blk_id > jt, -FORCE, jnp.where(forced, FORCE, imp))
        _, idx = lax.top_k(imp, top_k)
        k_sel = ks[b_idx, g_idx, idx]
        v_sel = vs[b_idx, g_idx, idx].reshape(B, G, Q_BLOCK, top_k * SLC_BLOCK, dk)
        pos_s = idx[..., None] * SLC_BLOCK + jnp.arange(SLC_BLOCK)
        dist_s = (t[None, None, :, None, None] - pos_s)[:, :, None]
        s_s = jnp.einsum('bghqd,bgqkjd->bghqkj', qb, k_sel).astype(jnp.float32) - slopes[..., None] * dist_s
        p_s = masked_softmax(s_s.reshape(B, G, HPG, Q_BLOCK, -1),
                             (dist_s >= 0).reshape(B, G, 1, Q_BLOCK, -1))
        o_s = jnp.einsum('bghqm,bgqmd->bghqd', p_s.astype(v_sel.dtype), v_sel)
        kwb = lax.dynamic_slice_in_dim(kw, t0, Q_BLOCK + WINDOW, axis=2)
        vwb = lax.dynamic_slice_in_dim(vw, t0, Q_BLOCK + WINDOW, axis=2)
        pos_w = t0 - WINDOW + jnp.arange(Q_BLOCK + WINDOW)
        dist_w = t[:, None] - pos_w[None, :]
        valid_w = (dist_w >= 0) & (dist_w < WINDOW) & (pos_w[None, :] >= 0)
        s_w = jnp.einsum('bghqd,bgkd->bghqk', qb, kwb).astype(jnp.float32) - slopes * dist_w
        p_w = masked_softmax(s_w, valid_w)
        o_w = jnp.einsum('bghqk,bgkd->bghqd', p_w.astype(vwb.dtype), vwb)
        o = gb[..., 0:1] * o_c + gb[..., 1:2] * o_s + gb[..., 2:3] * o_w
        return o.astype(q.dtype)

    out = lax.map(query_block, jnp.arange(T // Q_BLOCK))
    return out.transpose(1, 0, 4, 2, 3, 5).reshape(B, T, NSA_Q_DIM)


def rwkv7_time_mix(r, k, v, w_lora, a_lora, g_lora, w0, w_up, a0, a_up, g_up, k_k, k_a, r_k, gn_w, gn_b):
    B, T, _ = r.shape
    H, N = RWKV_HEADS, HEAD_DIM
    f32 = jnp.float32

    def heads(z):
        return z.astype(f32).reshape(B, T, H, N)

    decay = jnp.exp(-DECAY_SCALE * jax.nn.sigmoid((w0 + jnp.tanh(w_lora) @ w_up).astype(f32)))
    a = jax.nn.sigmoid((a0 + a_lora @ a_up).astype(f32))
    g = jax.nn.sigmoid(g_lora) @ g_up
    kk = heads(k * k_k)
    kk = kk / jnp.maximum(jnp.sqrt(jnp.sum(kk * kk, axis=-1, keepdims=True)), 1e-12)
    k = heads(k.astype(f32) * (1.0 + (a - 1.0) * k_a.astype(f32)))
    r, v, a, decay = heads(r), heads(v), heads(a), heads(decay)

    def step(state, inp):
        r_t, w_t, k_t, v_t, kk_t, b_t = inp
        sa = jnp.einsum('bhvk,bhk->bhv', state, kk_t)
        state = (state * w_t[:, :, None, :] - sa[..., None] * b_t[:, :, None, :]
                 + v_t[..., None] * k_t[:, :, None, :])
        return state, jnp.einsum('bhvk,bhk->bhv', state, r_t)

    def tm(z):
        return z.transpose(1, 0, 2, 3)

    _, y = lax.scan(step, jnp.zeros((B, H, N, N), f32),
                    (tm(r), tm(decay), tm(k), tm(v), tm(kk), tm(kk * a)))
    y = layer_norm(tm(y), gn_w.reshape(H, N), gn_b.reshape(H, N), RWKV_GN_EPS)
    bonus = jnp.sum(r * k * r_k.astype(f32), axis=-1, keepdims=True) * v
    return ((y + bonus).reshape(B, T, H * N) * g).astype(g.dtype)


def nsa_rwkv_mixer(h, w_in, cmp_pe, cmp_w1, cmp_w2, mu, w0, w_up, a0, a_up, g_up,
                   k_k, k_a, r_k, gn_w, gn_b, w_out):
    z = h @ w_in
    nsa_z, rwkv_z = z[..., :NSA_COLS], z[..., NSA_COLS:]
    q, kc, vc, ks, vs, kw, vw, gl = split_cols(nsa_z, [NSA_Q_DIM] + [NSA_KV_DIM] * 6 + [NSA_GATE_DIM])
    y_a = nsa_attention(q, kc, vc, ks, vs, kw, vw, jax.nn.sigmoid(gl), cmp_pe, cmp_w1, cmp_w2)
    rwkv_z = rwkv_z + (shift_right(rwkv_z) - rwkv_z) * mu
    r, k, v, wl, al, gl2 = split_cols(rwkv_z, [RWKV_DIM] * 3 + [DECAY_LORA, AAA_LORA, GATE_LORA])
    y_b = rwkv7_time_mix(r, k, v, wl, al, gl2, w0, w_up, a0, a_up, g_up, k_k, k_a, r_k, gn_w, gn_b)
    return jnp.concatenate([y_a, y_b], axis=-1) @ w_out


def sgu_mixer(h, w_in, b_in, vn_w, vn_b, w_s, b_s, w_out, b_out):
    B, T, _ = h.shape
    z = jax.nn.gelu(h @ w_in + b_in)
    u, v = jnp.split(z, 2, axis=-1)
    v = layer_norm(v, vn_w, vn_b, LN_EPS)
    vc = v.reshape(B, T // SGU_CHUNK, SGU_CHUNK, SGU_GROUPS, SGU_DIM // SGU_GROUPS)
    causal = jnp.tril(jnp.ones((SGU_CHUNK, SGU_CHUNK), dtype=bool))
    ws = jnp.where(causal, w_s, 0.0)
    sv = jnp.einsum('gts,bnsgc->bntgc', ws.astype(vc.dtype), vc) + b_s.T[:, :, None]
    return (u * sv.reshape(B, T, SGU_DIM)) @ w_out + b_out


def setup_inputs(seed: int = 0):
    key = jax.random.key(seed)
    keys = iter(jax.random.split(key, 48))

    def nrm(shape, scale):
        return scale * jax.random.normal(next(keys), shape, jnp.float32)

    D, F, E, O = D_MODEL, FFN_DIM, N_EVEN, N_ODD
    return {
        'x': nrm((BATCH, SEQ, D), 1.0),
        'c': nrm((BATCH, D), 1.0),
        'ada_w': nrm((DEPTH, D, 6 * D), 0.5 * D ** -0.5),
        'ada_b': nrm((DEPTH, 6 * D), 0.01),
        'norm_mix': 1.0 + nrm((DEPTH, D), 0.02),
        'norm_ffn': 1.0 + nrm((DEPTH, D), 0.02),
        'ffn_w_gate': nrm((DEPTH, D, F), D ** -0.5),
        'ffn_w_up': nrm((DEPTH, D, F), D ** -0.5),
        'ffn_conv_w': nrm((DEPTH, CONV_WIDTH, F), CONV_WIDTH ** -0.5),
        'ffn_conv_b': nrm((DEPTH, F), 0.01),
        'ffn_w_down': nrm((DEPTH, F, D), F ** -0.5),
        'ab_w_in': nrm((E, D, AB_COLS), D ** -0.5),
        'nsa_cmp_pe': nrm((E, 2, CMP_LEN, HEAD_DIM), 0.1),
        'nsa_cmp_w1': nrm((E, 2, CMP_LEN * HEAD_DIM, CMP_HIDDEN), (CMP_LEN * HEAD_DIM) ** -0.5),
        'nsa_cmp_w2': nrm((E, 2, CMP_HIDDEN, HEAD_DIM), CMP_HIDDEN ** -0.5),
        'rwkv_mu': jax.random.uniform(next(keys), (E, RWKV_COLS), jnp.float32),
        'rwkv_w0': nrm((E, RWKV_DIM), 0.5),
        'rwkv_w_up': nrm((E, DECAY_LORA, RWKV_DIM), 0.5 * DECAY_LORA ** -0.5),
        'rwkv_a0': nrm((E, RWKV_DIM), 0.5),
        'rwkv_a_up': nrm((E, AAA_LORA, RWKV_DIM), 0.5 * AAA_LORA ** -0.5),
        'rwkv_g_up': nrm((E, GATE_LORA, RWKV_DIM), GATE_LORA ** -0.5),
        'rwkv_k_k': 1.0 + nrm((E, RWKV_DIM), 0.1),
        'rwkv_k_a': 1.0 + nrm((E, RWKV_DIM), 0.1),
        'rwkv_r_k': nrm((E, RWKV_HEADS, HEAD_DIM), 0.1),
        'rwkv_gn_w': 1.0 + nrm((E, RWKV_DIM), 0.02),
        'rwkv_gn_b': nrm((E, RWKV_DIM), 0.01),
        'ab_w_out': nrm((E, AB_OUT, D), AB_OUT ** -0.5),
        'sgu_w_in': nrm((O, D, 2 * SGU_DIM), D ** -0.5),
        'sgu_b_in': nrm((O, 2 * SGU_DIM), 0.01),
        'sgu_vn_w': 1.0 + nrm((O, SGU_DIM), 0.02),
        'sgu_vn_b': nrm((O, SGU_DIM), 0.01),
        'sgu_w_s': nrm((O, SGU_GROUPS, SGU_CHUNK, SGU_CHUNK), SGU_CHUNK ** -0.5),
        'sgu_b_s': 1.0 + nrm((O, SGU_GROUPS, SGU_CHUNK), 0.1),
        'sgu_w_out': nrm((O, SGU_DIM, D), SGU_DIM ** -0.5),
        'sgu_b_out': nrm((O, D), 0.01),
        'final_norm': 1.0 + nrm((D,), 0.02),
    }


def reference(x, c, ada_w, ada_b, norm_mix, norm_ffn, ffn_w_gate, ffn_w_up, ffn_conv_w, ffn_conv_b,
              ffn_w_down, ab_w_in, nsa_cmp_pe, nsa_cmp_w1, nsa_cmp_w2, rwkv_mu, rwkv_w0, rwkv_w_up,
              rwkv_a0, rwkv_a_up, rwkv_g_up, rwkv_k_k, rwkv_k_a, rwkv_r_k, rwkv_gn_w, rwkv_gn_b,
              ab_w_out, sgu_w_in, sgu_b_in, sgu_vn_w, sgu_vn_b, sgu_w_s, sgu_b_s, sgu_w_out,
              sgu_b_out, final_norm):
    cond = jax.nn.silu(c)
    for i in range(DEPTH):
        mod = (cond @ ada_w[i] + ada_b[i])[:, None, :]
        sh1, sc1, g1, sh2, sc2, g2 = jnp.split(mod, 6, axis=-1)
        h = rms_norm(x, norm_mix[i]) * (1.0 + sc1) + sh1
        j = i // 2
        if i % 2 == 0:
            y = nsa_rwkv_mixer(h, ab_w_in[j], nsa_cmp_pe[j], nsa_cmp_w1[j], nsa_cmp_w2[j], rwkv_mu[j],
                               rwkv_w0[j], rwkv_w_up[j], rwkv_a0[j], rwkv_a_up[j], rwkv_g_up[j],
                               rwkv_k_k[j], rwkv_k_a[j], rwkv_r_k[j], rwkv_gn_w[j], rwkv_gn_b[j],
                               ab_w_out[j])
        else:
            y = sgu_mixer(h, sgu_w_in[j], sgu_b_in[j], sgu_vn_w[j], sgu_vn_b[j], sgu_w_s[j],
                          sgu_b_s[j], sgu_w_out[j], sgu_b_out[j])
        x = x + g1 * y
        h = rms_norm(x, norm_ffn[i]) * (1.0 + sc2) + sh2
        x = x + g2 * conv_glu(h, ffn_w_gate[i], ffn_w_up[i], ffn_conv_w[i], ffn_conv_b[i], ffn_w_down[i])
    return rms_norm(x, final_norm)
```

```python
import functools

import numpy as np
import jax
import jax.numpy as jnp
from jax import lax
from jax.experimental import pallas as pl
from jax.experimental.pallas import tpu as pltpu

F32 = jnp.float32
BF16 = jnp.bfloat16
HI = lax.Precision.HIGHEST

D_MODEL = 1024
DEPTH = 2
HEAD_DIM = 64
NSA_HEADS = 8
NSA_KV_HEADS = 2
NSA_HPG = NSA_HEADS // NSA_KV_HEADS
CMP_STRIDE = 16
CMP_LEN = 2 * CMP_STRIDE
CMP_HIDDEN = 128
SLC_BLOCK = 64
SLC_TOPK = 16
WINDOW = 512
Q_BLOCK = 128
FORCE = 1e4
RWKV_HEADS = 8
RWKV_DIM = RWKV_HEADS * HEAD_DIM
DECAY_LORA = 64
AAA_LORA = 64
GATE_LORA = 160
DECAY_SCALE = 0.6065306597126334
RWKV_GN_EPS = 64e-5
SGU_CHUNK = 128
SGU_GROUPS = 8
SGU_DIM = 2048
FFN_DIM = 2816
CONV_WIDTH = 3
NORM_EPS = 1e-6
LN_EPS = 1e-5
NEG_INF = -1e30
NSA_Q_DIM = NSA_HEADS * HEAD_DIM
NSA_KV_DIM = NSA_KV_HEADS * HEAD_DIM
NSA_GATE_DIM = NSA_HEADS * 3
NSA_COLS = NSA_Q_DIM + 6 * NSA_KV_DIM + NSA_GATE_DIM

LANES = 128
SUBLANES = 8
VMEM_LIMIT_BYTES = 48 * 1024 * 1024

ROW_TILE = 512
NSA_COLS_PAD = 1408
RWKV_LORA_PAD = 128
RWKV_GATE_PAD = 256
RWKV_COLS_PAD = 3 * RWKV_DIM + 2 * RWKV_LORA_PAD + RWKV_GATE_PAD
SLC_KEY_TILE = 512
NS_PAD = LANES
RWKV_CHUNK = 64
RWKV_SUB = 16
RWKV_STEP_ROWS = 512
FFN_COL_CHUNK = 256


def _cparams(*sem):
    return pltpu.CompilerParams(dimension_semantics=sem, vmem_limit_bytes=VMEM_LIMIT_BYTES)


def _gelu(x):
    return 0.5 * x * (1.0 + jnp.tanh(0.7978845608028654 * (x + 0.044715 * (x * x * x))))


def _rms(x, w):
    return x * lax.rsqrt(jnp.mean(x * x, axis=-1, keepdims=True) + NORM_EPS) * w


def _prev_rows(cur, halo, shift):
    tm = cur.shape[0]
    ext = jnp.concatenate([halo, cur], axis=0)
    return ext[SUBLANES - shift:SUBLANES - shift + tm]


def _ada_kernel(c_ref, w_ref, b_ref, o_ref):
    c = c_ref[...]
    cond = c * jax.nn.sigmoid(c)
    o_ref[0] = jnp.dot(cond, w_ref[0], precision=HI, preferred_element_type=F32) + b_ref[0]


def _ada_mod(c, ada_w, ada_b):
    B, D = c.shape
    depth, _, N = ada_w.shape
    rows = -(-B // SUBLANES) * SUBLANES
    cp = jnp.zeros((rows, D), F32).at[:B].set(c)
    tn = N // 4
    out = pl.pallas_call(
        _ada_kernel,
        grid=(depth, N // tn),
        in_specs=[pl.BlockSpec((rows, D), lambda i, j: (0, 0)),
                  pl.BlockSpec((1, D, tn), lambda i, j: (i, 0, j)),
                  pl.BlockSpec((1, 1, tn), lambda i, j: (i, 0, j))],
        out_specs=pl.BlockSpec((1, rows, tn), lambda i, j: (i, 0, j)),
        out_shape=jax.ShapeDtypeStruct((depth, rows, N), F32),
        compiler_params=_cparams("parallel", "parallel"),
        name="ada_mod",
    )(cp, ada_w, ada_b.reshape(depth, 1, N))
    return out[:, :B]


def _nm_matmul_kernel(x_ref, nw_ref, sc_ref, sh_ref, w_ref, b_ref, o_ref, h_ref, *, act):
    @pl.when(pl.program_id(2) == 0)
    def _():
        h = _rms(x_ref[0], nw_ref[...]) * (1.0 + sc_ref[0]) + sh_ref[0]
        h_ref[...] = h.astype(BF16)

    acc = jnp.dot(h_ref[...], w_ref[...], preferred_element_type=F32) + b_ref[...]
    if act:
        acc = _gelu(acc)
    o_ref[0] = acc.astype(o_ref.dtype)


def _nm_matmul(x, nw, sc, sh, w, bias, *, tn, act=False, name):
    B, T, D = x.shape
    N = w.shape[1]
    tm = min(ROW_TILE, T)
    return pl.pallas_call(
        functools.partial(_nm_matmul_kernel, act=act),
        grid=(B, T // tm, N // tn),
        in_specs=[pl.BlockSpec((1, tm, D), lambda b, i, j: (b, i, 0)),
                  pl.BlockSpec((1, D), lambda b, i, j: (0, 0)),
                  pl.BlockSpec((1, 1, D), lambda b, i, j: (b, 0, 0)),
                  pl.BlockSpec((1, 1, D), lambda b, i, j: (b, 0, 0)),
                  pl.BlockSpec((D, tn), lambda b, i, j: (0, j)),
                  pl.BlockSpec((1, tn), lambda b, i, j: (0, j))],
        out_specs=pl.BlockSpec((1, tm, tn), lambda b, i, j: (b, i, j)),
        out_shape=jax.ShapeDtypeStruct((B, T, N), F32),
        scratch_shapes=[pltpu.VMEM((tm, D), BF16)],
        compiler_params=_cparams("parallel", "parallel", "arbitrary"),
        name=name,
    )(x, nw.reshape(1, D), sc, sh, w, bias.reshape(1, N))


def _nsa_compress_kernel(x_ref, pe_ref, w1_ref, w2_ref, o_ref):
    half = CMP_STRIDE * HEAD_DIM
    x = x_ref[0, 0]
    pe = pe_ref[0]
    xa = (x + pe[:, :half]).astype(BF16)
    xb = (x + pe[:, half:]).astype(BF16)
    w1 = w1_ref[0]
    p = jnp.dot(xa, w1[:half], preferred_element_type=F32)
    q = jnp.dot(xb, w1[half:], preferred_element_type=F32)
    nc = x.shape[0]
    hid = _gelu(p + pltpu.roll(q, nc - 1, 0))
    o_ref[0, 0] = jnp.dot(hid.astype(BF16), w2_ref[0], preferred_element_type=F32)


def _nsa_compress(x16, pe, w1, w2):
    S, BG, NC, W = x16.shape
    return pl.pallas_call(
        _nsa_compress_kernel,
        grid=(S, BG),
        in_specs=[pl.BlockSpec((1, 1, NC, W), lambda s, b: (s, b, 0, 0)),
                  pl.BlockSpec((1, 1, 2 * W), lambda s, b: (s, 0, 0)),
                  pl.BlockSpec((1, 2 * W, CMP_HIDDEN), lambda s, b: (s, 0, 0)),
                  pl.BlockSpec((1, CMP_HIDDEN, HEAD_DIM), lambda s, b: (s, 0, 0))],
        out_specs=pl.BlockSpec((1, 1, NC, HEAD_DIM), lambda s, b: (s, b, 0, 0)),
        out_shape=jax.ShapeDtypeStruct((S, BG, NC, HEAD_DIM), F32),
        compiler_params=_cparams("parallel", "parallel"),
        name="nsa_compress",
    )(x16, pe, w1, w2)


def _nsa_attn_kernel(q_ref, gl_ref, kct_ref, vc_ref, kst_ref, vs_ref, kwt_ref, vw_ref, agg_ref, exp_ref,
                     o_ref, *, n_slc, top_k):
    g = pl.program_id(1)
    n = pl.program_id(2)
    Q, R = Q_BLOCK, NSA_HPG * Q_BLOCK
    t0 = pl.multiple_of(n * Q, Q)
    q = q_ref[0, 0].reshape(R, HEAD_DIM)

    rowh = lax.broadcasted_iota(jnp.int32, (R, 1), 0) // Q
    slope_g = jnp.where(g == 0, 1.0, 2.0 ** (-NSA_HPG)).astype(F32)
    slope = jnp.where(rowh == 0, 0.5, jnp.where(rowh == 1, 0.25, jnp.where(rowh == 2, 0.125, 0.0625)))
    slope = slope.astype(F32) * slope_g
    tq = t0 + lax.broadcasted_iota(jnp.int32, (Q, 1), 0)

    def tile_heads(a):
        return jnp.concatenate([a] * NSA_HPG, axis=0)

    ncp = kct_ref.shape[-1]
    s = jnp.dot(q, kct_ref[0, 0], preferred_element_type=F32)
    cmp_end = lax.broadcasted_iota(jnp.int32, (1, ncp), 1) * CMP_STRIDE + (CMP_LEN - 1)
    dist = tq - cmp_end
    dmask = tile_heads(jnp.where(dist >= 0, -dist.astype(F32), NEG_INF))
    valid = dmask > 0.5 * NEG_INF
    s = jnp.where(valid, s + slope * dmask, NEG_INF)
    m = jnp.max(s, axis=-1, keepdims=True)
    e = jnp.where(valid, jnp.exp(s - m), 0.0)
    l = jnp.sum(e, axis=-1, keepdims=True)
    p = e / jnp.where(l > 0.0, l, 1.0)
    o_c = jnp.dot(p.astype(BF16), vc_ref[0, 0], preferred_element_type=F32)

    psum = p[0:Q] + p[Q:2 * Q] + p[2 * Q:3 * Q] + p[3 * Q:4 * Q]
    imp = jnp.dot(psum, agg_ref[...], precision=HI, preferred_element_type=F32)
    blk = lax.broadcasted_iota(jnp.int32, (1, NS_PAD), 1)
    jt = tq // SLC_BLOCK
    forced = (blk == 0) | (blk == jt) | (blk == jt - 1)
    imp = jnp.where(blk > jt, -FORCE, jnp.where(forced, FORCE, imp))
    vals = jnp.where(blk < n_slc, imp, -jnp.inf)
    blk_f = blk.astype(F32)
    sel = jnp.zeros((Q, NS_PAD), F32)
    for _ in range(top_k):
        vmax = jnp.max(vals, axis=-1, keepdims=True)
        first = jnp.min(jnp.where(vals == vmax, blk_f, float(NS_PAD)), axis=-1, keepdims=True)
        pick = blk_f == first
        sel = jnp.where(pick, 1.0, sel)
        vals = jnp.where(pick, -jnp.inf, vals)
    sel_b = sel.astype(BF16)

    TK = SLC_KEY_TILE
    n_tiles = (t0 + Q + TK - 1) // TK

    def slc_step(kt, carry):
        m_i, l_i, acc = carry
        k0 = pl.multiple_of(kt * TK, TK)
        sc = jnp.dot(q, kst_ref[0, 0, :, pl.ds(k0, TK)], preferred_element_type=F32)
        chosen = jnp.dot(sel_b, exp_ref[:, pl.ds(k0, TK)], preferred_element_type=F32)
        rel = (k0 + lax.broadcasted_iota(jnp.int32, (1, TK), 1)) - tq
        ok = (chosen > 0.5) & (rel <= 0)
        dm = tile_heads(jnp.where(ok, rel.astype(F32), NEG_INF))
        sc = sc + slope * dm
        m_new = jnp.maximum(m_i, jnp.max(sc, axis=-1, keepdims=True))
        alpha = jnp.exp(m_i - m_new)
        pe_ = jnp.exp(sc - m_new)
        l_new = alpha * l_i + jnp.sum(pe_, axis=-1, keepdims=True)
        acc = alpha * acc + jnp.dot(pe_.astype(BF16), vs_ref[0, 0, pl.ds(k0, TK), :],
                                    preferred_element_type=F32)
        return m_new, l_new, acc

    m0 = jnp.full((R, 1), NEG_INF, F32)
    l0 = jnp.zeros((R, 1), F32)
    a0 = jnp.zeros((R, HEAD_DIM), F32)
    _, l_s, acc_s = lax.fori_loop(0, n_tiles, slc_step, (m0, l0, a0))
    o_s = acc_s / l_s

    WK = WINDOW + Q
    sw = jnp.dot(q, kwt_ref[0, 0, :, pl.ds(t0, WK)], preferred_element_type=F32)
    pos = (t0 - WINDOW) + lax.broadcasted_iota(jnp.int32, (1, WK), 1)
    dw = tq - pos
    okw = (dw >= 0) & (dw < WINDOW) & (pos >= 0)
    sw = sw + slope * tile_heads(jnp.where(okw, -dw.astype(F32), NEG_INF))
    mw = jnp.max(sw, axis=-1, keepdims=True)
    ew = jnp.exp(sw - mw)
    lw = jnp.sum(ew, axis=-1, keepdims=True)
    o_w = jnp.dot(ew.astype(BF16), vw_ref[0, 0, pl.ds(t0, WK), :], preferred_element_type=F32) / lw

    gates = jax.nn.sigmoid(gl_ref[0, 0].reshape(R, 3))
    o = gates[:, 0:1] * o_c + gates[:, 1:2] * o_s + gates[:, 2:3] * o_w
    o_ref[0, 0] = o.reshape(NSA_HPG, Q, HEAD_DIM)


def _nsa_attention(q, gl, kct, vc, kst, vs, kwt, vw, agg, expand, *, n_slc, top_k):
    B, G, HPG, T, dk = q.shape
    ncp = kct.shape[-1]
    tw = kwt.shape[-1]
    grid = (B, G, T // Q_BLOCK)
    per_bg = lambda b, g, n: (b, g, 0, 0)
    return pl.pallas_call(
        functools.partial(_nsa_attn_kernel, n_slc=n_slc, top_k=top_k),
        grid=grid,
        in_specs=[pl.BlockSpec((1, 1, HPG, Q_BLOCK, dk), lambda b, g, n: (b, g, 0, n, 0)),
                  pl.BlockSpec((1, 1, HPG, Q_BLOCK, 3), lambda b, g, n: (b, g, 0, n, 0)),
                  pl.BlockSpec((1, 1, dk, ncp), per_bg),
                  pl.BlockSpec((1, 1, ncp, dk), per_bg),
                  pl.BlockSpec((1, 1, dk, T), per_bg),
                  pl.BlockSpec((1, 1, T, dk), per_bg),
                  pl.BlockSpec((1, 1, dk, tw), per_bg),
                  pl.BlockSpec((1, 1, tw, dk), per_bg),
                  pl.BlockSpec((ncp, NS_PAD), lambda b, g, n: (0, 0)),
                  pl.BlockSpec((NS_PAD, T), lambda b, g, n: (0, 0))],
        out_specs=pl.BlockSpec((1, 1, HPG, Q_BLOCK, dk), lambda b, g, n: (b, g, 0, n, 0)),
        out_shape=jax.ShapeDtypeStruct((B, G, HPG, T, dk), F32),
        compiler_params=_cparams("parallel", "parallel", "arbitrary"),
        name="nsa_attention",
    )(q, gl, kct, vc, kst, vs, kwt, vw, agg, expand)


def _rwkv_pre_kernel(z_ref, halo_ref, mu_ref, w0_ref, a0_ref, kk_ref, ka_ref, wup_ref, aup_ref, gup_ref,
                     ones_ref, r_ref, lw_ref, k_ref, v_ref, kap_ref, b_ref, g_ref):
    i = pl.program_id(1)
    z = z_ref[0]
    halo = jnp.where(i > 0, halo_ref[0], 0.0)
    zs = z + (_prev_rows(z, halo, 1) - z) * mu_ref[...]
    D = RWKV_DIM
    r, k, v = zs[:, 0:D], zs[:, D:2 * D], zs[:, 2 * D:3 * D]
    o1 = 3 * D
    o2 = o1 + RWKV_LORA_PAD
    o3 = o2 + RWKV_LORA_PAD
    wl, al, gl = zs[:, o1:o2], zs[:, o2:o3], zs[:, o3:]
    dot = functools.partial(jnp.dot, precision=HI, preferred_element_type=F32)
    lw = -DECAY_SCALE * jax.nn.sigmoid(w0_ref[...] + dot(jnp.tanh(wl), wup_ref[...]))
    a = jax.nn.sigmoid(a0_ref[...] + dot(al, aup_ref[...]))
    g = dot(jax.nn.sigmoid(gl), gup_ref[...])
    kk = k * kk_ref[...]
    ss = dot(kk * kk, ones_ref[...])
    kk = kk / jnp.maximum(jnp.sqrt(ss), 1e-12)
    r_ref[0] = r
    lw_ref[0] = lw
    k_ref[0] = k * (1.0 + (a - 1.0) * ka_ref[...])
    v_ref[0] = v
    kap_ref[0] = kk
    b_ref[0] = kk * a
    g_ref[0] = g


def _rwkv_pre(z, mu, w0, a0, k_k, k_a, w_up, a_up, g_up, ones_bd):
    B, T, W = z.shape
    D = RWKV_DIM
    tm = min(ROW_TILE, T)
    hb = tm // SUBLANES
    vec = lambda n: pl.BlockSpec((1, n), lambda b, i: (0, 0))
    mat = lambda s: pl.BlockSpec(s, lambda b, i: (0, 0))
    out = jax.ShapeDtypeStruct((B, T, D), F32)
    return pl.pallas_call(
        _rwkv_pre_kernel,
        grid=(B, T // tm),
        in_specs=[pl.BlockSpec((1, tm, W), lambda b, i: (b, i, 0)),
                  pl.BlockSpec((1, SUBLANES, W), lambda b, i: (b, jnp.maximum(i * hb - 1, 0), 0)),
                  vec(W), vec(D), vec(D), vec(D), vec(D),
                  mat((RWKV_LORA_PAD, D)), mat((RWKV_LORA_PAD, D)), mat((RWKV_GATE_PAD, D)), mat((D, D))],
        out_specs=[pl.BlockSpec((1, tm, D), lambda b, i: (b, i, 0))] * 7,
        out_shape=[out] * 7,
        compiler_params=_cparams("parallel", "parallel"),
        name="rwkv_pre",
    )(z, z, mu.reshape(1, W), w0.reshape(1, D), a0.reshape(1, D), k_k.reshape(1, D), k_a.reshape(1, D),
      w_up, a_up, g_up, ones_bd)


def _rwkv_core_kernel(r_ref, lw_ref, k_ref, v_ref, kap_ref, b_ref, g_ref, rk_ref, gw_ref, gb_ref,
                      o_ref, h_ref):
    C = RWKV_CHUNK

    @pl.when(pl.program_id(1) == 0)
    def _():
        h_ref[...] = jnp.zeros_like(h_ref)

    ri = lax.broadcasted_iota(jnp.int32, (C, C), 0)
    ci = lax.broadcasted_iota(jnp.int32, (C, C), 1)
    tril = (ri >= ci).astype(F32)
    stril = (ri > ci).astype(F32)
    eye = (ri == ci).astype(F32)
    diag_blk = ((ri // RWKV_SUB) == (ci // RWKV_SUB)).astype(F32)

    def mm(a, b):
        return jnp.dot(a, b, precision=HI, preferred_element_type=F32)

    def mm_nt(a, b):
        return lax.dot_general(a, b, (((1,), (1,)), ((), ())), precision=HI, preferred_element_type=F32)

    def mm_tn(a, b):
        return lax.dot_general(a, b, (((0,), (0,)), ((), ())), precision=HI, preferred_element_type=F32)

    rk, gw, gb = rk_ref[0], gw_ref[0], gb_ref[0]

    def chunk(c, h):
        rows = pl.ds(pl.multiple_of(c * C, C), C)
        r, lw, k, v = r_ref[0, rows, :], lw_ref[0, rows, :], k_ref[0, rows, :], v_ref[0, rows, :]
        kap, b = kap_ref[0, rows, :], b_ref[0, rows, :]
        cum = mm(tril, lw)
        g_in = jnp.exp(cum)
        g_ex = jnp.exp(cum - lw)
        g_inv = jnp.exp(-cum)
        g_end = jnp.exp(cum[C - 1:C, :])
        kap_h, r_h, b_h, k_h = kap * g_ex, r * g_in, b * g_inv, k * g_inv
        a_kb = stril * mm_nt(kap_h, b_h)
        a_kk = stril * mm_nt(kap_h, k_h)
        a_rb = tril * mm_nt(r_h, b_h)
        a_rk = tril * mm_nt(r_h, k_h)
        d = a_kb * diag_blk
        e = a_kb - d
        d_inv = eye - d
        pw = mm(d, d)
        steps = int(np.log2(RWKV_SUB)) - 1
        for s_ in range(steps):
            d_inv = mm(d_inv, eye + pw)
            if s_ + 1 < steps:
                pw = mm(pw, pw)
        nb = mm(d_inv, e)
        t_inv = eye - nb
        pw = mm(nb, nb)
        bsteps = int(np.log2(C // RWKV_SUB)) - 1
        for s_ in range(bsteps):
            t_inv = mm(t_inv, eye + pw)
            if s_ + 1 < bsteps:
                pw = mm(pw, pw)
        t_inv = mm(t_inv, d_inv)
        u = mm(t_inv, mm_nt(kap_h, h) + mm(a_kk, v))
        y = mm_nt(r_h, h) + mm(a_rk, v) - mm(a_rb, u)
        h_new = h * g_end + mm_tn(v, k_h * g_end) - mm_tn(u, b_h * g_end)
        mu = jnp.mean(y, axis=-1, keepdims=True)
        var = jnp.mean(jnp.square(y - mu), axis=-1, keepdims=True)
        yn = (y - mu) * lax.rsqrt(var + RWKV_GN_EPS) * gw + gb
        bonus = jnp.sum(r * k * rk, axis=-1, keepdims=True) * v
        o_ref[0, rows, :] = (yn + bonus) * g_ref[0, rows, :]
        return h_new

    h_ref[...] = lax.fori_loop(0, r_ref.shape[1] // C, chunk, h_ref[...])


def _rwkv_core(r, lw, k, v, kap, b, g, r_k, gn_w, gn_b):
    BH, T, N = r.shape
    H = r_k.shape[0]
    tc = min(RWKV_STEP_ROWS, T)
    seq = pl.BlockSpec((1, tc, N), lambda bh, i: (bh, i, 0))
    par = pl.BlockSpec((1, 1, N), lambda bh, i: (bh % H, 0, 0))
    return pl.pallas_call(
        _rwkv_core_kernel,
        grid=(BH, T // tc),
        in_specs=[seq] * 7 + [par] * 3,
        out_specs=seq,
        out_shape=jax.ShapeDtypeStruct((BH, T, N), F32),
        scratch_shapes=[pltpu.VMEM((N, N), F32)],
        compiler_params=_cparams("parallel", "arbitrary"),
        name="rwkv_core",
    )(r, lw, k, v, kap, b, g, r_k, gn_w, gn_b)


def _out_proj_kernel(ya_ref, yb_ref, wa_ref, wb_ref, x_ref, g_ref, o_ref):
    y = jnp.dot(ya_ref[0].astype(BF16), wa_ref[...], preferred_element_type=F32)
    y = y + jnp.dot(yb_ref[0].astype(BF16), wb_ref[...], preferred_element_type=F32)
    o_ref[0] = x_ref[0] + g_ref[0] * y


def _out_proj(ya, yb, wa, wb, x, gate):
    B, T, D = x.shape
    tm = min(ROW_TILE, T)
    ka, kb = ya.shape[-1], yb.shape[-1]
    row = lambda w: pl.BlockSpec((1, tm, w), lambda b, i: (b, i, 0))
    return pl.pallas_call(
        _out_proj_kernel,
        grid=(B, T // tm),
        in_specs=[row(ka), row(kb),
                  pl.BlockSpec((ka, D), lambda b, i: (0, 0)),
                  pl.BlockSpec((kb, D), lambda b, i: (0, 0)),
                  row(D),
                  pl.BlockSpec((1, 1, D), lambda b, i: (b, 0, 0))],
        out_specs=row(D),
        out_shape=jax.ShapeDtypeStruct((B, T, D), F32),
        compiler_params=_cparams("parallel", "parallel"),
        name="out_proj",
    )(ya, yb, wa, wb, x, gate)


def _sgu_out_kernel(u_ref, v_ref, x_ref, g_ref, vnw_ref, vnb_ref, ws_ref, bst_ref, wo_ref, bo_ref,
                    o_ref, gated_ref):
    v = v_ref[0]
    mu = jnp.mean(v, axis=-1, keepdims=True)
    var = jnp.mean(jnp.square(v - mu), axis=-1, keepdims=True)
    vn = ((v - mu) * lax.rsqrt(var + LN_EPS) * vnw_ref[...] + vnb_ref[...]).astype(BF16)
    S = SGU_CHUNK
    gw = SGU_DIM // SGU_GROUPS
    causal = lax.broadcasted_iota(jnp.int32, (S, S), 0) >= lax.broadcasted_iota(jnp.int32, (S, S), 1)
    for gi in range(SGU_GROUPS):
        ws = jnp.where(causal, ws_ref[gi], 0.0).astype(BF16)
        bs = bst_ref[:, gi:gi + 1]
        for n in range(v.shape[0] // S):
            sv = jnp.dot(ws, vn[n * S:(n + 1) * S, gi * gw:(gi + 1) * gw], preferred_element_type=F32) + bs
            u = u_ref[0, n * S:(n + 1) * S, gi * gw:(gi + 1) * gw]
            gated_ref[n * S:(n + 1) * S, gi * gw:(gi + 1) * gw] = (u * sv).astype(BF16)
    y = jnp.dot(gated_ref[...], wo_ref[...], preferred_element_type=F32) + bo_ref[...]
    o_ref[0] = x_ref[0] + g_ref[0] * y


def _sgu_out(z, x, gate, vn_w, vn_b, w_s, b_s_t, w_out, b_out):
    B, T, D = x.shape
    E = SGU_DIM
    tm = min(ROW_TILE, T)
    vec = lambda n: pl.BlockSpec((1, n), lambda b, i: (0, 0))
    return pl.pallas_call(
        _sgu_out_kernel,
        grid=(B, T // tm),
        in_specs=[pl.BlockSpec((1, tm, E), lambda b, i: (b, i, 0)),
                  pl.BlockSpec((1, tm, E), lambda b, i: (b, i, 1)),
                  pl.BlockSpec((1, tm, D), lambda b, i: (b, i, 0)),
                  pl.BlockSpec((1, 1, D), lambda b, i: (b, 0, 0)),
                  vec(E), vec(E),
                  pl.BlockSpec((SGU_GROUPS, SGU_CHUNK, SGU_CHUNK), lambda b, i: (0, 0, 0)),
                  pl.BlockSpec((SGU_CHUNK, SGU_GROUPS), lambda b, i: (0, 0)),
                  pl.BlockSpec((E, D), lambda b, i: (0, 0)),
                  vec(D)],
        out_specs=pl.BlockSpec((1, tm, D), lambda b, i: (b, i, 0)),
        out_shape=jax.ShapeDtypeStruct((B, T, D), F32),
        scratch_shapes=[pltpu.VMEM((tm, E), BF16)],
        compiler_params=_cparams("parallel", "parallel"),
        name="sgu_out",
    )(z, z, x, gate, vn_w.reshape(1, E), vn_b.reshape(1, E), w_s, b_s_t, w_out, b_out.reshape(1, D))


def _ffn_out_kernel(gate_ref, halo_ref, up_ref, x_ref, g_ref, cw_ref, cb_ref, wd_ref, fn_ref, o_ref, *, final):
    i = pl.program_id(1)
    tm = x_ref.shape[1]
    F = gate_ref.shape[2]
    acc = jnp.zeros((tm, x_ref.shape[2]), F32)
    for c0 in range(0, F, FFN_COL_CHUNK):
        cols = slice(c0, c0 + FFN_COL_CHUNK)
        cur = gate_ref[0, :, cols]
        halo = jnp.where(i > 0, halo_ref[0, :, cols], 0.0)
        cw = cw_ref[:, cols]
        a = (cw[0:1] * _prev_rows(cur, halo, 2) + cw[1:2] * _prev_rows(cur, halo, 1) + cw[2:3] * cur
             + cb_ref[:, cols])
        act = (_gelu(a) * up_ref[0, :, cols]).astype(BF16)
        acc = acc + jnp.dot(act, wd_ref[cols, :], preferred_element_type=F32)
    xn = x_ref[0] + g_ref[0] * acc
    if final:
        xn = _rms(xn, fn_ref[...])
    o_ref[0] = xn


def _ffn_out(gu, x, gate, conv_w, conv_b, w_down, final_w, *, final):
    B, T, D = x.shape
    F = w_down.shape[0]
    tm = min(ROW_TILE, T)
    hb = tm // SUBLANES
    return pl.pallas_call(
        functools.partial(_ffn_out_kernel, final=final),
        grid=(B, T // tm),
        in_specs=[pl.BlockSpec((1, tm, F), lambda b, i: (b, i, 0)),
                  pl.BlockSpec((1, SUBLANES, F), lambda b, i: (b, jnp.maximum(i * hb - 1, 0), 0)),
                  pl.BlockSpec((1, tm, F), lambda b, i: (b, i, 1)),
                  pl.BlockSpec((1, tm, D), lambda b, i: (b, i, 0)),
                  pl.BlockSpec((1, 1, D), lambda b, i: (b, 0, 0)),
                  pl.BlockSpec((CONV_WIDTH, F), lambda b, i: (0, 0)),
                  pl.BlockSpec((1, F), lambda b, i: (0, 0)),
                  pl.BlockSpec((F, D), lambda b, i: (0, 0)),
                  pl.BlockSpec((1, D), lambda b, i: (0, 0))],
        out_specs=pl.BlockSpec((1, tm, D), lambda b, i: (b, i, 0)),
        out_shape=jax.ShapeDtypeStruct((B, T, D), F32),
        compiler_params=_cparams("parallel", "parallel"),
        name="ffn_out",
    )(gu, gu, gu, x, gate, conv_w, conv_b.reshape(1, F), w_down, final_w.reshape(1, D))


def _pad_cols(w, sizes, padded):
    parts, o = [], 0
    for s, p in zip(sizes, padded):
        parts.append(jnp.pad(w[..., o:o + s], [(0, 0)] * (w.ndim - 1) + [(0, p - s)]))
        o += s
    return jnp.concatenate(parts, axis=-1)


def _pad_rows(w, rows):
    return jnp.pad(w, ((0, rows - w.shape[0]), (0, 0)))


def _nsa_tables(T):
    n_cmp_pad = T // CMP_STRIDE
    n_slc = T // SLC_BLOCK
    c = np.arange(n_cmp_pad)
    s = np.arange(NS_PAD)
    cs, ce, ss = c * CMP_STRIDE, c * CMP_STRIDE + CMP_LEN - 1, s * SLC_BLOCK
    agg = (cs[:, None] < ss[None, :] + SLC_BLOCK) & (ce[:, None] >= ss[None, :]) & (s[None, :] < n_slc)
    agg &= (c[:, None] < n_cmp_pad - 1)
    expand = (np.arange(T)[None, :] // SLC_BLOCK) == s[:, None]
    return jnp.asarray(agg, F32), jnp.asarray(expand, BF16)


def _nsa_rwkv_mixer(x, nw, sc, sh, gate, w_in, cmp_pe, cmp_w1, cmp_w2, mu, w0, w_up, a0, a_up, g_up,
                    k_k, k_a, r_k, gn_w, gn_b, w_out):
    B, T, D = x.shape
    G, HPG, dk, H = NSA_KV_HEADS, NSA_HPG, HEAD_DIM, RWKV_HEADS
    n_slc = T // SLC_BLOCK
    assert n_slc <= NS_PAD and T % (CMP_STRIDE * LANES) == 0

    w_nsa = jnp.pad(w_in[:, :NSA_COLS], ((0, 0), (0, NSA_COLS_PAD - NSA_COLS))).astype(BF16)
    rw_sizes = [RWKV_DIM] * 3 + [DECAY_LORA, AAA_LORA, GATE_LORA]
    rw_pads = [RWKV_DIM] * 3 + [RWKV_LORA_PAD, RWKV_LORA_PAD, RWKV_GATE_PAD]
    w_rw = _pad_cols(w_in[:, NSA_COLS:], rw_sizes, rw_pads).astype(BF16)
    z_nsa = _nm_matmul(x, nw, sc, sh, w_nsa, jnp.zeros((NSA_COLS_PAD,), F32), tn=NSA_COLS_PAD, name="in_proj_nsa")
    z_rw = _nm_matmul(x, nw, sc, sh, w_rw, jnp.zeros((RWKV_COLS_PAD,), F32), tn=RWKV_COLS_PAD // 2,
                      name="in_proj_rwkv")

    def kv(i):
        o = NSA_Q_DIM + i * NSA_KV_DIM
        return z_nsa[..., o:o + NSA_KV_DIM].reshape(B, T, G, dk).transpose(0, 2, 1, 3)

    q = (z_nsa[..., :NSA_Q_DIM] * dk ** -0.5).reshape(B, T, G, HPG, dk).transpose(0, 2, 3, 1, 4).astype(BF16)
    gl = z_nsa[..., NSA_COLS - NSA_GATE_DIM:NSA_COLS].reshape(B, T, G, HPG, 3).transpose(0, 2, 3, 1, 4)
    x16 = jnp.stack([kv(0), kv(1)]).reshape(2, B * G, T // CMP_STRIDE, CMP_STRIDE * dk)
    cmp = _nsa_compress(x16, cmp_pe.reshape(2, 1, CMP_LEN * dk), cmp_w1.astype(BF16), cmp_w2.astype(BF16))
    cmp = cmp.reshape(2, B, G, T // CMP_STRIDE, dk)
    kct = cmp[0].transpose(0, 1, 3, 2).astype(BF16)
    vc = cmp[1].astype(BF16)
    kst = kv(2).transpose(0, 1, 3, 2).astype(BF16)
    vs = kv(3).astype(BF16)
    front = ((0, 0), (0, 0), (WINDOW, 0), (0, 0))
    kwt = jnp.pad(kv(4), front).transpose(0, 1, 3, 2).astype(BF16)
    vw = jnp.pad(kv(5), front).astype(BF16)
    agg, expand = _nsa_tables(T)
    y_a = _nsa_attention(q, gl, kct, vc, kst, vs, kwt, vw, agg, expand, n_slc=n_slc, top_k=min(SLC_TOPK, n_slc))
    y_a = y_a.transpose(0, 3, 1, 2, 4).reshape(B, T, NSA_Q_DIM)

    ones_bd = jnp.asarray(np.kron(np.eye(H), np.ones((dk, dk))), F32)
    pre = _rwkv_pre(z_rw, _pad_cols(mu, rw_sizes, rw_pads), w0, a0, k_k, k_a,
                    _pad_rows(w_up, RWKV_LORA_PAD), _pad_rows(a_up, RWKV_LORA_PAD),
                    _pad_rows(g_up, RWKV_GATE_PAD), ones_bd)
    heads = lambda a: a.reshape(B, T, H, dk).transpose(0, 2, 1, 3).reshape(B * H, T, dk)
    hp = lambda a: a.reshape(H, 1, dk)
    y_b = _rwkv_core(*[heads(a) for a in pre], hp(r_k), hp(gn_w), hp(gn_b))
    y_b = y_b.reshape(B, H, T, dk).transpose(0, 2, 1, 3).reshape(B, T, RWKV_DIM)

    w_out = w_out.astype(BF16)
    return _out_proj(y_a, y_b, w_out[:NSA_Q_DIM], w_out[NSA_Q_DIM:], x, gate)


def kernel(x, c, ada_w, ada_b, norm_mix, norm_ffn, ffn_w_gate, ffn_w_up, ffn_conv_w, ffn_conv_b, ffn_w_down, ab_w_in, nsa_cmp_pe, nsa_cmp_w1, nsa_cmp_w2, rwkv_mu, rwkv_w0, rwkv_w_up, rwkv_a0, rwkv_a_up, rwkv_g_up, rwkv_k_k, rwkv_k_a, rwkv_r_k, rwkv_gn_w, rwkv_gn_b, ab_w_out, sgu_w_in, sgu_b_in, sgu_vn_w, sgu_vn_b, sgu_w_s, sgu_b_s, sgu_w_out, sgu_b_out, final_norm):
    B, T, D = x.shape
    mod = _ada_mod(c, ada_w, ada_b)
    for i in range(DEPTH):
        sh1, sc1, g1, sh2, sc2, g2 = [m.reshape(B, 1, D) for m in jnp.split(mod[i], 6, axis=-1)]
        j = i // 2
        if i % 2 == 0:
            x = _nsa_rwkv_mixer(x, norm_mix[i], sc1, sh1, g1, ab_w_in[j], nsa_cmp_pe[j], nsa_cmp_w1[j],
                                nsa_cmp_w2[j], rwkv_mu[j], rwkv_w0[j], rwkv_w_up[j], rwkv_a0[j], rwkv_a_up[j],
                                rwkv_g_up[j], rwkv_k_k[j], rwkv_k_a[j], rwkv_r_k[j], rwkv_gn_w[j], rwkv_gn_b[j],
                                ab_w_out[j])
        else:
            z = _nm_matmul(x, norm_mix[i], sc1, sh1, sgu_w_in[j].astype(BF16), sgu_b_in[j], tn=1024, act=True,
                           name="sgu_in")
            x = _sgu_out(z, x, g1, sgu_vn_w[j], sgu_vn_b[j], sgu_w_s[j], sgu_b_s[j].T, sgu_w_out[j].astype(BF16),
                         sgu_b_out[j])
        w_gu = jnp.concatenate([ffn_w_gate[i], ffn_w_up[i]], axis=1).astype(BF16)
        gu = _nm_matmul(x, norm_ffn[i], sc2, sh2, w_gu, jnp.zeros((2 * FFN_DIM,), F32), tn=FFN_DIM // 2,
                        name="ffn_in")
        x = _ffn_out(gu, x, g2, ffn_conv_w[i], ffn_conv_b[i], ffn_w_down[i].astype(BF16), final_norm,
                     final=(i == DEPTH - 1))
    return x
```

```python
import functools

import numpy as np
import jax
import jax.numpy as jnp
from jax import lax
from jax.experimental import pallas as pl
from jax.experimental.pallas import tpu as pltpu

F32 = jnp.float32
BF16 = jnp.bfloat16
HI = lax.Precision.HIGHEST

D_MODEL = 1024
DEPTH = 2
HEAD_DIM = 64
NSA_HEADS = 8
NSA_KV_HEADS = 2
NSA_HPG = NSA_HEADS // NSA_KV_HEADS
CMP_STRIDE = 16
CMP_LEN = 2 * CMP_STRIDE
CMP_HIDDEN = 128
SLC_BLOCK = 64
SLC_TOPK = 16
WINDOW = 512
Q_BLOCK = 128
FORCE = 1e4
RWKV_HEADS = 8
RWKV_DIM = RWKV_HEADS * HEAD_DIM
DECAY_LORA = 64
AAA_LORA = 64
GATE_LORA = 160
DECAY_SCALE = 0.6065306597126334
RWKV_GN_EPS = 64e-5
SGU_CHUNK = 128
SGU_GROUPS = 8
SGU_DIM = 2048
FFN_DIM = 2816
CONV_WIDTH = 3
NORM_EPS = 1e-6
LN_EPS = 1e-5
NEG_INF = -1e30
LOG2E = 1.4426950408889634
NSA_Q_DIM = NSA_HEADS * HEAD_DIM
NSA_KV_DIM = NSA_KV_HEADS * HEAD_DIM
NSA_GATE_DIM = NSA_HEADS * 3
NSA_COLS = NSA_Q_DIM + 6 * NSA_KV_DIM + NSA_GATE_DIM

LANES = 128
SUBLANES = 8
VMEM_LIMIT_BYTES = 48 * 1024 * 1024

ROW_TILE = 512
NSA_COLS_PAD = 1408
RWKV_LORA_PAD = 128
RWKV_GATE_PAD = 256
RWKV_COLS_PAD = 3 * RWKV_DIM + 2 * RWKV_LORA_PAD + RWKV_GATE_PAD
SLC_KEY_TILE = 512
NS_PAD = LANES
RWKV_CHUNK = 64
RWKV_SUB = 16
RWKV_PACK = 4
RWKV_STEP_ROWS = 512
FFN_COL_CHUNK = 256


def _cparams(*sem):
    return pltpu.CompilerParams(dimension_semantics=sem, vmem_limit_bytes=VMEM_LIMIT_BYTES)


def _gelu(x):
    return 0.5 * x * (1.0 + jnp.tanh(0.7978845608028654 * (x + 0.044715 * (x * x * x))))


def _rms(x, w):
    return x * lax.rsqrt(jnp.mean(x * x, axis=-1, keepdims=True) + NORM_EPS) * w


def _prev_rows(cur, halo, shift):
    tm = cur.shape[0]
    ext = jnp.concatenate([halo, cur], axis=0)
    return ext[SUBLANES - shift:SUBLANES - shift + tm]


def _ada_kernel(c_ref, w_ref, b_ref, o_ref):
    c = c_ref[...]
    cond = c * jax.nn.sigmoid(c)
    o_ref[0] = jnp.dot(cond, w_ref[0], precision=HI, preferred_element_type=F32) + b_ref[0]


def _ada_mod(c, ada_w, ada_b):
    B, D = c.shape
    depth, _, N = ada_w.shape
    rows = -(-B // SUBLANES) * SUBLANES
    cp = jnp.zeros((rows, D), F32).at[:B].set(c)
    tn = N // 4
    out = pl.pallas_call(
        _ada_kernel,
        grid=(depth, N // tn),
        in_specs=[pl.BlockSpec((rows, D), lambda i, j: (0, 0)),
                  pl.BlockSpec((1, D, tn), lambda i, j: (i, 0, j)),
                  pl.BlockSpec((1, 1, tn), lambda i, j: (i, 0, j))],
        out_specs=pl.BlockSpec((1, rows, tn), lambda i, j: (i, 0, j)),
        out_shape=jax.ShapeDtypeStruct((depth, rows, N), F32),
        compiler_params=_cparams("parallel", "parallel"),
        name="ada_mod",
    )(cp, ada_w, ada_b.reshape(depth, 1, N))
    return out[:, :B]


def _nm_matmul_kernel(x_ref, nw_ref, sc_ref, sh_ref, w_ref, b_ref, o_ref, h_ref, *, act):
    @pl.when(pl.program_id(2) == 0)
    def _():
        h = _rms(x_ref[0], nw_ref[...]) * (1.0 + sc_ref[0]) + sh_ref[0]
        h_ref[...] = h.astype(BF16)

    acc = jnp.dot(h_ref[...], w_ref[...], preferred_element_type=F32) + b_ref[...]
    if act:
        acc = _gelu(acc)
    o_ref[0] = acc.astype(o_ref.dtype)


def _nm_matmul(x, nw, sc, sh, w, bias, *, tn, act=False, name):
    B, T, D = x.shape
    N = w.shape[1]
    tm = min(ROW_TILE, T)
    return pl.pallas_call(
        functools.partial(_nm_matmul_kernel, act=act),
        grid=(B, T // tm, N // tn),
        in_specs=[pl.BlockSpec((1, tm, D), lambda b, i, j: (b, i, 0)),
                  pl.BlockSpec((1, D), lambda b, i, j: (0, 0)),
                  pl.BlockSpec((1, 1, D), lambda b, i, j: (b, 0, 0)),
                  pl.BlockSpec((1, 1, D), lambda b, i, j: (b, 0, 0)),
                  pl.BlockSpec((D, tn), lambda b, i, j: (0, j)),
                  pl.BlockSpec((1, tn), lambda b, i, j: (0, j))],
        out_specs=pl.BlockSpec((1, tm, tn), lambda b, i, j: (b, i, j)),
        out_shape=jax.ShapeDtypeStruct((B, T, N), F32),
        scratch_shapes=[pltpu.VMEM((tm, D), BF16)],
        compiler_params=_cparams("parallel", "parallel", "arbitrary"),
        name=name,
    )(x, nw.reshape(1, D), sc, sh, w, bias.reshape(1, N))


def _nsa_compress_kernel(x_ref, pe_ref, w1_ref, w2_ref, o_ref):
    half = CMP_STRIDE * HEAD_DIM
    x = x_ref[0, 0]
    pe = pe_ref[0]
    xa = (x + pe[:, :half]).astype(BF16)
    xb = (x + pe[:, half:]).astype(BF16)
    w1 = w1_ref[0]
    p = jnp.dot(xa, w1[:half], preferred_element_type=F32)
    q = jnp.dot(xb, w1[half:], preferred_element_type=F32)
    nc = x.shape[0]
    hid = _gelu(p + pltpu.roll(q, nc - 1, 0))
    o_ref[0, 0] = jnp.dot(hid.astype(BF16), w2_ref[0], preferred_element_type=F32)


def _nsa_compress(x16, pe, w1, w2):
    S, BG, NC, W = x16.shape
    return pl.pallas_call(
        _nsa_compress_kernel,
        grid=(S, BG),
        in_specs=[pl.BlockSpec((1, 1, NC, W), lambda s, b: (s, b, 0, 0)),
                  pl.BlockSpec((1, 1, 2 * W), lambda s, b: (s, 0, 0)),
                  pl.BlockSpec((1, 2 * W, CMP_HIDDEN), lambda s, b: (s, 0, 0)),
                  pl.BlockSpec((1, CMP_HIDDEN, HEAD_DIM), lambda s, b: (s, 0, 0))],
        out_specs=pl.BlockSpec((1, 1, NC, HEAD_DIM), lambda s, b: (s, b, 0, 0)),
        out_shape=jax.ShapeDtypeStruct((S, BG, NC, HEAD_DIM), F32),
        compiler_params=_cparams("parallel", "parallel"),
        name="nsa_compress",
    )(x16, pe, w1, w2)


def _nsa_attn_kernel(q_ref, gl_ref, kct_ref, vc_ref, kst_ref, vs_ref, kwt_ref, vw_ref, agg_ref, exp_ref,
                     o_ref, *, n_slc, top_k):
    g = pl.program_id(1)
    n = pl.program_id(2)
    Q, HP, R = Q_BLOCK, NSA_HPG, NSA_HPG * Q_BLOCK
    t0 = pl.multiple_of(n * Q, Q)
    q = q_ref[0, 0].reshape(R, HEAD_DIM)
    heads = [slice(h * Q, (h + 1) * Q) for h in range(HP)]

    slope_g = jnp.float32(1.0)
    for gi in range(1, NSA_KV_HEADS):
        slope_g = jnp.where(g == gi, jnp.float32(2.0 ** (-HP * gi)), slope_g)
    slopes = [slope_g * (LOG2E * 2.0 ** -(h + 1)) for h in range(HP)]
    tq = t0 + lax.broadcasted_iota(jnp.int32, (Q, 1), 0)

    ncp = kct_ref.shape[-1]
    s = jnp.dot(q, kct_ref[0, 0], preferred_element_type=F32)
    cmp_end = lax.broadcasted_iota(jnp.int32, (1, ncp), 1) * CMP_STRIDE + (CMP_LEN - 1)
    dist = tq - cmp_end
    valid = dist >= 0
    dmask = jnp.where(valid, -dist.astype(F32), NEG_INF)
    vc = vc_ref[0, 0]
    psum = jnp.zeros((Q, ncp), F32)
    o_c = []
    for h in range(HP):
        sh = s[heads[h]] + slopes[h] * dmask
        e = jnp.where(valid, jnp.exp2(sh - jnp.max(sh, axis=-1, keepdims=True)), 0.0)
        l = jnp.sum(e, axis=-1, keepdims=True)
        p = e / jnp.where(l > 0.0, l, 1.0)
        psum = psum + p
        o_c.append(jnp.dot(p.astype(BF16), vc, preferred_element_type=F32))

    WK = WINDOW + Q
    sw = jnp.dot(q, kwt_ref[0, 0, :, pl.ds(t0, WK)], preferred_element_type=F32)
    pos = (t0 - WINDOW) + lax.broadcasted_iota(jnp.int32, (1, WK), 1)
    dw = tq - pos
    okw = (dw >= 0) & (dw < WINDOW) & (pos >= 0)
    dmw = jnp.where(okw, -dw.astype(F32), NEG_INF)
    vwt = vw_ref[0, 0, pl.ds(t0, WK), :]
    o_w = []
    for h in range(HP):
        sh = sw[heads[h]] + slopes[h] * dmw
        e = jnp.exp2(sh - jnp.max(sh, axis=-1, keepdims=True))
        l = jnp.sum(e, axis=-1, keepdims=True)
        o_w.append(jnp.dot(e.astype(BF16), vwt, preferred_element_type=F32) / l)

    imp = jnp.dot(psum, agg_ref[...], precision=HI, preferred_element_type=F32)
    blk = lax.broadcasted_iota(jnp.int32, (1, NS_PAD), 1)
    jt = tq // SLC_BLOCK
    forced = (blk == 0) | (blk == jt) | (blk == jt - 1)
    imp = jnp.where(blk > jt, -FORCE, jnp.where(forced, FORCE, imp))
    vals = jnp.where(blk < n_slc, imp, -jnp.inf)
    blk_f = blk.astype(F32)
    sel = jnp.zeros((Q, NS_PAD), F32)
    for _ in range(top_k):
        vmax = jnp.max(vals, axis=-1, keepdims=True)
        first = jnp.min(jnp.where(vals == vmax, blk_f, float(NS_PAD)), axis=-1, keepdims=True)
        pick = blk_f == first
        sel = jnp.where(pick, 1.0, sel)
        vals = jnp.where(pick, -jnp.inf, vals)
    sel_b = sel.astype(BF16)

    TK = SLC_KEY_TILE
    n_tiles = (t0 + Q + TK - 1) // TK

    def slc_step(kt, carry):
        m_c, l_c, a_c = carry
        k0 = pl.multiple_of(kt * TK, TK)
        sc = jnp.dot(q, kst_ref[0, 0, :, pl.ds(k0, TK)], preferred_element_type=F32)
        chosen = jnp.dot(sel_b, exp_ref[:, pl.ds(k0, TK)], preferred_element_type=F32)
        rel = (k0 + lax.broadcasted_iota(jnp.int32, (1, TK), 1)) - tq
        dm = jnp.where((chosen > 0.5) & (rel <= 0), rel.astype(F32), NEG_INF)
        vt = vs_ref[0, 0, pl.ds(k0, TK), :]
        m_n, l_n, a_n = [], [], []
        for h in range(HP):
            sh = sc[heads[h]] + slopes[h] * dm
            m_new = jnp.maximum(m_c[h], jnp.max(sh, axis=-1, keepdims=True))
            alpha = jnp.exp2(m_c[h] - m_new)
            pe_ = jnp.exp2(sh - m_new)
            m_n.append(m_new)
            l_n.append(alpha * l_c[h] + jnp.sum(pe_, axis=-1, keepdims=True))
            a_n.append(alpha * a_c[h] + jnp.dot(pe_.astype(BF16), vt, preferred_element_type=F32))
        return tuple(m_n), tuple(l_n), tuple(a_n)

    init = (tuple(jnp.full((Q, 1), NEG_INF, F32) for _ in range(HP)),
            tuple(jnp.zeros((Q, 1), F32) for _ in range(HP)),
            tuple(jnp.zeros((Q, HEAD_DIM), F32) for _ in range(HP)))
    _, l_s, a_s = lax.fori_loop(0, n_tiles, slc_step, init)

    gates = jax.nn.sigmoid(gl_ref[0, 0])
    for h in range(HP):
        gh = gates[h]
        o_ref[0, 0, h] = gh[:, 0:1] * o_c[h] + gh[:, 1:2] * (a_s[h] / l_s[h]) + gh[:, 2:3] * o_w[h]


def _nsa_attention(q, gl, kct, vc, kst, vs, kwt, vw, agg, expand, *, n_slc, top_k):
    B, G, HPG, T, dk = q.shape
    ncp = kct.shape[-1]
    tw = kwt.shape[-1]
    grid = (B, G, T // Q_BLOCK)
    per_bg = lambda b, g, n: (b, g, 0, 0)
    return pl.pallas_call(
        functools.partial(_nsa_attn_kernel, n_slc=n_slc, top_k=top_k),
        grid=grid,
        in_specs=[pl.BlockSpec((1, 1, HPG, Q_BLOCK, dk), lambda b, g, n: (b, g, 0, n, 0)),
                  pl.BlockSpec((1, 1, HPG, Q_BLOCK, 3), lambda b, g, n: (b, g, 0, n, 0)),
                  pl.BlockSpec((1, 1, dk, ncp), per_bg),
                  pl.BlockSpec((1, 1, ncp, dk), per_bg),
                  pl.BlockSpec((1, 1, dk, T), per_bg),
                  pl.BlockSpec((1, 1, T, dk), per_bg),
                  pl.BlockSpec((1, 1, dk, tw), per_bg),
                  pl.BlockSpec((1, 1, tw, dk), per_bg),
                  pl.BlockSpec((ncp, NS_PAD), lambda b, g, n: (0, 0)),
                  pl.BlockSpec((NS_PAD, T), lambda b, g, n: (0, 0))],
        out_specs=pl.BlockSpec((1, 1, HPG, Q_BLOCK, dk), lambda b, g, n: (b, g, 0, n, 0)),
        out_shape=jax.ShapeDtypeStruct((B, G, HPG, T, dk), F32),
        compiler_params=_cparams("parallel", "parallel", "arbitrary"),
        name="nsa_attention",
    )(q, gl, kct, vc, kst, vs, kwt, vw, agg, expand)


def _nsa_attn_t_kernel(qt_ref, glt_ref, kc_ref, vct_ref, ks_ref, vst_ref, kw_ref, vwt_ref, aggt_ref,
                       o_ref, selt_ref, *, n_slc, top_k):
    g = pl.program_id(1)
    n = pl.program_id(2)
    Q, HP, R = Q_BLOCK, NSA_HPG, NSA_HPG * Q_BLOCK
    t0 = pl.multiple_of(n * Q, Q)
    qt = qt_ref[0, 0, 0]

    def per_head(a):
        return jnp.concatenate([a] * HP, axis=1)

    slope_g = jnp.float32(1.0)
    for gi in range(1, NSA_KV_HEADS):
        slope_g = jnp.where(g == gi, jnp.float32(2.0 ** (-HP * gi)), slope_g)
    lane_h = lax.broadcasted_iota(jnp.int32, (1, R), 1) // Q
    slope = jnp.full((1, R), LOG2E * 2.0 ** -HP, F32)
    for h in range(HP - 1):
        slope = jnp.where(lane_h == h, jnp.float32(LOG2E * 2.0 ** -(h + 1)), slope)
    slope = slope * slope_g
    tq = t0 + lax.broadcasted_iota(jnp.int32, (1, Q), 1)

    ncp = kc_ref.shape[2]
    s = jnp.dot(kc_ref[0, 0], qt, preferred_element_type=F32)
    cmp_end = lax.broadcasted_iota(jnp.int32, (ncp, 1), 0) * CMP_STRIDE + (CMP_LEN - 1)
    dist = tq - cmp_end
    bias = per_head(jnp.where(dist >= 0, -dist.astype(F32), NEG_INF))
    valid = bias > 0.5 * NEG_INF
    s = s + slope * bias
    e = jnp.where(valid, jnp.exp2(s - jnp.max(s, axis=0, keepdims=True)), 0.0)
    l = jnp.sum(e, axis=0, keepdims=True)
    p = e * (1.0 / jnp.where(l > 0.0, l, 1.0))
    o_c = jnp.dot(vct_ref[0, 0], p.astype(BF16), preferred_element_type=F32)

    WK = WINDOW + Q
    sw = jnp.dot(kw_ref[0, 0, pl.ds(t0, WK), :], qt, preferred_element_type=F32)
    pos = (t0 - WINDOW) + lax.broadcasted_iota(jnp.int32, (WK, 1), 0)
    dw = tq - pos
    okw = (dw >= 0) & (dw < WINDOW) & (pos >= 0)
    sw = sw + slope * per_head(jnp.where(okw, -dw.astype(F32), NEG_INF))
    ew = jnp.exp2(sw - jnp.max(sw, axis=0, keepdims=True))
    lw = jnp.sum(ew, axis=0, keepdims=True)
    o_w = jnp.dot(vwt_ref[0, 0, :, pl.ds(t0, WK)], ew.astype(BF16), preferred_element_type=F32) * (1.0 / lw)

    psum = p[:, 0:Q]
    for h in range(1, HP):
        psum = psum + p[:, h * Q:(h + 1) * Q]
    p_hi = psum.astype(BF16)
    p_lo = (psum - p_hi.astype(F32)).astype(BF16)
    aggt = aggt_ref[...]
    imp = jnp.dot(jnp.concatenate([aggt, aggt], axis=1), jnp.concatenate([p_hi, p_lo], axis=0),
                  preferred_element_type=F32)
    blk = lax.broadcasted_iota(jnp.int32, (NS_PAD, 1), 0)
    jt = tq // SLC_BLOCK
    forced = (blk == 0) | (blk == jt) | (blk == jt - 1)
    imp = jnp.where(blk > jt, -FORCE, jnp.where(forced, FORCE, imp))
    vals = jnp.where(blk < n_slc, imp, -jnp.inf)
    blk_f = blk.astype(F32)
    sel = jnp.zeros((NS_PAD, Q), F32)
    for _ in range(top_k):
        vmax = jnp.max(vals, axis=0, keepdims=True)
        first = jnp.min(jnp.where(vals == vmax, blk_f, float(NS_PAD)), axis=0, keepdims=True)
        pick = blk_f == first
        sel = jnp.where(pick, 1.0, sel)
        vals = jnp.where(pick, -jnp.inf, vals)
    selt_ref[...] = sel

    TK = SLC_KEY_TILE
    nb = TK // SLC_BLOCK
    n_tiles = (t0 + Q + TK - 1) // TK
    key_in_blk = lax.broadcasted_iota(jnp.int32, (SLC_BLOCK, 1), 0)

    def slc_step(kt, carry):
        m_i, l_i, acc = carry
        k0 = pl.multiple_of(kt * TK, TK)
        sc = jnp.dot(ks_ref[0, 0, pl.ds(k0, TK), :], qt, preferred_element_type=F32)
        dms = []
        for j in range(nb):
            rel = (k0 + j * SLC_BLOCK + key_in_blk) - tq
            chosen = selt_ref[pl.ds(kt * nb + j, 1), :] > 0.5
            dms.append(jnp.where(chosen & (rel <= 0), rel.astype(F32), NEG_INF))
        sc = sc + slope * per_head(jnp.concatenate(dms, axis=0))
        m_new = jnp.maximum(m_i, jnp.max(sc, axis=0, keepdims=True))
        alpha = jnp.exp2(m_i - m_new)
        pe_ = jnp.exp2(sc - m_new)
        l_new = alpha * l_i + jnp.sum(pe_, axis=0, keepdims=True)
        acc = alpha * acc + jnp.dot(vst_ref[0, 0, :, pl.ds(k0, TK)], pe_.astype(BF16),
                                    preferred_element_type=F32)
        return m_new, l_new, acc

    init = (jnp.full((1, R), NEG_INF, F32), jnp.zeros((1, R), F32), jnp.zeros((HEAD_DIM, R), F32))
    _, l_s, acc_s = lax.fori_loop(0, n_tiles, slc_step, init)

    gates = jax.nn.sigmoid(glt_ref[0, 0, 0])
    o_ref[0, 0, 0] = gates[0:1] * o_c + gates[1:2] * (acc_s * (1.0 / l_s)) + gates[2:3] * o_w


def _nsa_attention_t(qt, glt, kc, vct, ks, vst, kw, vwt, aggt, *, n_slc, top_k):
    B, G, NQ, dk, R = qt.shape
    ncp = kc.shape[2]
    T = ks.shape[2]
    tw = kw.shape[2]
    per_bg = lambda b, g, n: (b, g, 0, 0)
    per_q = lambda rows: pl.BlockSpec((1, 1, 1, rows, R), lambda b, g, n: (b, g, n, 0, 0))
    return pl.pallas_call(
        functools.partial(_nsa_attn_t_kernel, n_slc=n_slc, top_k=top_k),
        grid=(B, G, NQ),
        in_specs=[per_q(dk), per_q(3),
                  pl.BlockSpec((1, 1, ncp, dk), per_bg),
                  pl.BlockSpec((1, 1, dk, ncp), per_bg),
                  pl.BlockSpec((1, 1, T, dk), per_bg),
                  pl.BlockSpec((1, 1, dk, T), per_bg),
                  pl.BlockSpec((1, 1, tw, dk), per_bg),
                  pl.BlockSpec((1, 1, dk, tw), per_bg),
                  pl.BlockSpec((NS_PAD, ncp), lambda b, g, n: (0, 0))],
        out_specs=per_q(dk),
        out_shape=jax.ShapeDtypeStruct((B, G, NQ, dk, R), F32),
        scratch_shapes=[pltpu.VMEM((NS_PAD, Q_BLOCK), F32)],
        compiler_params=_cparams("parallel", "parallel", "arbitrary"),
        name="nsa_attention",
    )(qt, glt, kc, vct, ks, vst, kw, vwt, aggt)


def _rwkv_pre_kernel(z_ref, halo_ref, mu_ref, w0_ref, a0_ref, kk_ref, ka_ref, rk_ref, wup_ref, aup_ref, gup_ref,
                     ones_ref, ctril_ref, cones_ref,
                     kap_ref, r_ref, b_ref, k_ref, v_ref, kend_ref, bend_ref, gend_ref, bonus_ref, g_ref):
    i = pl.program_id(1)
    z = z_ref[0]
    halo = jnp.where(i > 0, halo_ref[0], 0.0)
    zs = z + (_prev_rows(z, halo, 1) - z) * mu_ref[...]
    D = RWKV_DIM
    r, k, v = zs[:, 0:D], zs[:, D:2 * D], zs[:, 2 * D:3 * D]
    o1 = 3 * D
    o2 = o1 + RWKV_LORA_PAD
    o3 = o2 + RWKV_LORA_PAD
    wl, al, gl = zs[:, o1:o2], zs[:, o2:o3], zs[:, o3:]
    dot = functools.partial(jnp.dot, precision=HI, preferred_element_type=F32)
    lw = -DECAY_SCALE * jax.nn.sigmoid(w0_ref[...] + dot(jnp.tanh(wl), wup_ref[...]))
    a = jax.nn.sigmoid(a0_ref[...] + dot(al, aup_ref[...]))
    kk = k * kk_ref[...]
    kk = kk / jnp.maximum(jnp.sqrt(dot(kk * kk, ones_ref[...])), 1e-12)
    kt = k * (1.0 + (a - 1.0) * ka_ref[...])
    b = kk * a
    cum = dot(ctril_ref[...], lw)
    tot = dot(cones_ref[...], lw)
    g_inv = jnp.exp(-cum)
    tail = jnp.exp(tot - cum)
    kap_ref[0] = kk * jnp.exp(cum - lw)
    r_ref[0] = r * jnp.exp(cum)
    b_ref[0] = b * g_inv
    k_ref[0] = kt * g_inv
    v_ref[0] = v
    kend_ref[0] = kt * tail
    bend_ref[0] = b * tail
    gend_ref[0] = jnp.exp(tot)
    bonus_ref[0] = dot(r * kt * rk_ref[...], ones_ref[...]) * v
    g_ref[0] = dot(jax.nn.sigmoid(gl), gup_ref[...])


def _rwkv_pre(z, mu, w0, a0, k_k, k_a, r_k, w_up, a_up, g_up):
    B, T, W = z.shape
    D = RWKV_DIM
    tm = min(ROW_TILE, T)
    hb = tm // SUBLANES
    ones_bd = jnp.asarray(np.kron(np.eye(RWKV_HEADS), np.ones((HEAD_DIM, HEAD_DIM))), F32)
    nch = tm // RWKV_CHUNK
    ctril = jnp.asarray(np.kron(np.eye(nch), np.tril(np.ones((RWKV_CHUNK, RWKV_CHUNK)))), F32)
    cones = jnp.asarray(np.kron(np.eye(nch), np.ones((RWKV_CHUNK, RWKV_CHUNK))), F32)
    vec = lambda n: pl.BlockSpec((1, n), lambda b, i: (0, 0))
    mat = lambda s: pl.BlockSpec(s, lambda b, i: (0, 0))
    out = jax.ShapeDtypeStruct((B, T, D), F32)
    return pl.pallas_call(
        _rwkv_pre_kernel,
        grid=(B, T // tm),
        in_specs=[pl.BlockSpec((1, tm, W), lambda b, i: (b, i, 0)),
                  pl.BlockSpec((1, SUBLANES, W), lambda b, i: (b, jnp.maximum(i * hb - 1, 0), 0)),
                  vec(W), vec(D), vec(D), vec(D), vec(D), vec(D),
                  mat((RWKV_LORA_PAD, D)), mat((RWKV_LORA_PAD, D)), mat((RWKV_GATE_PAD, D)), mat((D, D)),
                  mat((tm, tm)), mat((tm, tm))],
        out_specs=[pl.BlockSpec((1, tm, D), lambda b, i: (b, i, 0))] * 10,
        out_shape=[out] * 10,
        compiler_params=_cparams("parallel", "parallel"),
        name="rwkv_pre",
    )(z, z, mu.reshape(1, W), w0.reshape(1, D), a0.reshape(1, D), k_k.reshape(1, D), k_a.reshape(1, D),
      r_k.reshape(1, D), w_up, a_up, g_up, ones_bd, ctril, cones)


def _rwkv_core_kernel(kap_ref, r_ref, v_ref, kend_ref, bend_ref, bt_ref, kt_ref, gend_ref, o_ref, h_ref):
    C, N, P = RWKV_CHUNK, HEAD_DIM, RWKV_PACK
    W = P * N
    groups = kap_ref.shape[2] // W

    @pl.when(pl.program_id(1) == 0)
    def _():
        h_ref[...] = jnp.zeros_like(h_ref)

    ri = lax.broadcasted_iota(jnp.int32, (C, W), 0)
    cj = lax.broadcasted_iota(jnp.int32, (C, W), 1) % N
    tril = (ri >= cj).astype(F32)
    stril = (ri > cj).astype(F32)
    eye = (ri == cj).astype(F32)
    diag_blk = ((ri // RWKV_SUB) == (cj // RWKV_SUB)).astype(F32)
    same_head = (lax.broadcasted_iota(jnp.int32, (W, W), 0) // N) == (lax.broadcasted_iota(jnp.int32, (W, W), 1) // N)

    def split(x):
        hi = x.astype(BF16)
        return hi, (x - hi.astype(F32)).astype(BF16)

    def lhs3(x):
        hi, lo = split(x)
        return jnp.concatenate([hi, lo, hi], axis=1)

    def rhs3(hi, lo):
        return jnp.concatenate([hi, hi, lo], axis=0)

    def bd(x):
        hi, lo = split(x)
        blk = lambda a: jnp.where(same_head, jnp.concatenate([a] * P, axis=0), jnp.zeros((), BF16))
        return rhs3(blk(hi), blk(lo))

    def bd_t(xt):
        full = jnp.where(same_head, jnp.concatenate([xt] * P, axis=1), 0.0)
        return rhs3(*split(full))

    def mmw(x, w3):
        return jnp.dot(lhs3(x), w3, preferred_element_type=F32)

    def mmp(x, y):
        return mmw(x, bd(y))

    def gm(f, *cols):
        return [f(*args) for args in zip(*cols)]

    def chunk(c, hs):
        rows = pl.ds(pl.multiple_of(c * C, C), C)
        lanes = [slice(p * W, (p + 1) * W) for p in range(groups)]
        v = [v_ref[0, rows, ln] for ln in lanes]
        lhs = [lhs3(jnp.concatenate([kap_ref[0, rows, ln], r_ref[0, rows, ln]], axis=0)) for ln in lanes]
        ab = gm(lambda l, ln: jnp.dot(l, bd_t(bt_ref[0, c, ln, :]), preferred_element_type=F32), lhs, lanes)
        ak = gm(lambda l, ln: jnp.dot(l, bd_t(kt_ref[0, c, ln, :]), preferred_element_type=F32), lhs, lanes)
        a_kb = [stril * x[:C] for x in ab]
        a_rb = [tril * x[C:] for x in ab]
        a_kr = [jnp.concatenate([stril * x[:C], tril * x[C:]], axis=0) for x in ak]
        d = [x * diag_blk for x in a_kb]
        e = gm(lambda x, y: x - y, a_kb, d)
        d_inv = [eye - x for x in d]
        pw = gm(mmp, d, d)
        steps = int(np.log2(RWKV_SUB)) - 1
        for s_ in range(steps):
            d_inv = gm(lambda x, y: mmp(x, eye + y), d_inv, pw)
            if s_ + 1 < steps:
                pw = gm(mmp, pw, pw)
        nb = gm(mmp, d_inv, e)
        t_inv = [eye - x for x in nb]
        pw = gm(mmp, nb, nb)
        bsteps = int(np.log2(C // RWKV_SUB)) - 1
        for s_ in range(bsteps):
            t_inv = gm(lambda x, y: mmp(x, eye + y), t_inv, pw)
            if s_ + 1 < bsteps:
                pw = gm(mmp, pw, pw)
        t_inv = gm(mmp, t_inv, d_inv)
        sh = gm(lambda l, h: jnp.dot(l, bd(h), preferred_element_type=F32), lhs, hs)
        av = gm(mmp, a_kr, v)
        u = gm(lambda t, x, y: mmp(t, x[:C] + y[:C]), t_inv, sh, av)
        y = gm(lambda x, z, a, uu: x[C:] + z[C:] - mmp(a, uu), sh, av, a_rb, u)
        for ln, yy in zip(lanes, y):
            o_ref[0, rows, ln] = yy
        zero = jnp.zeros((C, W), F32)
        left = [jnp.concatenate([eye * gend_ref[0, pl.ds(c, 1), ln], kend_ref[0, rows, ln], bend_ref[0, rows, ln],
                                 zero], axis=0) for ln in lanes]
        right = gm(lambda h, vv, uu: rhs3(*split(jnp.concatenate([h, vv, -uu, zero], axis=0))), hs, v, u)
        full = gm(lambda l, rr: jnp.where(same_head, mmw(jnp.transpose(l), rr), 0.0), left, right)
        return tuple(sum(f[s_ * N:(s_ + 1) * N] for s_ in range(1, P)) + f[0:N] for f in full)

    hs = lax.fori_loop(0, kap_ref.shape[1] // C, chunk, tuple(h_ref[p] for p in range(groups)))
    for p in range(groups):
        h_ref[p] = hs[p]


def _rwkv_core(kap, r, v, kend, bend, bt, kt, gend):
    B, T, D = kap.shape
    W = RWKV_PACK * HEAD_DIM
    assert D % W == 0 and W == 4 * RWKV_CHUNK
    tc = min(RWKV_STEP_ROWS, T)
    nch = tc // RWKV_CHUNK
    seq = pl.BlockSpec((1, tc, D), lambda b, i: (b, i, 0))
    tr = pl.BlockSpec((1, nch, D, RWKV_CHUNK), lambda b, i: (b, i, 0, 0))
    return pl.pallas_call(
        _rwkv_core_kernel,
        grid=(B, T // tc),
        in_specs=[seq] * 5 + [tr] * 2 + [pl.BlockSpec((1, nch, D), lambda b, i: (b, i, 0))],
        out_specs=seq,
        out_shape=jax.ShapeDtypeStruct((B, T, D), F32),
        scratch_shapes=[pltpu.VMEM((D // W, HEAD_DIM, W), F32)],
        compiler_params=_cparams("parallel", "arbitrary"),
        name="rwkv_core",
    )(kap, r, v, kend, bend, bt, kt, gend)


def _out_proj_kernel(ya_ref, y_ref, bonus_ref, gg_ref, gw_ref, gb_ref, ones_ref, wa_ref, wb_ref, x_ref, g_ref,
                     o_ref):
    dot = functools.partial(jnp.dot, precision=HI, preferred_element_type=F32)
    y = y_ref[0]
    inv_n = 1.0 / HEAD_DIM
    dlt = y - dot(y, ones_ref[...]) * inv_n
    var = dot(dlt * dlt, ones_ref[...]) * inv_n
    yb = (dlt * lax.rsqrt(var + RWKV_GN_EPS) * gw_ref[...] + gb_ref[...] + bonus_ref[0]) * gg_ref[0]
    o = jnp.dot(ya_ref[0].astype(BF16), wa_ref[...], preferred_element_type=F32)
    o = o + jnp.dot(yb.astype(BF16), wb_ref[...], preferred_element_type=F32)
    o_ref[0] = x_ref[0] + g_ref[0] * o


def _out_proj(ya, y, bonus, gg, gn_w, gn_b, wa, wb, x, gate):
    B, T, D = x.shape
    tm = min(ROW_TILE, T)
    ka, kb = ya.shape[-1], y.shape[-1]
    ones_bd = jnp.asarray(np.kron(np.eye(kb // HEAD_DIM), np.ones((HEAD_DIM, HEAD_DIM))), F32)
    row = lambda w: pl.BlockSpec((1, tm, w), lambda b, i: (b, i, 0))
    cst = lambda s: pl.BlockSpec(s, lambda b, i: (0, 0))
    return pl.pallas_call(
        _out_proj_kernel,
        grid=(B, T // tm),
        in_specs=[row(ka), row(kb), row(kb), row(kb), cst((1, kb)), cst((1, kb)), cst((kb, kb)),
                  cst((ka, D)), cst((kb, D)), row(D),
                  pl.BlockSpec((1, 1, D), lambda b, i: (b, 0, 0))],
        out_specs=row(D),
        out_shape=jax.ShapeDtypeStruct((B, T, D), F32),
        compiler_params=_cparams("parallel", "parallel"),
        name="out_proj",
    )(ya, y, bonus, gg, gn_w.reshape(1, kb), gn_b.reshape(1, kb), ones_bd, wa, wb, x, gate)


def _sgu_out_kernel(u_ref, v_ref, x_ref, g_ref, vnw_ref, vnb_ref, ws_ref, bst_ref, wo_ref, bo_ref,
                    o_ref, gated_ref):
    v = v_ref[0]
    mu = jnp.mean(v, axis=-1, keepdims=True)
    var = jnp.mean(jnp.square(v - mu), axis=-1, keepdims=True)
    vn = ((v - mu) * lax.rsqrt(var + LN_EPS) * vnw_ref[...] + vnb_ref[...]).astype(BF16)
    S = SGU_CHUNK
    gw = SGU_DIM // SGU_GROUPS
    causal = lax.broadcasted_iota(jnp.int32, (S, S), 0) >= lax.broadcasted_iota(jnp.int32, (S, S), 1)
    for gi in range(SGU_GROUPS):
        ws = jnp.where(causal, ws_ref[gi], 0.0).astype(BF16)
        bs = bst_ref[:, gi:gi + 1]
        for n in range(v.shape[0] // S):
            sv = jnp.dot(ws, vn[n * S:(n + 1) * S, gi * gw:(gi + 1) * gw], preferred_element_type=F32) + bs
            u = u_ref[0, n * S:(n + 1) * S, gi * gw:(gi + 1) * gw]
            gated_ref[n * S:(n + 1) * S, gi * gw:(gi + 1) * gw] = (u * sv).astype(BF16)
    y = jnp.dot(gated_ref[...], wo_ref[...], preferred_element_type=F32) + bo_ref[...]
    o_ref[0] = x_ref[0] + g_ref[0] * y


def _sgu_out(z, x, gate, vn_w, vn_b, w_s, b_s_t, w_out, b_out):
    B, T, D = x.shape
    E = SGU_DIM
    tm = min(ROW_TILE, T)
    vec = lambda n: pl.BlockSpec((1, n), lambda b, i: (0, 0))
    return pl.pallas_call(
        _sgu_out_kernel,
        grid=(B, T // tm),
        in_specs=[pl.BlockSpec((1, tm, E), lambda b, i: (b, i, 0)),
                  pl.BlockSpec((1, tm, E), lambda b, i: (b, i, 1)),
                  pl.BlockSpec((1, tm, D), lambda b, i: (b, i, 0)),
                  pl.BlockSpec((1, 1, D), lambda b, i: (b, 0, 0)),
                  vec(E), vec(E),
                  pl.BlockSpec((SGU_GROUPS, SGU_CHUNK, SGU_CHUNK), lambda b, i: (0, 0, 0)),
                  pl.BlockSpec((SGU_CHUNK, SGU_GROUPS), lambda b, i: (0, 0)),
                  pl.BlockSpec((E, D), lambda b, i: (0, 0)),
                  vec(D)],
        out_specs=pl.BlockSpec((1, tm, D), lambda b, i: (b, i, 0)),
        out_shape=jax.ShapeDtypeStruct((B, T, D), F32),
        scratch_shapes=[pltpu.VMEM((tm, E), BF16)],
        compiler_params=_cparams("parallel", "parallel"),
        name="sgu_out",
    )(z, z, x, gate, vn_w.reshape(1, E), vn_b.reshape(1, E), w_s, b_s_t, w_out, b_out.reshape(1, D))


def _ffn_out_kernel(gate_ref, halo_ref, up_ref, x_ref, g_ref, cw_ref, cb_ref, wd_ref, fn_ref, o_ref, *, final):
    i = pl.program_id(1)
    tm = x_ref.shape[1]
    F = gate_ref.shape[2]
    acc = jnp.zeros((tm, x_ref.shape[2]), F32)
    for c0 in range(0, F, FFN_COL_CHUNK):
        cols = slice(c0, c0 + FFN_COL_CHUNK)
        cur = gate_ref[0, :, cols]
        halo = jnp.where(i > 0, halo_ref[0, :, cols], 0.0)
        cw = cw_ref[:, cols]
        a = (cw[0:1] * _prev_rows(cur, halo, 2) + cw[1:2] * _prev_rows(cur, halo, 1) + cw[2:3] * cur
             + cb_ref[:, cols])
        act = (_gelu(a) * up_ref[0, :, cols]).astype(BF16)
        acc = acc + jnp.dot(act, wd_ref[cols, :], preferred_element_type=F32)
    xn = x_ref[0] + g_ref[0] * acc
    if final:
        xn = _rms(xn, fn_ref[...])
    o_ref[0] = xn


def _ffn_out(gu, x, gate, conv_w, conv_b, w_down, final_w, *, final):
    B, T, D = x.shape
    F = w_down.shape[0]
    tm = min(ROW_TILE, T)
    hb = tm // SUBLANES
    return pl.pallas_call(
        functools.partial(_ffn_out_kernel, final=final),
        grid=(B, T // tm),
        in_specs=[pl.BlockSpec((1, tm, F), lambda b, i: (b, i, 0)),
                  pl.BlockSpec((1, SUBLANES, F), lambda b, i: (b, jnp.maximum(i * hb - 1, 0), 0)),
                  pl.BlockSpec((1, tm, F), lambda b, i: (b, i, 1)),
                  pl.BlockSpec((1, tm, D), lambda b, i: (b, i, 0)),
                  pl.BlockSpec((1, 1, D), lambda b, i: (b, 0, 0)),
                  pl.BlockSpec((CONV_WIDTH, F), lambda b, i: (0, 0)),
                  pl.BlockSpec((1, F), lambda b, i: (0, 0)),
                  pl.BlockSpec((F, D), lambda b, i: (0, 0)),
                  pl.BlockSpec((1, D), lambda b, i: (0, 0))],
        out_specs=pl.BlockSpec((1, tm, D), lambda b, i: (b, i, 0)),
        out_shape=jax.ShapeDtypeStruct((B, T, D), F32),
        compiler_params=_cparams("parallel", "parallel"),
        name="ffn_out",
    )(gu, gu, gu, x, gate, conv_w, conv_b.reshape(1, F), w_down, final_w.reshape(1, D))


def _pad_cols(w, sizes, padded):
    parts, o = [], 0
    for s, p in zip(sizes, padded):
        parts.append(jnp.pad(w[..., o:o + s], [(0, 0)] * (w.ndim - 1) + [(0, p - s)]))
        o += s
    return jnp.concatenate(parts, axis=-1)


def _pad_rows(w, rows):
    return jnp.pad(w, ((0, rows - w.shape[0]), (0, 0)))


def _nsa_tables(T):
    n_cmp_pad = T // CMP_STRIDE
    n_slc = T // SLC_BLOCK
    c = np.arange(n_cmp_pad)
    s = np.arange(NS_PAD)
    cs, ce, ss = c * CMP_STRIDE, c * CMP_STRIDE + CMP_LEN - 1, s * SLC_BLOCK
    agg = (cs[:, None] < ss[None, :] + SLC_BLOCK) & (ce[:, None] >= ss[None, :]) & (s[None, :] < n_slc)
    agg &= (c[:, None] < n_cmp_pad - 1)
    expand = (np.arange(T)[None, :] // SLC_BLOCK) == s[:, None]
    return jnp.asarray(agg, F32), jnp.asarray(expand, BF16)


def _nsa_rwkv_mixer(x, nw, sc, sh, gate, w_in, cmp_pe, cmp_w1, cmp_w2, mu, w0, w_up, a0, a_up, g_up,
                    k_k, k_a, r_k, gn_w, gn_b, w_out):
    B, T, D = x.shape
    G, HPG, dk = NSA_KV_HEADS, NSA_HPG, HEAD_DIM
    n_slc = T // SLC_BLOCK
    assert n_slc <= NS_PAD and T % (CMP_STRIDE * LANES) == 0

    w_nsa = jnp.pad(w_in[:, :NSA_COLS], ((0, 0), (0, NSA_COLS_PAD - NSA_COLS))).astype(BF16)
    rw_sizes = [RWKV_DIM] * 3 + [DECAY_LORA, AAA_LORA, GATE_LORA]
    rw_pads = [RWKV_DIM] * 3 + [RWKV_LORA_PAD, RWKV_LORA_PAD, RWKV_GATE_PAD]
    w_rw = _pad_cols(w_in[:, NSA_COLS:], rw_sizes, rw_pads).astype(BF16)
    z_nsa = _nm_matmul(x, nw, sc, sh, w_nsa, jnp.zeros((NSA_COLS_PAD,), F32), tn=NSA_COLS_PAD, name="in_proj_nsa")
    z_rw = _nm_matmul(x, nw, sc, sh, w_rw, jnp.zeros((RWKV_COLS_PAD,), F32), tn=RWKV_COLS_PAD // 2,
                      name="in_proj_rwkv")

    def kv(i):
        o = NSA_Q_DIM + i * NSA_KV_DIM
        return z_nsa[..., o:o + NSA_KV_DIM].reshape(B, T, G, dk).transpose(0, 2, 1, 3)

    NQ = T // Q_BLOCK

    def per_q_block(a, w):
        a = a.reshape(B, NQ, Q_BLOCK, G, HPG, w).transpose(0, 3, 1, 5, 4, 2)
        return a.reshape(B, G, NQ, w, HPG * Q_BLOCK)

    qt = per_q_block(z_nsa[..., :NSA_Q_DIM] * (dk ** -0.5 * LOG2E), dk).astype(BF16)
    glt = per_q_block(z_nsa[..., NSA_COLS - NSA_GATE_DIM:NSA_COLS], 3)
    x16 = jnp.stack([kv(0), kv(1)]).reshape(2, B * G, T // CMP_STRIDE, CMP_STRIDE * dk)
    cmp = _nsa_compress(x16, cmp_pe.reshape(2, 1, CMP_LEN * dk), cmp_w1.astype(BF16), cmp_w2.astype(BF16))
    cmp = cmp.reshape(2, B, G, T // CMP_STRIDE, dk)
    swap = lambda a: a.transpose(0, 1, 3, 2)
    front = ((0, 0), (0, 0), (WINDOW, 0), (0, 0))
    agg, _ = _nsa_tables(T)
    o_t = _nsa_attention_t(qt, glt, cmp[0].astype(BF16), swap(cmp[1]).astype(BF16),
                           kv(2).astype(BF16), swap(kv(3)).astype(BF16),
                           jnp.pad(kv(4), front).astype(BF16), swap(jnp.pad(kv(5), front)).astype(BF16),
                           agg.T.astype(BF16), n_slc=n_slc, top_k=min(SLC_TOPK, n_slc))
    y_a = o_t.reshape(B, G, NQ, dk, HPG, Q_BLOCK).transpose(0, 2, 5, 1, 4, 3).reshape(B, T, NSA_Q_DIM)

    kap, r, b_h, k_h, v, kend, bend, gend, bonus, gg = _rwkv_pre(
        z_rw, _pad_cols(mu, rw_sizes, rw_pads), w0, a0, k_k, k_a, r_k,
        _pad_rows(w_up, RWKV_LORA_PAD), _pad_rows(a_up, RWKV_LORA_PAD), _pad_rows(g_up, RWKV_GATE_PAD))
    chunk_t = lambda a: a.reshape(B, T // RWKV_CHUNK, RWKV_CHUNK, RWKV_DIM).transpose(0, 1, 3, 2)
    y_b = _rwkv_core(kap, r, v, kend, bend, chunk_t(b_h), chunk_t(k_h), gend[:, ::RWKV_CHUNK])

    w_out = w_out.astype(BF16)
    return _out_proj(y_a, y_b, bonus, gg, gn_w, gn_b, w_out[:NSA_Q_DIM], w_out[NSA_Q_DIM:], x, gate)


def kernel(x, c, ada_w, ada_b, norm_mix, norm_ffn, ffn_w_gate, ffn_w_up, ffn_conv_w, ffn_conv_b, ffn_w_down, ab_w_in, nsa_cmp_pe, nsa_cmp_w1, nsa_cmp_w2, rwkv_mu, rwkv_w0, rwkv_w_up, rwkv_a0, rwkv_a_up, rwkv_g_up, rwkv_k_k, rwkv_k_a, rwkv_r_k, rwkv_gn_w, rwkv_gn_b, ab_w_out, sgu_w_in, sgu_b_in, sgu_vn_w, sgu_vn_b, sgu_w_s, sgu_b_s, sgu_w_out, sgu_b_out, final_norm):
    B, T, D = x.shape
    mod = _ada_mod(c, ada_w, ada_b)
    for i in range(DEPTH):
        sh1, sc1, g1, sh2, sc2, g2 = [m.reshape(B, 1, D) for m in jnp.split(mod[i], 6, axis=-1)]
        j = i // 2
        if i % 2 == 0:
            x = _nsa_rwkv_mixer(x, norm_mix[i], sc1, sh1, g1, ab_w_in[j], nsa_cmp_pe[j], nsa_cmp_w1[j],
                                nsa_cmp_w2[j], rwkv_mu[j], rwkv_w0[j], rwkv_w_up[j], rwkv_a0[j], rwkv_a_up[j],
                                rwkv_g_up[j], rwkv_k_k[j], rwkv_k_a[j], rwkv_r_k[j], rwkv_gn_w[j], rwkv_gn_b[j],
                                ab_w_out[j])
        else:
            z = _nm_matmul(x, norm_mix[i], sc1, sh1, sgu_w_in[j].astype(BF16), sgu_b_in[j], tn=1024, act=True,
                           name="sgu_in")
            x = _sgu_out(z, x, g1, sgu_vn_w[j], sgu_vn_b[j], sgu_w_s[j], sgu_b_s[j].T, sgu_w_out[j].astype(BF16),
                         sgu_b_out[j])
        w_gu = jnp.concatenate([ffn_w_gate[i], ffn_w_up[i]], axis=1).astype(BF16)
        gu = _nm_matmul(x, norm_ffn[i], sc2, sh2, w_gu, jnp.zeros((2 * FFN_DIM,), F32), tn=FFN_DIM // 2,
                        name="ffn_in")
        x = _ffn_out(gu, x, g2, ffn_conv_w[i], ffn_conv_b[i], ffn_w_down[i].astype(BF16), final_norm,
                     final=(i == DEPTH - 1))
    return x
```

```python
import functools

import numpy as np
import jax
import jax.numpy as jnp
from jax import lax
from jax.experimental import pallas as pl
from jax.experimental.pallas import tpu as pltpu

F32 = jnp.float32
BF16 = jnp.bfloat16
HI = lax.Precision.HIGHEST

D_MODEL = 1024
DEPTH = 2
HEAD_DIM = 64
NSA_HEADS = 8
NSA_KV_HEADS = 2
NSA_HPG = NSA_HEADS // NSA_KV_HEADS
CMP_STRIDE = 16
CMP_LEN = 2 * CMP_STRIDE
CMP_HIDDEN = 128
SLC_BLOCK = 64
SLC_TOPK = 16
WINDOW = 512
Q_BLOCK = 128
FORCE = 1e4
RWKV_HEADS = 8
RWKV_DIM = RWKV_HEADS * HEAD_DIM
DECAY_LORA = 64
AAA_LORA = 64
GATE_LORA = 160
DECAY_SCALE = 0.6065306597126334
RWKV_GN_EPS = 64e-5
SGU_CHUNK = 128
SGU_GROUPS = 8
SGU_DIM = 2048
FFN_DIM = 2816
CONV_WIDTH = 3
NORM_EPS = 1e-6
LN_EPS = 1e-5
NEG_INF = -1e30
LOG2E = 1.4426950408889634
NSA_Q_DIM = NSA_HEADS * HEAD_DIM
NSA_KV_DIM = NSA_KV_HEADS * HEAD_DIM
NSA_GATE_DIM = NSA_HEADS * 3
NSA_COLS = NSA_Q_DIM + 6 * NSA_KV_DIM + NSA_GATE_DIM

LANES = 128
SUBLANES = 8
VMEM_LIMIT_BYTES = 52 * 1024 * 1024

ROW_TILE = 512
COL_CHUNK = 512
NSA_COLS_PAD = 1408
RWKV_LORA_PAD = 128
RWKV_GATE_PAD = 256
RWKV_COLS_PAD = 3 * RWKV_DIM + 2 * RWKV_LORA_PAD + RWKV_GATE_PAD
SLC_KEY_TILE = 256
NS_PAD = LANES
RWKV_CHUNK = 64
RWKV_SUB = 16
RWKV_PACK = 4
RWKV_STEP_ROWS = 512
FFN_COL_CHUNK = 256


def _cparams(*sem):
    return pltpu.CompilerParams(dimension_semantics=sem, vmem_limit_bytes=VMEM_LIMIT_BYTES)


def _resident(shape):
    return pl.BlockSpec(shape, lambda *_: (0,) * len(shape), pipeline_mode=pl.Buffered(1))


def _gelu(x):
    return 0.5 * x * (1.0 + jnp.tanh(0.7978845608028654 * (x + 0.044715 * (x * x * x))))


def _rms(x, w):
    return x * lax.rsqrt(jnp.mean(x * x, axis=-1, keepdims=True) + NORM_EPS) * w


def _prev_rows(cur, halo, shift):
    tm = cur.shape[0]
    ext = jnp.concatenate([halo, cur], axis=0)
    return ext[SUBLANES - shift:SUBLANES - shift + tm]


def _split3(x):
    hi = x.astype(BF16)
    r1 = x - hi.astype(F32)
    mid = r1.astype(BF16)
    return hi, mid, (r1 - mid.astype(F32)).astype(BF16)


def _dot_exact_rhs(x, m):
    return jnp.dot(jnp.concatenate(_split3(x), axis=1), jnp.concatenate([m, m, m], axis=0),
                   preferred_element_type=F32)


def _dot_exact_lhs(m, x):
    return jnp.dot(jnp.concatenate([m, m, m], axis=1), jnp.concatenate(_split3(x), axis=0),
                   preferred_element_type=F32)


def _ada_kernel(c_ref, w_ref, b_ref, o_ref):
    c = c_ref[...]
    cond = c * jax.nn.sigmoid(c)
    o_ref[0] = jnp.dot(cond, w_ref[0], precision=HI, preferred_element_type=F32) + b_ref[0]


def _ada_mod(c, ada_w, ada_b):
    B, D = c.shape
    depth, _, N = ada_w.shape
    rows = -(-B // SUBLANES) * SUBLANES
    cp = jnp.zeros((rows, D), F32).at[:B].set(c)
    tn = N // 4
    out = pl.pallas_call(
        _ada_kernel,
        grid=(depth, N // tn),
        in_specs=[pl.BlockSpec((rows, D), lambda i, j: (0, 0)),
                  pl.BlockSpec((1, D, tn), lambda i, j: (i, 0, j)),
                  pl.BlockSpec((1, 1, tn), lambda i, j: (i, 0, j))],
        out_specs=pl.BlockSpec((1, rows, tn), lambda i, j: (i, 0, j)),
        out_shape=jax.ShapeDtypeStruct((depth, rows, N), F32),
        compiler_params=_cparams("parallel", "parallel"),
        name="ada_mod",
    )(cp, ada_w, ada_b.reshape(depth, 1, N))
    return out[:, :B]


def _in_proj_kernel(x_ref, nw_ref, sc_ref, sh_ref, w_ref, oa_ref, ob_ref):
    h = (_rms(x_ref[0], nw_ref[...]) * (1.0 + sc_ref[0]) + sh_ref[0]).astype(BF16)
    na = oa_ref.shape[2]
    for o_ref, base in ((oa_ref, 0), (ob_ref, na)):
        n = o_ref.shape[2]
        for c0 in range(0, n, COL_CHUNK):
            c1 = min(c0 + COL_CHUNK, n)
            o_ref[0, :, c0:c1] = jnp.dot(h, w_ref[:, base + c0:base + c1], preferred_element_type=F32)


def _in_proj(x, nw, sc, sh, w, na):
    B, T, D = x.shape
    N = w.shape[1]
    tm = min(ROW_TILE, T)
    row = lambda n: pl.BlockSpec((1, tm, n), lambda b, i: (b, i, 0))
    per_b = pl.BlockSpec((1, 1, D), lambda b, i: (b, 0, 0))
    return pl.pallas_call(
        _in_proj_kernel,
        grid=(B, T // tm),
        in_specs=[row(D), _resident((1, D)), per_b, per_b, _resident((D, N))],
        out_specs=[row(na), row(N - na)],
        out_shape=[jax.ShapeDtypeStruct((B, T, na), F32), jax.ShapeDtypeStruct((B, T, N - na), F32)],
        compiler_params=_cparams("parallel", "parallel"),
        name="in_proj",
    )(x, nw.reshape(1, D), sc, sh, w)


def _nsa_compress_kernel(x_ref, pe_ref, w1_ref, w2_ref, o_ref):
    half = CMP_STRIDE * HEAD_DIM
    x = x_ref[0, 0]
    pe = pe_ref[0]
    xa = (x + pe[:, :half]).astype(BF16)
    xb = (x + pe[:, half:]).astype(BF16)
    w1 = w1_ref[0]
    p = jnp.dot(xa, w1[:half], preferred_element_type=F32)
    q = jnp.dot(xb, w1[half:], preferred_element_type=F32)
    nc = x.shape[0]
    hid = _gelu(p + pltpu.roll(q, nc - 1, 0))
    o_ref[0, 0] = jnp.dot(hid.astype(BF16), w2_ref[0], preferred_element_type=F32)


def _nsa_compress(x16, pe, w1, w2):
    S, BG, NC, W = x16.shape
    return pl.pallas_call(
        _nsa_compress_kernel,
        grid=(S, BG),
        in_specs=[pl.BlockSpec((1, 1, NC, W), lambda s, b: (s, b, 0, 0)),
                  pl.BlockSpec((1, 1, 2 * W), lambda s, b: (s, 0, 0)),
                  pl.BlockSpec((1, 2 * W, CMP_HIDDEN), lambda s, b: (s, 0, 0)),
                  pl.BlockSpec((1, CMP_HIDDEN, HEAD_DIM), lambda s, b: (s, 0, 0))],
        out_specs=pl.BlockSpec((1, 1, NC, HEAD_DIM), lambda s, b: (s, b, 0, 0)),
        out_shape=jax.ShapeDtypeStruct((S, BG, NC, HEAD_DIM), F32),
        compiler_params=_cparams("parallel", "parallel"),
        name="nsa_compress",
    )(x16, pe, w1, w2)


def _nsa_attn_kernel(qt_ref, glt_ref, kc_ref, vct_ref, ks_ref, vst_ref, kw_ref, vwt_ref, aggt_ref,
                     o_ref, selt_ref, sa_ref, sb_ref, pa_ref, pb_ref, *, n_slc, top_k):
    g = pl.program_id(1)
    n = pl.program_id(2)
    Q, HP, R = Q_BLOCK, NSA_HPG, NSA_HPG * Q_BLOCK
    t0 = pl.multiple_of(n * Q, Q)
    qt = qt_ref[0, 0, 0]

    def per_head(a):
        return jnp.concatenate([a] * HP, axis=1)

    slope_g = jnp.float32(1.0)
    for gi in range(1, NSA_KV_HEADS):
        slope_g = jnp.where(g == gi, jnp.float32(2.0 ** (-HP * gi)), slope_g)
    lane_h = lax.broadcasted_iota(jnp.int32, (1, R), 1) // Q
    slope = jnp.full((1, R), LOG2E * 2.0 ** -HP, F32)
    for h in range(HP - 1):
        slope = jnp.where(lane_h == h, jnp.float32(LOG2E * 2.0 ** -(h + 1)), slope)
    slope = slope * slope_g
    tq = t0 + lax.broadcasted_iota(jnp.int32, (1, Q), 1)

    ncp = kc_ref.shape[2]
    s = jnp.dot(kc_ref[0, 0], qt, preferred_element_type=F32)
    cmp_end = lax.broadcasted_iota(jnp.int32, (ncp, 1), 0) * CMP_STRIDE + (CMP_LEN - 1)
    dist = tq - cmp_end
    bias = per_head(jnp.where(dist >= 0, -dist.astype(F32), NEG_INF))
    valid = bias > 0.5 * NEG_INF
    s = s + slope * bias
    e = jnp.where(valid, jnp.exp2(s - jnp.max(s, axis=0, keepdims=True)), 0.0)
    l = jnp.sum(e, axis=0, keepdims=True)
    p = e * (1.0 / jnp.where(l > 0.0, l, 1.0))
    o_c = jnp.dot(vct_ref[0, 0], p.astype(BF16), preferred_element_type=F32)

    WK = WINDOW + Q
    sw = jnp.dot(kw_ref[0, 0, pl.ds(t0, WK), :], qt, preferred_element_type=F32)
    pos = (t0 - WINDOW) + lax.broadcasted_iota(jnp.int32, (WK, 1), 0)
    dw = tq - pos
    okw = (dw >= 0) & (dw < WINDOW) & (pos >= 0)
    sw = sw + slope * per_head(jnp.where(okw, -dw.astype(F32), NEG_INF))
    ew = jnp.exp2(sw - jnp.max(sw, axis=0, keepdims=True))
    lw = jnp.sum(ew, axis=0, keepdims=True)
    o_w = jnp.dot(vwt_ref[0, 0, :, pl.ds(t0, WK)], ew.astype(BF16), preferred_element_type=F32) * (1.0 / lw)

    psum = p[:, 0:Q]
    for h in range(1, HP):
        psum = psum + p[:, h * Q:(h + 1) * Q]
    p_hi = psum.astype(BF16)
    p_lo = (psum - p_hi.astype(F32)).astype(BF16)
    aggt = aggt_ref[...]
    imp = jnp.dot(jnp.concatenate([aggt, aggt], axis=1), jnp.concatenate([p_hi, p_lo], axis=0),
                  preferred_element_type=F32)
    blk = lax.broadcasted_iota(jnp.int32, (NS_PAD, 1), 0)
    jt = tq // SLC_BLOCK
    forced = (blk == 0) | (blk == jt) | (blk == jt - 1)
    imp = jnp.where(blk > jt, -FORCE, jnp.where(forced, FORCE, imp))
    vals = jnp.where(blk < n_slc, imp, -jnp.inf)
    blk_f = blk.astype(F32)
    sel = jnp.zeros((NS_PAD, Q), F32)
    for _ in range(top_k):
        vmax = jnp.max(vals, axis=0, keepdims=True)
        first = jnp.min(jnp.where(vals == vmax, blk_f, float(NS_PAD)), axis=0, keepdims=True)
        pick = blk_f == first
        sel = jnp.where(pick, 1.0, sel)
        vals = jnp.where(pick, -jnp.inf, vals)
    selt_ref[...] = sel

    TK = SLC_KEY_TILE
    nb = TK // SLC_BLOCK
    n_tiles = (t0 + Q + TK - 1) // TK
    last_tile = ks_ref.shape[2] // TK - 1
    key_in_blk = lax.broadcasted_iota(jnp.int32, (SLC_BLOCK, 1), 0)

    def qk(kt, s_ref):
        k0 = pl.multiple_of(jnp.minimum(kt, last_tile) * TK, TK)
        s_ref[...] = jnp.dot(ks_ref[0, 0, pl.ds(k0, TK), :], qt, preferred_element_type=F32)

    def pv(kt, p_ref):
        k0 = pl.multiple_of(jnp.clip(kt, 0, last_tile) * TK, TK)
        return jnp.dot(vst_ref[0, 0, :, pl.ds(k0, TK)], p_ref[...], preferred_element_type=F32)

    def softmax_tile(kt, s_ref, p_ref, m_i, l_i):
        dms = []
        for j in range(nb):
            rel = (kt * TK + j * SLC_BLOCK + key_in_blk) - tq
            chosen = selt_ref[pl.ds(jnp.minimum(kt * nb + j, NS_PAD - 1), 1), :] > 0.5
            dms.append(jnp.where(chosen & (rel <= 0), rel.astype(F32), NEG_INF))
        sc = s_ref[...] + slope * per_head(jnp.concatenate(dms, axis=0))
        m_new = jnp.maximum(m_i, jnp.max(sc, axis=0, keepdims=True))
        alpha = jnp.exp2(m_i - m_new)
        pe_ = jnp.exp2(sc - m_new)
        p_ref[...] = pe_.astype(BF16)
        return m_new, alpha, alpha * l_i + jnp.sum(pe_, axis=0, keepdims=True)

    def slc_pair(j, carry):
        m_i, l_i, acc = carry
        ka = 2 * j
        qk(ka + 1, sb_ref)
        acc = acc + pv(ka - 1, pb_ref)
        m_i, alpha, l_i = softmax_tile(ka, sa_ref, pa_ref, m_i, l_i)
        acc = alpha * acc
        qk(ka + 2, sa_ref)
        acc = acc + pv(ka, pa_ref)
        m_i, alpha, l_i = softmax_tile(ka + 1, sb_ref, pb_ref, m_i, l_i)
        return m_i, l_i, alpha * acc

    pb_ref[...] = jnp.zeros_like(pb_ref)
    qk(0, sa_ref)
    n_pairs = (n_tiles + 1) // 2
    init = (jnp.full((1, R), NEG_INF, F32), jnp.zeros((1, R), F32), jnp.zeros((HEAD_DIM, R), F32))
    _, l_s, acc_s = lax.fori_loop(0, n_pairs, slc_pair, init)
    acc_s = acc_s + pv(2 * n_pairs - 1, pb_ref)

    gates = jax.nn.sigmoid(glt_ref[0, 0, 0])
    o_ref[0, 0, 0] = gates[0:1] * o_c + gates[1:2] * (acc_s * (1.0 / l_s)) + gates[2:3] * o_w


def _nsa_attention(qt, glt, kc, vct, ks, vst, kw, vwt, aggt, *, n_slc, top_k):
    B, G, NQ, dk, R = qt.shape
    ncp = kc.shape[2]
    T = ks.shape[2]
    tw = kw.shape[2]
    per_bg = lambda b, g, n: (b, g, 0, 0)
    per_q = lambda rows: pl.BlockSpec((1, 1, 1, rows, R), lambda b, g, n: (b, g, n, 0, 0))
    return pl.pallas_call(
        functools.partial(_nsa_attn_kernel, n_slc=n_slc, top_k=top_k),
        grid=(B, G, NQ),
        in_specs=[per_q(dk), per_q(3),
                  pl.BlockSpec((1, 1, ncp, dk), per_bg),
                  pl.BlockSpec((1, 1, dk, ncp), per_bg),
                  pl.BlockSpec((1, 1, T, dk), per_bg),
                  pl.BlockSpec((1, 1, dk, T), per_bg),
                  pl.BlockSpec((1, 1, tw, dk), per_bg),
                  pl.BlockSpec((1, 1, dk, tw), per_bg),
                  _resident((NS_PAD, ncp))],
        out_specs=per_q(dk),
        out_shape=jax.ShapeDtypeStruct((B, G, NQ, dk, R), F32),
        scratch_shapes=[pltpu.VMEM((NS_PAD, Q_BLOCK), F32),
                        pltpu.VMEM((SLC_KEY_TILE, R), F32), pltpu.VMEM((SLC_KEY_TILE, R), F32),
                        pltpu.VMEM((SLC_KEY_TILE, R), BF16), pltpu.VMEM((SLC_KEY_TILE, R), BF16)],
        compiler_params=_cparams("parallel", "parallel", "arbitrary"),
        name="nsa_attention",
    )(qt, glt, kc, vct, ks, vst, kw, vwt, aggt)


def _rwkv_pre_kernel(z_ref, halo_ref, mu_ref, w0_ref, a0_ref, kk_ref, ka_ref, rk_ref, wup_ref, aup_ref, gup_ref,
                     ones_ref, csum_ref,
                     kap_ref, r_ref, b_ref, k_ref, v_ref, kend_ref, bend_ref, gend_ref, bonus_ref, g_ref):
    i = pl.program_id(1)
    z = z_ref[0]
    tm = z.shape[0]
    halo = jnp.where(i > 0, halo_ref[0], 0.0)
    zs = z + (_prev_rows(z, halo, 1) - z) * mu_ref[...]
    D = RWKV_DIM
    r, k, v = zs[:, 0:D], zs[:, D:2 * D], zs[:, 2 * D:3 * D]
    o1 = 3 * D
    o2 = o1 + RWKV_LORA_PAD
    o3 = o2 + RWKV_LORA_PAD
    wl, al, gl = zs[:, o1:o2], zs[:, o2:o3], zs[:, o3:]
    dot = functools.partial(jnp.dot, precision=HI, preferred_element_type=F32)
    lw = -DECAY_SCALE * jax.nn.sigmoid(w0_ref[...] + dot(jnp.tanh(wl), wup_ref[...]))
    a = jax.nn.sigmoid(a0_ref[...] + dot(al, aup_ref[...]))
    kt = k * (1.0 + (a - 1.0) * ka_ref[...])
    kk = k * kk_ref[...]
    head_sums = _dot_exact_rhs(jnp.concatenate([kk * kk, r * kt * rk_ref[...]], axis=0), ones_ref[...])
    kk = kk / jnp.maximum(jnp.sqrt(head_sums[:tm]), 1e-12)
    b = kk * a
    sums = _dot_exact_lhs(csum_ref[...], lw)
    cum, tot = sums[:tm], sums[tm:]
    g_inv = jnp.exp(-cum)
    tail = jnp.exp(tot - cum)
    kap_ref[0] = kk * jnp.exp(cum - lw)
    r_ref[0] = r * jnp.exp(cum)
    b_ref[0] = b * g_inv
    k_ref[0] = kt * g_inv
    v_ref[0] = v
    kend_ref[0] = kt * tail
    bend_ref[0] = b * tail
    gend_ref[0] = jnp.exp(tot)
    bonus_ref[0] = head_sums[tm:] * v
    g_ref[0] = dot(jax.nn.sigmoid(gl), gup_ref[...])


def _rwkv_pre(z, mu, w0, a0, k_k, k_a, r_k, w_up, a_up, g_up):
    B, T, W = z.shape
    D = RWKV_DIM
    tm = min(ROW_TILE, T)
    hb = tm // SUBLANES
    C = RWKV_CHUNK
    ones_bd = jnp.asarray(np.kron(np.eye(RWKV_HEADS), np.ones((HEAD_DIM, HEAD_DIM))), BF16)
    chunks = np.eye(tm // C)
    csum = jnp.asarray(np.concatenate([np.kron(chunks, np.tril(np.ones((C, C)))),
                                       np.kron(chunks, np.ones((C, C)))], axis=0), BF16)
    out = jax.ShapeDtypeStruct((B, T, D), F32)
    return pl.pallas_call(
        _rwkv_pre_kernel,
        grid=(B, T // tm),
        in_specs=[pl.BlockSpec((1, tm, W), lambda b, i: (b, i, 0)),
                  pl.BlockSpec((1, SUBLANES, W), lambda b, i: (b, jnp.maximum(i * hb - 1, 0), 0)),
                  _resident((1, W)), _resident((1, D)), _resident((1, D)), _resident((1, D)), _resident((1, D)),
                  _resident((1, D)), _resident((RWKV_LORA_PAD, D)), _resident((RWKV_LORA_PAD, D)),
                  _resident((RWKV_GATE_PAD, D)), _resident((D, D)), _resident((2 * tm, tm))],
        out_specs=[pl.BlockSpec((1, tm, D), lambda b, i: (b, i, 0))] * 10,
        out_shape=[out] * 10,
        compiler_params=_cparams("parallel", "parallel"),
        name="rwkv_pre",
    )(z, z, mu.reshape(1, W), w0.reshape(1, D), a0.reshape(1, D), k_k.reshape(1, D), k_a.reshape(1, D),
      r_k.reshape(1, D), w_up, a_up, g_up, ones_bd, csum)


def _rwkv_core_kernel(kap_ref, r_ref, v_ref, kend_ref, bend_ref, bt_ref, kt_ref, gend_ref, o_ref, h_ref):
    C, N, P = RWKV_CHUNK, HEAD_DIM, RWKV_PACK
    W = P * N
    groups = kap_ref.shape[2] // W

    @pl.when(pl.program_id(1) == 0)
    def _():
        h_ref[...] = jnp.zeros_like(h_ref)

    ri = lax.broadcasted_iota(jnp.int32, (C, W), 0)
    cj = lax.broadcasted_iota(jnp.int32, (C, W), 1) % N
    tril = (ri >= cj).astype(F32)
    stril = (ri > cj).astype(F32)
    eye = (ri == cj).astype(F32)
    diag_blk = ((ri // RWKV_SUB) == (cj // RWKV_SUB)).astype(F32)
    same_head = (lax.broadcasted_iota(jnp.int32, (W, W), 0) // N) == (lax.broadcasted_iota(jnp.int32, (W, W), 1) // N)

    def split(x):
        hi = x.astype(BF16)
        return hi, (x - hi.astype(F32)).astype(BF16)

    def lhs3(x):
        hi, lo = split(x)
        return jnp.concatenate([hi, lo, hi], axis=1)

    def rhs3(hi, lo):
        return jnp.concatenate([hi, hi, lo], axis=0)

    def bd(x):
        hi, lo = split(x)
        blk = lambda a: jnp.where(same_head, jnp.concatenate([a] * P, axis=0), jnp.zeros((), BF16))
        return rhs3(blk(hi), blk(lo))

    def bd_t(xt):
        full = jnp.where(same_head, jnp.concatenate([xt] * P, axis=1), 0.0)
        return rhs3(*split(full))

    def mmw(x, w3):
        return jnp.dot(lhs3(x), w3, preferred_element_type=F32)

    def mmp(x, y):
        return mmw(x, bd(y))

    def gm(f, *cols):
        return [f(*args) for args in zip(*cols)]

    def chunk(c, hs):
        rows = pl.ds(pl.multiple_of(c * C, C), C)
        lanes = [slice(p * W, (p + 1) * W) for p in range(groups)]
        v = [v_ref[0, rows, ln] for ln in lanes]
        lhs = [lhs3(jnp.concatenate([kap_ref[0, rows, ln], r_ref[0, rows, ln]], axis=0)) for ln in lanes]
        ab = gm(lambda l, ln: jnp.dot(l, bd_t(bt_ref[0, c, ln, :]), preferred_element_type=F32), lhs, lanes)
        ak = gm(lambda l, ln: jnp.dot(l, bd_t(kt_ref[0, c, ln, :]), preferred_element_type=F32), lhs, lanes)
        a_kb = [stril * x[:C] for x in ab]
        a_rb = [tril * x[C:] for x in ab]
        a_kr = [jnp.concatenate([stril * x[:C], tril * x[C:]], axis=0) for x in ak]
        d = [x * diag_blk for x in a_kb]
        e = gm(lambda x, y: x - y, a_kb, d)
        d_inv = [eye - x for x in d]
        pw = gm(mmp, d, d)
        steps = int(np.log2(RWKV_SUB)) - 1
        for s_ in range(steps):
            d_inv = gm(lambda x, y: mmp(x, eye + y), d_inv, pw)
            if s_ + 1 < steps:
                pw = gm(mmp, pw, pw)
        nb = gm(mmp, d_inv, e)
        t_inv = [eye - x for x in nb]
        pw = gm(mmp, nb, nb)
        bsteps = int(np.log2(C // RWKV_SUB)) - 1
        for s_ in range(bsteps):
            t_inv = gm(lambda x, y: mmp(x, eye + y), t_inv, pw)
            if s_ + 1 < bsteps:
                pw = gm(mmp, pw, pw)
        t_inv = gm(mmp, t_inv, d_inv)
        sh = gm(lambda l, h: jnp.dot(l, bd(h), preferred_element_type=F32), lhs, hs)
        av = gm(mmp, a_kr, v)
        u = gm(lambda t, x, y: mmp(t, x[:C] + y[:C]), t_inv, sh, av)
        y = gm(lambda x, z, a, uu: x[C:] + z[C:] - mmp(a, uu), sh, av, a_rb, u)
        for ln, yy in zip(lanes, y):
            o_ref[0, rows, ln] = yy
        zero = jnp.zeros((C, W), F32)
        left = [jnp.concatenate([eye * gend_ref[0, pl.ds(c, 1), ln], kend_ref[0, rows, ln], bend_ref[0, rows, ln],
                                 zero], axis=0) for ln in lanes]
        right = gm(lambda h, vv, uu: rhs3(*split(jnp.concatenate([h, vv, -uu, zero], axis=0))), hs, v, u)
        full = gm(lambda l, rr: jnp.where(same_head, mmw(jnp.transpose(l), rr), 0.0), left, right)
        return tuple(sum(f[s_ * N:(s_ + 1) * N] for s_ in range(1, P)) + f[0:N] for f in full)

    hs = lax.fori_loop(0, kap_ref.shape[1] // C, chunk, tuple(h_ref[p] for p in range(groups)))
    for p in range(groups):
        h_ref[p] = hs[p]


def _rwkv_core(kap, r, v, kend, bend, bt, kt, gend):
    B, T, D = kap.shape
    W = RWKV_PACK * HEAD_DIM
    assert D % W == 0 and W == 4 * RWKV_CHUNK
    tc = min(RWKV_STEP_ROWS, T)
    nch = tc // RWKV_CHUNK
    seq = pl.BlockSpec((1, tc, D), lambda b, i: (b, i, 0))
    tr = pl.BlockSpec((1, nch, D, RWKV_CHUNK), lambda b, i: (b, i, 0, 0))
    return pl.pallas_call(
        _rwkv_core_kernel,
        grid=(B, T // tc),
        in_specs=[seq] * 5 + [tr] * 2 + [pl.BlockSpec((1, nch, D), lambda b, i: (b, i, 0))],
        out_specs=seq,
        out_shape=jax.ShapeDtypeStruct((B, T, D), F32),
        scratch_shapes=[pltpu.VMEM((D // W, HEAD_DIM, W), F32)],
        compiler_params=_cparams("parallel", "arbitrary"),
        name="rwkv_core",
    )(kap, r, v, kend, bend, bt, kt, gend)


def _out_proj_kernel(ya_ref, y_ref, bonus_ref, gg_ref, gw_ref, gb_ref, ones_ref, wa_ref, wb_ref, x_ref, g_ref,
                     o_ref):
    y = y_ref[0]
    inv_n = 1.0 / HEAD_DIM
    dlt = y - _dot_exact_rhs(y, ones_ref[...]) * inv_n
    var = _dot_exact_rhs(dlt * dlt, ones_ref[...]) * inv_n
    yb = (dlt * lax.rsqrt(var + RWKV_GN_EPS) * gw_ref[...] + gb_ref[...] + bonus_ref[0]) * gg_ref[0]
    o = jnp.dot(ya_ref[0].astype(BF16), wa_ref[...], preferred_element_type=F32)
    o = o + jnp.dot(yb.astype(BF16), wb_ref[...], preferred_element_type=F32)
    o_ref[0] = x_ref[0] + g_ref[0] * o


def _out_proj(ya, y, bonus, gg, gn_w, gn_b, wa, wb, x, gate):
    B, T, D = x.shape
    tm = min(ROW_TILE, T)
    ka, kb = ya.shape[-1], y.shape[-1]
    ones_bd = jnp.asarray(np.kron(np.eye(kb // HEAD_DIM), np.ones((HEAD_DIM, HEAD_DIM))), BF16)
    row = lambda w: pl.BlockSpec((1, tm, w), lambda b, i: (b, i, 0))
    return pl.pallas_call(
        _out_proj_kernel,
        grid=(B, T // tm),
        in_specs=[row(ka), row(kb), row(kb), row(kb), _resident((1, kb)), _resident((1, kb)), _resident((kb, kb)),
                  _resident((ka, D)), _resident((kb, D)), row(D),
                  pl.BlockSpec((1, 1, D), lambda b, i: (b, 0, 0))],
        out_specs=row(D),
        out_shape=jax.ShapeDtypeStruct((B, T, D), F32),
        compiler_params=_cparams("parallel", "parallel"),
        name="out_proj",
    )(ya, y, bonus, gg, gn_w.reshape(1, kb), gn_b.reshape(1, kb), ones_bd, wa, wb, x, gate)


def _sgu_kernel(x_ref, nw_ref, sc_ref, sh_ref, g_ref, wi_ref, bi_ref, vnw_ref, vnb_ref, ws_ref, bst_ref,
                wo_ref, bo_ref, o_ref, u_ref, v_ref, gated_ref):
    x = x_ref[0]
    tm = x.shape[0]
    E = SGU_DIM
    h = (_rms(x, nw_ref[...]) * (1.0 + sc_ref[0]) + sh_ref[0]).astype(BF16)
    for c0 in range(0, 2 * E, COL_CHUNK):
        zc = _gelu(jnp.dot(h, wi_ref[:, c0:c0 + COL_CHUNK], preferred_element_type=F32) + bi_ref[:, c0:c0 + COL_CHUNK])
        if c0 < E:
            u_ref[:, c0:c0 + COL_CHUNK] = zc
        else:
            v_ref[:, c0 - E:c0 - E + COL_CHUNK] = zc
    v = v_ref[...]
    mu = jnp.mean(v, axis=-1, keepdims=True)
    var = jnp.mean(jnp.square(v - mu), axis=-1, keepdims=True)
    vn = ((v - mu) * lax.rsqrt(var + LN_EPS) * vnw_ref[...] + vnb_ref[...]).astype(BF16)
    S = SGU_CHUNK
    gw = E // SGU_GROUPS
    causal = lax.broadcasted_iota(jnp.int32, (S, S), 0) >= lax.broadcasted_iota(jnp.int32, (S, S), 1)
    for gi in range(SGU_GROUPS):
        ws = jnp.where(causal, ws_ref[gi], 0.0).astype(BF16)
        bs = bst_ref[:, gi:gi + 1]
        for n in range(tm // S):
            sv = jnp.dot(ws, vn[n * S:(n + 1) * S, gi * gw:(gi + 1) * gw], preferred_element_type=F32) + bs
            u = u_ref[n * S:(n + 1) * S, gi * gw:(gi + 1) * gw]
            gated_ref[n * S:(n + 1) * S, gi * gw:(gi + 1) * gw] = (u * sv).astype(BF16)
    y = jnp.dot(gated_ref[...], wo_ref[...], preferred_element_type=F32) + bo_ref[...]
    o_ref[0] = x + g_ref[0] * y


def _sgu(x, nw, sc, sh, gate, w_in, b_in, vn_w, vn_b, w_s, b_s_t, w_out, b_out):
    B, T, D = x.shape
    E = SGU_DIM
    tm = min(ROW_TILE, T)
    assert tm % SGU_CHUNK == 0
    row = pl.BlockSpec((1, tm, D), lambda b, i: (b, i, 0))
    per_b = pl.BlockSpec((1, 1, D), lambda b, i: (b, 0, 0))
    return pl.pallas_call(
        _sgu_kernel,
        grid=(B, T // tm),
        in_specs=[row, _resident((1, D)), per_b, per_b, per_b,
                  _resident((D, 2 * E)), _resident((1, 2 * E)), _resident((1, E)), _resident((1, E)),
                  _resident((SGU_GROUPS, SGU_CHUNK, SGU_CHUNK)), _resident((SGU_CHUNK, SGU_GROUPS)),
                  _resident((E, D)), _resident((1, D))],
        out_specs=row,
        out_shape=jax.ShapeDtypeStruct((B, T, D), F32),
        scratch_shapes=[pltpu.VMEM((tm, E), F32), pltpu.VMEM((tm, E), F32), pltpu.VMEM((tm, E), BF16)],
        compiler_params=_cparams("parallel", "parallel"),
        name="sgu",
    )(x, nw.reshape(1, D), sc, sh, gate, w_in, b_in.reshape(1, 2 * E), vn_w.reshape(1, E), vn_b.reshape(1, E),
      w_s, b_s_t, w_out, b_out.reshape(1, D))


def _ffn_kernel(x_ref, halo_ref, nw_ref, sc_ref, sh_ref, g_ref, wg_ref, wu_ref, cw_ref, cb_ref, wd_ref, fn_ref,
                o_ref, *, final):
    i = pl.program_id(1)
    x = x_ref[0]
    tm = x.shape[0]
    F = wg_ref.shape[1]
    mod = lambda a: (_rms(a, nw_ref[...]) * (1.0 + sc_ref[0]) + sh_ref[0]).astype(BF16)
    h = mod(x)
    h_halo = mod(halo_ref[0])
    acc = jnp.zeros((tm, x.shape[1]), F32)
    for c0 in range(0, F, FFN_COL_CHUNK):
        cols = slice(c0, c0 + FFN_COL_CHUNK)
        wg = wg_ref[:, cols]
        cur = jnp.dot(h, wg, preferred_element_type=F32)
        halo = jnp.where(i > 0, jnp.dot(h_halo, wg, preferred_element_type=F32), 0.0)
        cw = cw_ref[:, cols]
        a = (cw[0:1] * _prev_rows(cur, halo, 2) + cw[1:2] * _prev_rows(cur, halo, 1) + cw[2:3] * cur
             + cb_ref[:, cols])
        up = jnp.dot(h, wu_ref[:, cols], preferred_element_type=F32)
        acc = acc + jnp.dot((_gelu(a) * up).astype(BF16), wd_ref[cols, :], preferred_element_type=F32)
    xn = x + g_ref[0] * acc
    if final:
        xn = _rms(xn, fn_ref[...])
    o_ref[0] = xn


def _ffn(x, nw, sc, sh, gate, w_gate, w_up, conv_w, conv_b, w_down, final_w, *, final):
    B, T, D = x.shape
    F = w_down.shape[0]
    tm = min(ROW_TILE, T)
    hb = tm // SUBLANES
    row = pl.BlockSpec((1, tm, D), lambda b, i: (b, i, 0))
    per_b = pl.BlockSpec((1, 1, D), lambda b, i: (b, 0, 0))
    return pl.pallas_call(
        functools.partial(_ffn_kernel, final=final),
        grid=(B, T // tm),
        in_specs=[row,
                  pl.BlockSpec((1, SUBLANES, D), lambda b, i: (b, jnp.maximum(i * hb - 1, 0), 0)),
                  _resident((1, D)), per_b, per_b, per_b,
                  _resident((D, F)), _resident((D, F)), _resident((CONV_WIDTH, F)), _resident((1, F)),
                  _resident((F, D)), _resident((1, D))],
        out_specs=row,
        out_shape=jax.ShapeDtypeStruct((B, T, D), F32),
        compiler_params=_cparams("parallel", "parallel"),
        name="ffn",
    )(x, x, nw.reshape(1, D), sc, sh, gate, w_gate, w_up, conv_w, conv_b.reshape(1, F), w_down,
      final_w.reshape(1, D))


def _pad_cols(w, sizes, padded):
    parts, o = [], 0
    for s, p in zip(sizes, padded):
        parts.append(jnp.pad(w[..., o:o + s], [(0, 0)] * (w.ndim - 1) + [(0, p - s)]))
        o += s
    return jnp.concatenate(parts, axis=-1)


def _pad_rows(w, rows):
    return jnp.pad(w, ((0, rows - w.shape[0]), (0, 0)))


def _nsa_agg_t(T):
    n_cmp_pad = T // CMP_STRIDE
    n_slc = T // SLC_BLOCK
    c = np.arange(n_cmp_pad)
    s = np.arange(NS_PAD)
    cs, ce, ss = c * CMP_STRIDE, c * CMP_STRIDE + CMP_LEN - 1, s * SLC_BLOCK
    agg = (cs[None, :] < ss[:, None] + SLC_BLOCK) & (ce[None, :] >= ss[:, None]) & (s[:, None] < n_slc)
    agg &= (c[None, :] < n_cmp_pad - 1)
    return jnp.asarray(agg, BF16)


def _nsa_rwkv_mixer(x, nw, sc, sh, gate, w_in, cmp_pe, cmp_w1, cmp_w2, mu, w0, w_up, a0, a_up, g_up,
                    k_k, k_a, r_k, gn_w, gn_b, w_out):
    B, T, D = x.shape
    G, HPG, dk = NSA_KV_HEADS, NSA_HPG, HEAD_DIM
    n_slc = T // SLC_BLOCK
    NQ = T // Q_BLOCK
    assert n_slc <= NS_PAD and T % (CMP_STRIDE * LANES) == 0 and T % (2 * SLC_KEY_TILE) == 0

    rw_sizes = [RWKV_DIM] * 3 + [DECAY_LORA, AAA_LORA, GATE_LORA]
    rw_pads = [RWKV_DIM] * 3 + [RWKV_LORA_PAD, RWKV_LORA_PAD, RWKV_GATE_PAD]
    w_all = jnp.concatenate([jnp.pad(w_in[:, :NSA_COLS], ((0, 0), (0, NSA_COLS_PAD - NSA_COLS))),
                             _pad_cols(w_in[:, NSA_COLS:], rw_sizes, rw_pads)], axis=1).astype(BF16)
    z_nsa, z_rw = _in_proj(x, nw, sc, sh, w_all, NSA_COLS_PAD)

    def kv(i):
        o = NSA_Q_DIM + i * NSA_KV_DIM
        return z_nsa[..., o:o + NSA_KV_DIM].reshape(B, T, G, dk).transpose(0, 2, 1, 3)

    def per_q_block(a, w):
        a = a.reshape(B, NQ, Q_BLOCK, G, HPG, w).transpose(0, 3, 1, 5, 4, 2)
        return a.reshape(B, G, NQ, w, HPG * Q_BLOCK)

    qt = per_q_block(z_nsa[..., :NSA_Q_DIM] * (dk ** -0.5 * LOG2E), dk).astype(BF16)
    glt = per_q_block(z_nsa[..., NSA_COLS - NSA_GATE_DIM:NSA_COLS], 3)
    x16 = jnp.stack([kv(0), kv(1)]).reshape(2, B * G, T // CMP_STRIDE, CMP_STRIDE * dk)
    cmp = _nsa_compress(x16, cmp_pe.reshape(2, 1, CMP_LEN * dk), cmp_w1.astype(BF16), cmp_w2.astype(BF16))
    cmp = cmp.reshape(2, B, G, T // CMP_STRIDE, dk)
    swap = lambda a: a.transpose(0, 1, 3, 2)
    front = ((0, 0), (0, 0), (WINDOW, 0), (0, 0))
    o_t = _nsa_attention(qt, glt, cmp[0].astype(BF16), swap(cmp[1]).astype(BF16),
                         kv(2).astype(BF16), swap(kv(3)).astype(BF16),
                         jnp.pad(kv(4), front).astype(BF16), swap(jnp.pad(kv(5), front)).astype(BF16),
                         _nsa_agg_t(T), n_slc=n_slc, top_k=min(SLC_TOPK, n_slc))
    y_a = o_t.reshape(B, G, NQ, dk, HPG, Q_BLOCK).transpose(0, 2, 5, 1, 4, 3).reshape(B, T, NSA_Q_DIM)

    kap, r, b_h, k_h, v, kend, bend, gend, bonus, gg = _rwkv_pre(
        z_rw, _pad_cols(mu, rw_sizes, rw_pads), w0, a0, k_k, k_a, r_k,
        _pad_rows(w_up, RWKV_LORA_PAD), _pad_rows(a_up, RWKV_LORA_PAD), _pad_rows(g_up, RWKV_GATE_PAD))
    chunk_t = lambda a: a.reshape(B, T // RWKV_CHUNK, RWKV_CHUNK, RWKV_DIM).transpose(0, 1, 3, 2)
    y_b = _rwkv_core(kap, r, v, kend, bend, chunk_t(b_h), chunk_t(k_h), gend[:, ::RWKV_CHUNK])

    w_out = w_out.astype(BF16)
    return _out_proj(y_a, y_b, bonus, gg, gn_w, gn_b, w_out[:NSA_Q_DIM], w_out[NSA_Q_DIM:], x, gate)


def kernel(x, c, ada_w, ada_b, norm_mix, norm_ffn, ffn_w_gate, ffn_w_up, ffn_conv_w, ffn_conv_b, ffn_w_down, ab_w_in, nsa_cmp_pe, nsa_cmp_w1, nsa_cmp_w2, rwkv_mu, rwkv_w0, rwkv_w_up, rwkv_a0, rwkv_a_up, rwkv_g_up, rwkv_k_k, rwkv_k_a, rwkv_r_k, rwkv_gn_w, rwkv_gn_b, ab_w_out, sgu_w_in, sgu_b_in, sgu_vn_w, sgu_vn_b, sgu_w_s, sgu_b_s, sgu_w_out, sgu_b_out, final_norm):
    B, T, D = x.shape
    mod = _ada_mod(c, ada_w, ada_b)
    for i in range(DEPTH):
        sh1, sc1, g1, sh2, sc2, g2 = [m.reshape(B, 1, D) for m in jnp.split(mod[i], 6, axis=-1)]
        j = i // 2
        if i % 2 == 0:
            x = _nsa_rwkv_mixer(x, norm_mix[i], sc1, sh1, g1, ab_w_in[j], nsa_cmp_pe[j], nsa_cmp_w1[j],
                                nsa_cmp_w2[j], rwkv_mu[j], rwkv_w0[j], rwkv_w_up[j], rwkv_a0[j], rwkv_a_up[j],
                                rwkv_g_up[j], rwkv_k_k[j], rwkv_k_a[j], rwkv_r_k[j], rwkv_gn_w[j], rwkv_gn_b[j],
                                ab_w_out[j])
        else:
            x = _sgu(x, norm_mix[i], sc1, sh1, g1, sgu_w_in[j].astype(BF16), sgu_b_in[j], sgu_vn_w[j], sgu_vn_b[j],
                     sgu_w_s[j], sgu_b_s[j].T, sgu_w_out[j].astype(BF16), sgu_b_out[j])
        x = _ffn(x, norm_ffn[i], sc2, sh2, g2, ffn_w_gate[i].astype(BF16), ffn_w_up[i].astype(BF16), ffn_conv_w[i],
                 ffn_conv_b[i], ffn_w_down[i].astype(BF16), final_norm, final=(i == DEPTH - 1))
    return x
```

```python
import functools

import numpy as np
import jax
import jax.numpy as jnp
from jax import lax
from jax.experimental import pallas as pl
from jax.experimental.pallas import tpu as pltpu

F32 = jnp.float32
BF16 = jnp.bfloat16
HI = lax.Precision.HIGHEST

D_MODEL = 1024
DEPTH = 2
HEAD_DIM = 64
NSA_HEADS = 8
NSA_KV_HEADS = 2
NSA_HPG = NSA_HEADS // NSA_KV_HEADS
CMP_STRIDE = 16
CMP_LEN = 2 * CMP_STRIDE
CMP_HIDDEN = 128
SLC_BLOCK = 64
SLC_TOPK = 16
WINDOW = 512
Q_BLOCK = 128
FORCE = 1e4
RWKV_HEADS = 8
RWKV_DIM = RWKV_HEADS * HEAD_DIM
DECAY_LORA = 64
AAA_LORA = 64
GATE_LORA = 160
DECAY_SCALE = 0.6065306597126334
RWKV_GN_EPS = 64e-5
SGU_CHUNK = 128
SGU_GROUPS = 8
SGU_DIM = 2048
FFN_DIM = 2816
CONV_WIDTH = 3
NORM_EPS = 1e-6
LN_EPS = 1e-5
NEG_INF = -1e30
LOG2E = 1.4426950408889634
NSA_Q_DIM = NSA_HEADS * HEAD_DIM
NSA_KV_DIM = NSA_KV_HEADS * HEAD_DIM
NSA_GATE_DIM = NSA_HEADS * 3
NSA_COLS = NSA_Q_DIM + 6 * NSA_KV_DIM + NSA_GATE_DIM

LANES = 128
SUBLANES = 8
VMEM_LIMIT_BYTES = 52 * 1024 * 1024

ROW_TILE = 512
COL_CHUNK = 512
NSA_COLS_PAD = 1408
RWKV_LORA_PAD = 128
RWKV_GATE_PAD = 256
RWKV_COLS_PAD = 3 * RWKV_DIM + 2 * RWKV_LORA_PAD + RWKV_GATE_PAD
SLC_KEY_TILE = 256
NS_PAD = LANES
RWKV_CHUNK = 64
RWKV_SUB = 16
RWKV_PACK = 4
RWKV_STEP_ROWS = 512
FFN_COL_CHUNK = 256
FFN_HALO_ROWS = 16


def _cparams(*sem):
    return pltpu.CompilerParams(dimension_semantics=sem, vmem_limit_bytes=VMEM_LIMIT_BYTES)


def _resident(shape):
    return pl.BlockSpec(shape, lambda *_: (0,) * len(shape), pipeline_mode=pl.Buffered(1))


def _gelu(x):
    return 0.5 * x * (1.0 + jnp.tanh(0.7978845608028654 * (x + 0.044715 * (x * x * x))))


def _rms(x, w):
    return x * lax.rsqrt(jnp.mean(x * x, axis=-1, keepdims=True) + NORM_EPS) * w


def _prev_rows(cur, halo, shift):
    tm = cur.shape[0]
    ext = jnp.concatenate([halo, cur], axis=0)
    return ext[SUBLANES - shift:SUBLANES - shift + tm]


def _split3(x):
    hi = x.astype(BF16)
    r1 = x - hi.astype(F32)
    mid = r1.astype(BF16)
    return hi, mid, (r1 - mid.astype(F32)).astype(BF16)


def _dot_exact_rhs(x, m):
    return jnp.dot(jnp.concatenate(_split3(x), axis=1), jnp.concatenate([m, m, m], axis=0),
                   preferred_element_type=F32)


def _dot_exact_lhs(m, x):
    return jnp.dot(jnp.concatenate([m, m, m], axis=1), jnp.concatenate(_split3(x), axis=0),
                   preferred_element_type=F32)


def _ada_kernel(c_ref, w_ref, b_ref, o_ref):
    c = c_ref[...]
    cond = c * jax.nn.sigmoid(c)
    o_ref[0] = jnp.dot(cond, w_ref[0], precision=HI, preferred_element_type=F32) + b_ref[0]


def _ada_mod(c, ada_w, ada_b):
    B, D = c.shape
    depth, _, N = ada_w.shape
    rows = -(-B // SUBLANES) * SUBLANES
    cp = jnp.zeros((rows, D), F32).at[:B].set(c)
    tn = N // 4
    out = pl.pallas_call(
        _ada_kernel,
        grid=(depth, N // tn),
        in_specs=[pl.BlockSpec((rows, D), lambda i, j: (0, 0)),
                  pl.BlockSpec((1, D, tn), lambda i, j: (i, 0, j)),
                  pl.BlockSpec((1, 1, tn), lambda i, j: (i, 0, j))],
        out_specs=pl.BlockSpec((1, rows, tn), lambda i, j: (i, 0, j)),
        out_shape=jax.ShapeDtypeStruct((depth, rows, N), F32),
        compiler_params=_cparams("parallel", "parallel"),
        name="ada_mod",
    )(cp, ada_w, ada_b.reshape(depth, 1, N))
    return out[:, :B]


def _in_proj_kernel(x_ref, nw_ref, sc_ref, sh_ref, w_ref,
                    qt_ref, zg_ref, kcs_ref, vcs_ref, ks_ref, vst_ref, kw_ref, vwt_ref, zr_ref):
    G, HP, dk, Q = NSA_KV_HEADS, NSA_HPG, HEAD_DIM, Q_BLOCK
    h = (_rms(x_ref[0], nw_ref[...]) * (1.0 + sc_ref[0]) + sh_ref[0]).astype(BF16)
    tm = h.shape[0]
    zq = jnp.dot(h, w_ref[:, 0:NSA_Q_DIM], preferred_element_type=F32) * (dk ** -0.5 * LOG2E)
    for nq in range(tm // Q):
        blk = zq[nq * Q:(nq + 1) * Q]
        for g in range(G):
            heads = [jnp.transpose(blk[:, (g * HP + hh) * dk:(g * HP + hh + 1) * dk]) for hh in range(HP)]
            qt_ref[0, g, nq] = jnp.concatenate(heads, axis=1).astype(BF16)
    kv_w = NSA_COLS_PAD - NSA_Q_DIM
    zkv = jnp.dot(h, w_ref[:, NSA_Q_DIM:NSA_COLS_PAD], preferred_element_type=F32)
    sec = lambda i, g: zkv[:, i * NSA_KV_DIM + g * dk:i * NSA_KV_DIM + (g + 1) * dk]
    for g in range(G):
        kcs_ref[0, g] = sec(0, g)
        vcs_ref[0, g] = sec(1, g)
        ks_ref[0, g] = sec(2, g).astype(BF16)
        vst_ref[0, g] = jnp.transpose(sec(3, g)).astype(BF16)
        kw_ref[0, g] = sec(4, g).astype(BF16)
        vwt_ref[0, g] = jnp.transpose(sec(5, g)).astype(BF16)
    zg_ref[0] = zkv[:, 6 * NSA_KV_DIM:kv_w]
    n = zr_ref.shape[2]
    for c0 in range(0, n, COL_CHUNK):
        zr_ref[0, :, c0:c0 + COL_CHUNK] = jnp.dot(h, w_ref[:, NSA_COLS_PAD + c0:NSA_COLS_PAD + c0 + COL_CHUNK],
                                                  preferred_element_type=F32)


def _in_proj(x, nw, sc, sh, w):
    B, T, D = x.shape
    G, HP, dk, Q = NSA_KV_HEADS, NSA_HPG, HEAD_DIM, Q_BLOCK
    N = w.shape[1]
    tm = min(ROW_TILE, T)
    nq = tm // Q
    row = lambda n: pl.BlockSpec((1, tm, n), lambda b, i: (b, i, 0))
    per_b = pl.BlockSpec((1, 1, D), lambda b, i: (b, 0, 0))
    nat = pl.BlockSpec((1, G, tm, dk), lambda b, i: (b, 0, i, 0))
    tr = pl.BlockSpec((1, G, dk, tm), lambda b, i: (b, 0, 0, i))
    nat_s = lambda dt: jax.ShapeDtypeStruct((B, G, T, dk), dt)
    tr_s = jax.ShapeDtypeStruct((B, G, dk, T), BF16)
    return pl.pallas_call(
        _in_proj_kernel,
        grid=(B, T // tm),
        in_specs=[row(D), _resident((1, D)), per_b, per_b, _resident((D, N))],
        out_specs=[pl.BlockSpec((1, G, nq, dk, HP * Q), lambda b, i: (b, 0, i, 0, 0)),
                   row(NSA_COLS_PAD - NSA_Q_DIM - 6 * NSA_KV_DIM), nat, nat, nat, tr, nat, tr, row(RWKV_COLS_PAD)],
        out_shape=[jax.ShapeDtypeStruct((B, G, T // Q, dk, HP * Q), BF16),
                   jax.ShapeDtypeStruct((B, T, NSA_COLS_PAD - NSA_Q_DIM - 6 * NSA_KV_DIM), F32),
                   nat_s(F32), nat_s(F32), nat_s(BF16), tr_s, nat_s(BF16), tr_s,
                   jax.ShapeDtypeStruct((B, T, RWKV_COLS_PAD), F32)],
        compiler_params=_cparams("parallel", "parallel"),
        name="in_proj",
    )(x, nw.reshape(1, D), sc, sh, w)


def _nsa_compress_kernel(xk_ref, xv_ref, pe_ref, w1_ref, w2_ref, kc_ref, vct_ref):
    half = CMP_STRIDE * HEAD_DIM

    def mlp(x, s):
        pe = pe_ref[s]
        xa = (x + pe[:, :half]).astype(BF16)
        xb = (x + pe[:, half:]).astype(BF16)
        w1 = w1_ref[s]
        p = jnp.dot(xa, w1[:half], preferred_element_type=F32)
        q = jnp.dot(xb, w1[half:], preferred_element_type=F32)
        hid = _gelu(p + pltpu.roll(q, x.shape[0] - 1, 0))
        return jnp.dot(hid.astype(BF16), w2_ref[s], preferred_element_type=F32)

    kc_ref[0] = mlp(xk_ref[0], 0).astype(BF16)
    vct_ref[0] = jnp.transpose(mlp(xv_ref[0], 1).astype(BF16))


def _nsa_compress(xk, xv, pe, w1, w2):
    BG, NC, W = xk.shape
    src = pl.BlockSpec((1, NC, W), lambda b: (b, 0, 0))
    return pl.pallas_call(
        _nsa_compress_kernel,
        grid=(BG,),
        in_specs=[src, src, _resident((2, 1, 2 * W)), _resident((2, 2 * W, CMP_HIDDEN)),
                  _resident((2, CMP_HIDDEN, HEAD_DIM))],
        out_specs=[pl.BlockSpec((1, NC, HEAD_DIM), lambda b: (b, 0, 0)),
                   pl.BlockSpec((1, HEAD_DIM, NC), lambda b: (b, 0, 0))],
        out_shape=[jax.ShapeDtypeStruct((BG, NC, HEAD_DIM), BF16), jax.ShapeDtypeStruct((BG, HEAD_DIM, NC), BF16)],
        compiler_params=_cparams("parallel"),
        name="nsa_compress",
    )(xk, xv, pe, w1, w2)


def _nsa_attn_kernel(qt_ref, glt_ref, kc_ref, vct_ref, ks_ref, vst_ref, kw_ref, vwt_ref, aggt_ref,
                     o_ref, selt_ref, sa_ref, sb_ref, pa_ref, pb_ref, *, n_slc, top_k):
    g = pl.program_id(1)
    n = pl.program_id(2)
    Q, HP, R = Q_BLOCK, NSA_HPG, NSA_HPG * Q_BLOCK
    t0 = pl.multiple_of(n * Q, Q)
    qt = qt_ref[0, 0, 0]

    def per_head(a):
        return jnp.concatenate([a] * HP, axis=1)

    slope_g = jnp.float32(1.0)
    for gi in range(1, NSA_KV_HEADS):
        slope_g = jnp.where(g == gi, jnp.float32(2.0 ** (-HP * gi)), slope_g)
    lane_h = lax.broadcasted_iota(jnp.int32, (1, R), 1) // Q
    slope = jnp.full((1, R), LOG2E * 2.0 ** -HP, F32)
    for h in range(HP - 1):
        slope = jnp.where(lane_h == h, jnp.float32(LOG2E * 2.0 ** -(h + 1)), slope)
    slope = slope * slope_g
    tq = t0 + lax.broadcasted_iota(jnp.int32, (1, Q), 1)

    ncp = kc_ref.shape[2]
    s = jnp.dot(kc_ref[0, 0], qt, preferred_element_type=F32)
    cmp_end = lax.broadcasted_iota(jnp.int32, (ncp, 1), 0) * CMP_STRIDE + (CMP_LEN - 1)
    dist = tq - cmp_end
    bias = per_head(jnp.where(dist >= 0, -dist.astype(F32), NEG_INF))
    valid = bias > 0.5 * NEG_INF
    s = s + slope * bias
    e = jnp.where(valid, jnp.exp2(s - jnp.max(s, axis=0, keepdims=True)), 0.0)
    l = jnp.sum(e, axis=0, keepdims=True)
    p = e * (1.0 / jnp.where(l > 0.0, l, 1.0))
    o_c = jnp.dot(vct_ref[0, 0], p.astype(BF16), preferred_element_type=F32)

    WK = WINDOW + Q
    w0 = pl.multiple_of(jnp.maximum(t0 - WINDOW, 0), Q)
    sw = jnp.dot(kw_ref[0, 0, pl.ds(w0, WK), :], qt, preferred_element_type=F32)
    dw = tq - (w0 + lax.broadcasted_iota(jnp.int32, (WK, 1), 0))
    okw = (dw >= 0) & (dw < WINDOW)
    sw = sw + slope * per_head(jnp.where(okw, -dw.astype(F32), NEG_INF))
    ew = jnp.exp2(sw - jnp.max(sw, axis=0, keepdims=True))
    lw = jnp.sum(ew, axis=0, keepdims=True)
    o_w = jnp.dot(vwt_ref[0, 0, :, pl.ds(w0, WK)], ew.astype(BF16), preferred_element_type=F32) * (1.0 / lw)

    psum = p[:, 0:Q]
    for h in range(1, HP):
        psum = psum + p[:, h * Q:(h + 1) * Q]
    p_hi = psum.astype(BF16)
    p_lo = (psum - p_hi.astype(F32)).astype(BF16)
    aggt = aggt_ref[...]
    imp = jnp.dot(jnp.concatenate([aggt, aggt], axis=1), jnp.concatenate([p_hi, p_lo], axis=0),
                  preferred_element_type=F32)
    blk = lax.broadcasted_iota(jnp.int32, (NS_PAD, 1), 0)
    jt = tq // SLC_BLOCK
    forced = (blk == 0) | (blk == jt) | (blk == jt - 1)
    imp = jnp.where(blk > jt, -FORCE, jnp.where(forced, FORCE, imp))
    vals = jnp.where(blk < n_slc, imp, -jnp.inf)
    blk_f = blk.astype(F32)
    sel = jnp.zeros((NS_PAD, Q), F32)
    for _ in range(top_k):
        vmax = jnp.max(vals, axis=0, keepdims=True)
        first = jnp.min(jnp.where(vals == vmax, blk_f, float(NS_PAD)), axis=0, keepdims=True)
        pick = blk_f == first
        sel = jnp.where(pick, 1.0, sel)
        vals = jnp.where(pick, -jnp.inf, vals)
    selt_ref[...] = sel

    TK = SLC_KEY_TILE
    nb = TK // SLC_BLOCK
    n_tiles = (t0 + Q + TK - 1) // TK
    last_tile = ks_ref.shape[2] // TK - 1
    key_in_blk = lax.broadcasted_iota(jnp.int32, (SLC_BLOCK, 1), 0)

    def qk(kt, s_ref):
        k0 = pl.multiple_of(jnp.minimum(kt, last_tile) * TK, TK)
        s_ref[...] = jnp.dot(ks_ref[0, 0, pl.ds(k0, TK), :], qt, preferred_element_type=F32)

    def pv(kt, p_ref):
        k0 = pl.multiple_of(jnp.clip(kt, 0, last_tile) * TK, TK)
        return jnp.dot(vst_ref[0, 0, :, pl.ds(k0, TK)], p_ref[...], preferred_element_type=F32)

    def softmax_tile(kt, s_ref, p_ref, m_i, l_i):
        dms = []
        for j in range(nb):
            rel = (kt * TK + j * SLC_BLOCK + key_in_blk) - tq
            chosen = selt_ref[pl.ds(jnp.minimum(kt * nb + j, NS_PAD - 1), 1), :] > 0.5
            dms.append(jnp.where(chosen & (rel <= 0), rel.astype(F32), NEG_INF))
        sc = s_ref[...] + slope * per_head(jnp.concatenate(dms, axis=0))
        m_new = jnp.maximum(m_i, jnp.max(sc, axis=0, keepdims=True))
        alpha = jnp.exp2(m_i - m_new)
        pe_ = jnp.exp2(sc - m_new)
        p_ref[...] = pe_.astype(BF16)
        return m_new, alpha, alpha * l_i + jnp.sum(pe_, axis=0, keepdims=True)

    def slc_pair(j, carry):
        m_i, l_i, acc = carry
        ka = 2 * j
        qk(ka + 1, sb_ref)
        acc = acc + pv(ka - 1, pb_ref)
        m_i, alpha, l_i = softmax_tile(ka, sa_ref, pa_ref, m_i, l_i)
        acc = alpha * acc
        qk(ka + 2, sa_ref)
        acc = acc + pv(ka, pa_ref)
        m_i, alpha, l_i = softmax_tile(ka + 1, sb_ref, pb_ref, m_i, l_i)
        return m_i, l_i, alpha * acc

    pb_ref[...] = jnp.zeros_like(pb_ref)
    qk(0, sa_ref)
    n_pairs = (n_tiles + 1) // 2
    init = (jnp.full((1, R), NEG_INF, F32), jnp.zeros((1, R), F32), jnp.zeros((HEAD_DIM, R), F32))
    _, l_s, acc_s = lax.fori_loop(0, n_pairs, slc_pair, init)
    acc_s = acc_s + pv(2 * n_pairs - 1, pb_ref)

    gates = jax.nn.sigmoid(glt_ref[0, 0, 0])
    o_ref[0, 0, 0] = gates[0:1] * o_c + gates[1:2] * (acc_s * (1.0 / l_s)) + gates[2:3] * o_w


def _nsa_attention(qt, glt, kc, vct, ks, vst, kw, vwt, aggt, *, n_slc, top_k):
    B, G, NQ, dk, R = qt.shape
    ncp = kc.shape[2]
    T = ks.shape[2]
    assert T >= WINDOW + Q_BLOCK
    per_bg = lambda b, g, n: (b, g, 0, 0)
    per_q = lambda rows: pl.BlockSpec((1, 1, 1, rows, R), lambda b, g, n: (b, g, n, 0, 0))
    return pl.pallas_call(
        functools.partial(_nsa_attn_kernel, n_slc=n_slc, top_k=top_k),
        grid=(B, G, NQ),
        in_specs=[per_q(dk), per_q(3),
                  pl.BlockSpec((1, 1, ncp, dk), per_bg),
                  pl.BlockSpec((1, 1, dk, ncp), per_bg),
                  pl.BlockSpec((1, 1, T, dk), per_bg),
                  pl.BlockSpec((1, 1, dk, T), per_bg),
                  pl.BlockSpec((1, 1, T, dk), per_bg),
                  pl.BlockSpec((1, 1, dk, T), per_bg),
                  _resident((NS_PAD, ncp))],
        out_specs=per_q(dk),
        out_shape=jax.ShapeDtypeStruct((B, G, NQ, dk, R), F32),
        scratch_shapes=[pltpu.VMEM((NS_PAD, Q_BLOCK), F32),
                        pltpu.VMEM((SLC_KEY_TILE, R), F32), pltpu.VMEM((SLC_KEY_TILE, R), F32),
                        pltpu.VMEM((SLC_KEY_TILE, R), BF16), pltpu.VMEM((SLC_KEY_TILE, R), BF16)],
        compiler_params=_cparams("parallel", "parallel", "arbitrary"),
        name="nsa_attention",
    )(qt, glt, kc, vct, ks, vst, kw, vwt, aggt)


def _rwkv_pre_kernel(z_ref, halo_ref, mu_ref, w0_ref, a0_ref, kk_ref, ka_ref, rk_ref, wup_ref, aup_ref, gup_ref,
                     ones_ref, csum_ref,
                     kap_ref, r_ref, v_ref, kend_ref, bend_ref, bt_ref, kt_ref, gend_ref, bonus_ref, g_ref):
    i = pl.program_id(1)
    z = z_ref[0]
    tm = z.shape[0]
    halo = jnp.where(i > 0, halo_ref[0], 0.0)
    zs = z + (_prev_rows(z, halo, 1) - z) * mu_ref[...]
    D = RWKV_DIM
    r, k, v = zs[:, 0:D], zs[:, D:2 * D], zs[:, 2 * D:3 * D]
    o1 = 3 * D
    o2 = o1 + RWKV_LORA_PAD
    o3 = o2 + RWKV_LORA_PAD
    wl, al, gl = zs[:, o1:o2], zs[:, o2:o3], zs[:, o3:]
    dot = functools.partial(jnp.dot, precision=HI, preferred_element_type=F32)
    lw = -DECAY_SCALE * jax.nn.sigmoid(w0_ref[...] + dot(jnp.tanh(wl), wup_ref[...]))
    a = jax.nn.sigmoid(a0_ref[...] + dot(al, aup_ref[...]))
    kt = k * (1.0 + (a - 1.0) * ka_ref[...])
    kk = k * kk_ref[...]
    head_sums = _dot_exact_rhs(jnp.concatenate([kk * kk, r * kt * rk_ref[...]], axis=0), ones_ref[...])
    kk = kk / jnp.maximum(jnp.sqrt(head_sums[:tm]), 1e-12)
    b = kk * a
    sums = _dot_exact_lhs(csum_ref[...], lw)
    cum, tot = sums[:tm], sums[tm:]
    g_inv = jnp.exp(-cum)
    tail = jnp.exp(tot - cum)
    kap_ref[0] = kk * jnp.exp(cum - lw)
    r_ref[0] = r * jnp.exp(cum)
    v_ref[0] = v
    kend_ref[0] = kt * tail
    bend_ref[0] = b * tail
    b_h, k_h, g_end = b * g_inv, kt * g_inv, jnp.exp(tot)
    C = RWKV_CHUNK
    for c in range(tm // C):
        bt_ref[0, c] = jnp.transpose(b_h[c * C:(c + 1) * C])
        kt_ref[0, c] = jnp.transpose(k_h[c * C:(c + 1) * C])
    gend_ref[0] = jnp.concatenate([g_end[c * C:c * C + 1] for c in range(tm // C)], axis=0)
    bonus_ref[0] = head_sums[tm:] * v
    g_ref[0] = dot(jax.nn.sigmoid(gl), gup_ref[...])


def _rwkv_pre(z, mu, w0, a0, k_k, k_a, r_k, w_up, a_up, g_up):
    B, T, W = z.shape
    D = RWKV_DIM
    tm = min(ROW_TILE, T)
    hb = tm // SUBLANES
    C = RWKV_CHUNK
    ones_bd = jnp.asarray(np.kron(np.eye(RWKV_HEADS), np.ones((HEAD_DIM, HEAD_DIM))), BF16)
    chunks = np.eye(tm // C)
    csum = jnp.asarray(np.concatenate([np.kron(chunks, np.tril(np.ones((C, C)))),
                                       np.kron(chunks, np.ones((C, C)))], axis=0), BF16)
    out = jax.ShapeDtypeStruct((B, T, D), F32)
    row = pl.BlockSpec((1, tm, D), lambda b, i: (b, i, 0))
    tr = pl.BlockSpec((1, tm // C, D, C), lambda b, i: (b, i, 0, 0))
    tr_s = jax.ShapeDtypeStruct((B, T // C, D, C), F32)
    return pl.pallas_call(
        _rwkv_pre_kernel,
        grid=(B, T // tm),
        in_specs=[pl.BlockSpec((1, tm, W), lambda b, i: (b, i, 0)),
                  pl.BlockSpec((1, SUBLANES, W), lambda b, i: (b, jnp.maximum(i * hb - 1, 0), 0)),
                  _resident((1, W)), _resident((1, D)), _resident((1, D)), _resident((1, D)), _resident((1, D)),
                  _resident((1, D)), _resident((RWKV_LORA_PAD, D)), _resident((RWKV_LORA_PAD, D)),
                  _resident((RWKV_GATE_PAD, D)), _resident((D, D)), _resident((2 * tm, tm))],
        out_specs=[row] * 5 + [tr, tr, pl.BlockSpec((1, tm // C, D), lambda b, i: (b, i, 0)), row, row],
        out_shape=[out] * 5 + [tr_s, tr_s, jax.ShapeDtypeStruct((B, T // C, D), F32), out, out],
        compiler_params=_cparams("parallel", "parallel"),
        name="rwkv_pre",
    )(z, z, mu.reshape(1, W), w0.reshape(1, D), a0.reshape(1, D), k_k.reshape(1, D), k_a.reshape(1, D),
      r_k.reshape(1, D), w_up, a_up, g_up, ones_bd, csum)


def _rwkv_core_kernel(kap_ref, r_ref, v_ref, kend_ref, bend_ref, bt_ref, kt_ref, gend_ref, o_ref, h_ref):
    C, N, P = RWKV_CHUNK, HEAD_DIM, RWKV_PACK
    W = P * N
    groups = kap_ref.shape[2] // W

    @pl.when(pl.program_id(1) == 0)
    def _():
        h_ref[...] = jnp.zeros_like(h_ref)

    ri = lax.broadcasted_iota(jnp.int32, (C, W), 0)
    cj = lax.broadcasted_iota(jnp.int32, (C, W), 1) % N
    tril = (ri >= cj).astype(F32)
    stril = (ri > cj).astype(F32)
    eye = (ri == cj).astype(F32)
    diag_blk = ((ri // RWKV_SUB) == (cj // RWKV_SUB)).astype(F32)
    same_head = (lax.broadcasted_iota(jnp.int32, (W, W), 0) // N) == (lax.broadcasted_iota(jnp.int32, (W, W), 1) // N)

    def split(x):
        hi = x.astype(BF16)
        return hi, (x - hi.astype(F32)).astype(BF16)

    def lhs3(x):
        hi, lo = split(x)
        return jnp.concatenate([hi, lo, hi], axis=1)

    def rhs3(hi, lo):
        return jnp.concatenate([hi, hi, lo], axis=0)

    def bd(x):
        hi, lo = split(x)
        blk = lambda a: jnp.where(same_head, jnp.concatenate([a] * P, axis=0), jnp.zeros((), BF16))
        return rhs3(blk(hi), blk(lo))

    def bd_t(xt):
        full = jnp.where(same_head, jnp.concatenate([xt] * P, axis=1), 0.0)
        return rhs3(*split(full))

    def mmw(x, w3):
        return jnp.dot(lhs3(x), w3, preferred_element_type=F32)

    def mmp(x, y):
        return mmw(x, bd(y))

    def gm(f, *cols):
        return [f(*args) for args in zip(*cols)]

    def chunk(c, hs):
        rows = pl.ds(pl.multiple_of(c * C, C), C)
        lanes = [slice(p * W, (p + 1) * W) for p in range(groups)]
        v = [v_ref[0, rows, ln] for ln in lanes]
        lhs = [lhs3(jnp.concatenate([kap_ref[0, rows, ln], r_ref[0, rows, ln]], axis=0)) for ln in lanes]
        ab = gm(lambda l, ln: jnp.dot(l, bd_t(bt_ref[0, c, ln, :]), preferred_element_type=F32), lhs, lanes)
        ak = gm(lambda l, ln: jnp.dot(l, bd_t(kt_ref[0, c, ln, :]), preferred_element_type=F32), lhs, lanes)
        a_kb = [stril * x[:C] for x in ab]
        a_rb = [tril * x[C:] for x in ab]
        a_kr = [jnp.concatenate([stril * x[:C], tril * x[C:]], axis=0) for x in ak]
        d = [x * diag_blk for x in a_kb]
        e = gm(lambda x, y: x - y, a_kb, d)
        d_inv = [eye - x for x in d]
        pw = gm(mmp, d, d)
        steps = int(np.log2(RWKV_SUB)) - 1
        for s_ in range(steps):
            d_inv = gm(lambda x, y: mmp(x, eye + y), d_inv, pw)
            if s_ + 1 < steps:
                pw = gm(mmp, pw, pw)
        nb = gm(mmp, d_inv, e)
        t_inv = [eye - x for x in nb]
        pw = gm(mmp, nb, nb)
        bsteps = int(np.log2(C // RWKV_SUB)) - 1
        for s_ in range(bsteps):
            t_inv = gm(lambda x, y: mmp(x, eye + y), t_inv, pw)
            if s_ + 1 < bsteps:
                pw = gm(mmp, pw, pw)
        t_inv = gm(mmp, t_inv, d_inv)
        sh = gm(lambda l, h: jnp.dot(l, bd(h), preferred_element_type=F32), lhs, hs)
        av = gm(mmp, a_kr, v)
        u = gm(lambda t, x, y: mmp(t, x[:C] + y[:C]), t_inv, sh, av)
        y = gm(lambda x, z, a, uu: x[C:] + z[C:] - mmp(a, uu), sh, av, a_rb, u)
        for ln, yy in zip(lanes, y):
            o_ref[0, rows, ln] = yy
        zero = jnp.zeros((C, W), F32)
        left = [jnp.concatenate([eye * gend_ref[0, pl.ds(c, 1), ln], kend_ref[0, rows, ln], bend_ref[0, rows, ln],
                                 zero], axis=0) for ln in lanes]
        right = gm(lambda h, vv, uu: rhs3(*split(jnp.concatenate([h, vv, -uu, zero], axis=0))), hs, v, u)
        full = gm(lambda l, rr: jnp.where(same_head, mmw(jnp.transpose(l), rr), 0.0), left, right)
        return tuple(sum(f[s_ * N:(s_ + 1) * N] for s_ in range(1, P)) + f[0:N] for f in full)

    hs = lax.fori_loop(0, kap_ref.shape[1] // C, chunk, tuple(h_ref[p] for p in range(groups)))
    for p in range(groups):
        h_ref[p] = hs[p]


def _rwkv_core(kap, r, v, kend, bend, bt, kt, gend):
    B, T, D = kap.shape
    W = RWKV_PACK * HEAD_DIM
    assert D % W == 0 and W == 4 * RWKV_CHUNK
    tc = min(RWKV_STEP_ROWS, T)
    nch = tc // RWKV_CHUNK
    seq = pl.BlockSpec((1, tc, D), lambda b, i: (b, i, 0))
    tr = pl.BlockSpec((1, nch, D, RWKV_CHUNK), lambda b, i: (b, i, 0, 0))
    return pl.pallas_call(
        _rwkv_core_kernel,
        grid=(B, T // tc),
        in_specs=[seq] * 5 + [tr] * 2 + [pl.BlockSpec((1, nch, D), lambda b, i: (b, i, 0))],
        out_specs=seq,
        out_shape=jax.ShapeDtypeStruct((B, T, D), F32),
        scratch_shapes=[pltpu.VMEM((D // W, HEAD_DIM, W), F32)],
        compiler_params=_cparams("parallel", "arbitrary"),
        name="rwkv_core",
    )(kap, r, v, kend, bend, bt, kt, gend)


def _out_proj_kernel(ot_ref, y_ref, bonus_ref, gg_ref, gw_ref, gb_ref, ones_ref, wa_ref, wb_ref, x_ref, g_ref,
                     o_ref):
    G, HP, Q = NSA_KV_HEADS, NSA_HPG, Q_BLOCK
    blocks = []
    for nq in range(ot_ref.shape[2]):
        heads = [jnp.transpose(ot_ref[0, g, nq][:, hh * Q:(hh + 1) * Q]) for g in range(G) for hh in range(HP)]
        blocks.append(jnp.concatenate(heads, axis=1))
    ya = jnp.concatenate(blocks, axis=0).astype(BF16)
    y = y_ref[0]
    inv_n = 1.0 / HEAD_DIM
    dlt = y - _dot_exact_rhs(y, ones_ref[...]) * inv_n
    var = _dot_exact_rhs(dlt * dlt, ones_ref[...]) * inv_n
    yb = (dlt * lax.rsqrt(var + RWKV_GN_EPS) * gw_ref[...] + gb_ref[...] + bonus_ref[0]) * gg_ref[0]
    o = jnp.dot(ya, wa_ref[...], preferred_element_type=F32)
    o = o + jnp.dot(yb.astype(BF16), wb_ref[...], preferred_element_type=F32)
    o_ref[0] = x_ref[0] + g_ref[0] * o


def _out_proj(ot, y, bonus, gg, gn_w, gn_b, wa, wb, x, gate):
    B, T, D = x.shape
    G, NQ, dk, R = ot.shape[1:]
    tm = min(ROW_TILE, T)
    ka, kb = wa.shape[0], y.shape[-1]
    ones_bd = jnp.asarray(np.kron(np.eye(kb // HEAD_DIM), np.ones((HEAD_DIM, HEAD_DIM))), BF16)
    row = lambda w: pl.BlockSpec((1, tm, w), lambda b, i: (b, i, 0))
    return pl.pallas_call(
        _out_proj_kernel,
        grid=(B, T // tm),
        in_specs=[pl.BlockSpec((1, G, tm // Q_BLOCK, dk, R), lambda b, i: (b, 0, i, 0, 0)),
                  row(kb), row(kb), row(kb), _resident((1, kb)), _resident((1, kb)), _resident((kb, kb)),
                  _resident((ka, D)), _resident((kb, D)), row(D),
                  pl.BlockSpec((1, 1, D), lambda b, i: (b, 0, 0))],
        out_specs=row(D),
        out_shape=jax.ShapeDtypeStruct((B, T, D), F32),
        compiler_params=_cparams("parallel", "parallel"),
        name="out_proj",
    )(ot, y, bonus, gg, gn_w.reshape(1, kb), gn_b.reshape(1, kb), ones_bd, wa, wb, x, gate)


def _sgu_kernel(x_ref, nw_ref, sc_ref, sh_ref, g_ref, wi_ref, bi_ref, vnw_ref, vnb_ref, ws_ref, bst_ref,
                wo_ref, bo_ref, o_ref, u_ref, v_ref, gated_ref):
    x = x_ref[0]
    tm = x.shape[0]
    E = SGU_DIM
    h = (_rms(x, nw_ref[...]) * (1.0 + sc_ref[0]) + sh_ref[0]).astype(BF16)
    for c0 in range(0, 2 * E, COL_CHUNK):
        zc = _gelu(jnp.dot(h, wi_ref[:, c0:c0 + COL_CHUNK], preferred_element_type=F32) + bi_ref[:, c0:c0 + COL_CHUNK])
        if c0 < E:
            u_ref[:, c0:c0 + COL_CHUNK] = zc
        else:
            v_ref[:, c0 - E:c0 - E + COL_CHUNK] = zc
    v = v_ref[...]
    mu = jnp.mean(v, axis=-1, keepdims=True)
    var = jnp.mean(jnp.square(v - mu), axis=-1, keepdims=True)
    vn = ((v - mu) * lax.rsqrt(var + LN_EPS) * vnw_ref[...] + vnb_ref[...]).astype(BF16)
    S = SGU_CHUNK
    gw = E // SGU_GROUPS
    causal = lax.broadcasted_iota(jnp.int32, (S, S), 0) >= lax.broadcasted_iota(jnp.int32, (S, S), 1)
    for gi in range(SGU_GROUPS):
        ws = jnp.where(causal, ws_ref[gi], 0.0).astype(BF16)
        bs = bst_ref[:, gi:gi + 1]
        for n in range(tm // S):
            sv = jnp.dot(ws, vn[n * S:(n + 1) * S, gi * gw:(gi + 1) * gw], preferred_element_type=F32) + bs
            u = u_ref[n * S:(n + 1) * S, gi * gw:(gi + 1) * gw]
            gated_ref[n * S:(n + 1) * S, gi * gw:(gi + 1) * gw] = (u * sv).astype(BF16)
    y = jnp.dot(gated_ref[...], wo_ref[...], preferred_element_type=F32) + bo_ref[...]
    o_ref[0] = x + g_ref[0] * y


def _sgu(x, nw, sc, sh, gate, w_in, b_in, vn_w, vn_b, w_s, b_s_t, w_out, b_out):
    B, T, D = x.shape
    E = SGU_DIM
    tm = min(ROW_TILE, T)
    assert tm % SGU_CHUNK == 0
    row = pl.BlockSpec((1, tm, D), lambda b, i: (b, i, 0))
    per_b = pl.BlockSpec((1, 1, D), lambda b, i: (b, 0, 0))
    return pl.pallas_call(
        _sgu_kernel,
        grid=(B, T // tm),
        in_specs=[row, _resident((1, D)), per_b, per_b, per_b,
                  _resident((D, 2 * E)), _resident((1, 2 * E)), _resident((1, E)), _resident((1, E)),
                  _resident((SGU_GROUPS, SGU_CHUNK, SGU_CHUNK)), _resident((SGU_CHUNK, SGU_GROUPS)),
                  _resident((E, D)), _resident((1, D))],
        out_specs=row,
        out_shape=jax.ShapeDtypeStruct((B, T, D), F32),
        scratch_shapes=[pltpu.VMEM((tm, E), F32), pltpu.VMEM((tm, E), F32), pltpu.VMEM((tm, E), BF16)],
        compiler_params=_cparams("parallel", "parallel"),
        name="sgu",
    )(x, nw.reshape(1, D), sc, sh, gate, w_in, b_in.reshape(1, 2 * E), vn_w.reshape(1, E), vn_b.reshape(1, E),
      w_s, b_s_t, w_out, b_out.reshape(1, D))


def _ffn_kernel(x_ref, halo_ref, nw_ref, sc_ref, sh_ref, g_ref, wg_ref, wu_ref, cw_ref, cb_ref, wd_ref, fn_ref,
                o_ref, act_ref, *, final):
    i = pl.program_id(1)
    x = x_ref[0]
    tm = x.shape[0]
    F = wg_ref.shape[1]
    HR = FFN_HALO_ROWS
    hx = (_rms(jnp.concatenate([halo_ref[0], x], axis=0), nw_ref[...]) * (1.0 + sc_ref[0]) + sh_ref[0]).astype(BF16)
    h = hx[HR:]
    for c0 in range(0, F, FFN_COL_CHUNK):
        cols = slice(c0, c0 + FFN_COL_CHUNK)
        ext = jnp.dot(hx, wg_ref[:, cols], preferred_element_type=F32)
        ext = jnp.concatenate([jnp.where(i > 0, ext[:HR], 0.0), ext[HR:]], axis=0)
        cw = cw_ref[:, cols]
        a = (cw[0:1] * ext[HR - 2:HR - 2 + tm] + cw[1:2] * ext[HR - 1:HR - 1 + tm] + cw[2:3] * ext[HR:]
             + cb_ref[:, cols])
        up = jnp.dot(h, wu_ref[:, cols], preferred_element_type=F32)
        act_ref[:, cols] = (_gelu(a) * up).astype(BF16)
    acc = jnp.dot(act_ref[...], wd_ref[...], preferred_element_type=F32)
    xn = x + g_ref[0] * acc
    if final:
        xn = _rms(xn, fn_ref[...])
    o_ref[0] = xn


def _ffn(x, nw, sc, sh, gate, w_gate, w_up, conv_w, conv_b, w_down, final_w, *, final):
    B, T, D = x.shape
    F = w_down.shape[0]
    tm = min(ROW_TILE, T)
    hb = tm // FFN_HALO_ROWS
    row = pl.BlockSpec((1, tm, D), lambda b, i: (b, i, 0))
    per_b = pl.BlockSpec((1, 1, D), lambda b, i: (b, 0, 0))
    return pl.pallas_call(
        functools.partial(_ffn_kernel, final=final),
        grid=(B, T // tm),
        in_specs=[row,
                  pl.BlockSpec((1, FFN_HALO_ROWS, D), lambda b, i: (b, jnp.maximum(i * hb - 1, 0), 0)),
                  _resident((1, D)), per_b, per_b, per_b,
                  _resident((D, F)), _resident((D, F)), _resident((CONV_WIDTH, F)), _resident((1, F)),
                  _resident((F, D)), _resident((1, D))],
        out_specs=row,
        out_shape=jax.ShapeDtypeStruct((B, T, D), F32),
        scratch_shapes=[pltpu.VMEM((tm, F), BF16)],
        compiler_params=_cparams("parallel", "parallel"),
        name="ffn",
    )(x, x, nw.reshape(1, D), sc, sh, gate, w_gate, w_up, conv_w, conv_b.reshape(1, F), w_down,
      final_w.reshape(1, D))


def _pad_cols(w, sizes, padded):
    parts, o = [], 0
    for s, p in zip(sizes, padded):
        parts.append(jnp.pad(w[..., o:o + s], [(0, 0)] * (w.ndim - 1) + [(0, p - s)]))
        o += s
    return jnp.concatenate(parts, axis=-1)


def _pad_rows(w, rows):
    return jnp.pad(w, ((0, rows - w.shape[0]), (0, 0)))


def _nsa_agg_t(T):
    n_cmp_pad = T // CMP_STRIDE
    n_slc = T // SLC_BLOCK
    c = np.arange(n_cmp_pad)
    s = np.arange(NS_PAD)
    cs, ce, ss = c * CMP_STRIDE, c * CMP_STRIDE + CMP_LEN - 1, s * SLC_BLOCK
    agg = (cs[None, :] < ss[:, None] + SLC_BLOCK) & (ce[None, :] >= ss[:, None]) & (s[:, None] < n_slc)
    agg &= (c[None, :] < n_cmp_pad - 1)
    return jnp.asarray(agg, BF16)


def _nsa_rwkv_mixer(x, nw, sc, sh, gate, w_in, cmp_pe, cmp_w1, cmp_w2, mu, w0, w_up, a0, a_up, g_up,
                    k_k, k_a, r_k, gn_w, gn_b, w_out):
    B, T, D = x.shape
    G, HPG, dk = NSA_KV_HEADS, NSA_HPG, HEAD_DIM
    n_slc = T // SLC_BLOCK
    NQ = T // Q_BLOCK
    assert n_slc <= NS_PAD and T % (CMP_STRIDE * LANES) == 0 and T % (2 * SLC_KEY_TILE) == 0

    rw_sizes = [RWKV_DIM] * 3 + [DECAY_LORA, AAA_LORA, GATE_LORA]
    rw_pads = [RWKV_DIM] * 3 + [RWKV_LORA_PAD, RWKV_LORA_PAD, RWKV_GATE_PAD]
    w_all = jnp.concatenate([jnp.pad(w_in[:, :NSA_COLS], ((0, 0), (0, NSA_COLS_PAD - NSA_COLS))),
                             _pad_cols(w_in[:, NSA_COLS:], rw_sizes, rw_pads)], axis=1).astype(BF16)
    qt, zg, kcs, vcs, ks, vst, kw, vwt, z_rw = _in_proj(x, nw, sc, sh, w_all)

    glt = zg[..., :NSA_GATE_DIM].reshape(B, NQ, Q_BLOCK, G, HPG, 3).transpose(0, 3, 1, 5, 4, 2)
    glt = glt.reshape(B, G, NQ, 3, HPG * Q_BLOCK)
    strides = lambda a: a.reshape(B * G, T // CMP_STRIDE, CMP_STRIDE * dk)
    kc, vct = _nsa_compress(strides(kcs), strides(vcs), cmp_pe.reshape(2, 1, CMP_LEN * dk),
                            cmp_w1.astype(BF16), cmp_w2.astype(BF16))
    o_t = _nsa_attention(qt, glt, kc.reshape(B, G, T // CMP_STRIDE, dk), vct.reshape(B, G, dk, T // CMP_STRIDE),
                         ks, vst, kw, vwt, _nsa_agg_t(T), n_slc=n_slc, top_k=min(SLC_TOPK, n_slc))

    kap, r, v, kend, bend, bt, kt, gend, bonus, gg = _rwkv_pre(
        z_rw, _pad_cols(mu, rw_sizes, rw_pads), w0, a0, k_k, k_a, r_k,
        _pad_rows(w_up, RWKV_LORA_PAD), _pad_rows(a_up, RWKV_LORA_PAD), _pad_rows(g_up, RWKV_GATE_PAD))
    y_b = _rwkv_core(kap, r, v, kend, bend, bt, kt, gend)

    w_out = w_out.astype(BF16)
    return _out_proj(o_t, y_b, bonus, gg, gn_w, gn_b, w_out[:NSA_Q_DIM], w_out[NSA_Q_DIM:], x, gate)


def kernel(x, c, ada_w, ada_b, norm_mix, norm_ffn, ffn_w_gate, ffn_w_up, ffn_conv_w, ffn_conv_b, ffn_w_down, ab_w_in, nsa_cmp_pe, nsa_cmp_w1, nsa_cmp_w2, rwkv_mu, rwkv_w0, rwkv_w_up, rwkv_a0, rwkv_a_up, rwkv_g_up, rwkv_k_k, rwkv_k_a, rwkv_r_k, rwkv_gn_w, rwkv_gn_b, ab_w_out, sgu_w_in, sgu_b_in, sgu_vn_w, sgu_vn_b, sgu_w_s, sgu_b_s, sgu_w_out, sgu_b_out, final_norm):
    B, T, D = x.shape
    mod = _ada_mod(c, ada_w, ada_b)
    for i in range(DEPTH):
        sh1, sc1, g1, sh2, sc2, g2 = [m.reshape(B, 1, D) for m in jnp.split(mod[i], 6, axis=-1)]
        j = i // 2
        if i % 2 == 0:
            x = _nsa_rwkv_mixer(x, norm_mix[i], sc1, sh1, g1, ab_w_in[j], nsa_cmp_pe[j], nsa_cmp_w1[j],
                                nsa_cmp_w2[j], rwkv_mu[j], rwkv_w0[j], rwkv_w_up[j], rwkv_a0[j], rwkv_a_up[j],
                                rwkv_g_up[j], rwkv_k_k[j], rwkv_k_a[j], rwkv_r_k[j], rwkv_gn_w[j], rwkv_gn_b[j],
                                ab_w_out[j])
        else:
            x = _sgu(x, norm_mix[i], sc1, sh1, g1, sgu_w_in[j].astype(BF16), sgu_b_in[j], sgu_vn_w[j], sgu_vn_b[j],
                     sgu_w_s[j], sgu_b_s[j].T, sgu_w_out[j].astype(BF16), sgu_b_out[j])
        x = _ffn(x, norm_ffn[i], sc2, sh2, g2, ffn_w_gate[i].astype(BF16), ffn_w_up[i].astype(BF16), ffn_conv_w[i],
                 ffn_conv_b[i], ffn_w_down[i].astype(BF16), final_norm, final=(i == DEPTH - 1))
    return x
```

```python
import functools

import numpy as np
import jax
import jax.numpy as jnp
from jax import lax
from jax.experimental import pallas as pl
from jax.experimental.pallas import tpu as pltpu

F32 = jnp.float32
BF16 = jnp.bfloat16
HI = lax.Precision.HIGHEST

D_MODEL = 1024
DEPTH = 2
HEAD_DIM = 64
NSA_HEADS = 8
NSA_KV_HEADS = 2
NSA_HPG = NSA_HEADS // NSA_KV_HEADS
CMP_STRIDE = 16
CMP_LEN = 2 * CMP_STRIDE
CMP_HIDDEN = 128
SLC_BLOCK = 64
SLC_TOPK = 16
WINDOW = 512
Q_BLOCK = 128
FORCE = 1e4
RWKV_HEADS = 8
RWKV_DIM = RWKV_HEADS * HEAD_DIM
DECAY_LORA = 64
AAA_LORA = 64
GATE_LORA = 160
DECAY_SCALE = 0.6065306597126334
RWKV_GN_EPS = 64e-5
SGU_CHUNK = 128
SGU_GROUPS = 8
SGU_DIM = 2048
FFN_DIM = 2816
CONV_WIDTH = 3
NORM_EPS = 1e-6
LN_EPS = 1e-5
NEG_INF = -1e30
LOG2E = 1.4426950408889634
NSA_Q_DIM = NSA_HEADS * HEAD_DIM
NSA_KV_DIM = NSA_KV_HEADS * HEAD_DIM
NSA_GATE_DIM = NSA_HEADS * 3
NSA_COLS = NSA_Q_DIM + 6 * NSA_KV_DIM + NSA_GATE_DIM

LANES = 128
SUBLANES = 8
VMEM_LIMIT_BYTES = 52 * 1024 * 1024

ROW_TILE = 512
COL_CHUNK = 512
NSA_COLS_PAD = 1408
RWKV_LORA_PAD = 128
RWKV_GATE_PAD = 256
RWKV_COLS_PAD = 3 * RWKV_DIM + 2 * RWKV_LORA_PAD + RWKV_GATE_PAD
SLC_KEY_TILE = 256
NS_PAD = LANES
POS_FEATS = 16
RWKV_CHUNK = 64
RWKV_SUB = 16
RWKV_PACK = 4
RWKV_STEP_ROWS = 256
RWKV_BATCH_PER_STEP = 2
FFN_COL_CHUNK = 256
FFN_HALO_ROWS = 16


def _cparams(*sem):
    return pltpu.CompilerParams(dimension_semantics=sem, vmem_limit_bytes=VMEM_LIMIT_BYTES)


def _resident(shape):
    return pl.BlockSpec(shape, lambda *_: (0,) * len(shape), pipeline_mode=pl.Buffered(1))


def _gelu(x):
    return 0.5 * x * (1.0 + jnp.tanh(0.7978845608028654 * (x + 0.044715 * (x * x * x))))


def _rms(x, w):
    return x * lax.rsqrt(jnp.mean(x * x, axis=-1, keepdims=True) + NORM_EPS) * w


def _prev_rows(cur, halo, shift):
    tm = cur.shape[0]
    ext = jnp.concatenate([halo, cur], axis=0)
    return ext[SUBLANES - shift:SUBLANES - shift + tm]


def _split3(x):
    hi = x.astype(BF16)
    r1 = x - hi.astype(F32)
    mid = r1.astype(BF16)
    return hi, mid, (r1 - mid.astype(F32)).astype(BF16)


def _with_pos_features(k, pos):
    lane = lax.broadcasted_iota(jnp.int32, (k.shape[0], POS_FEATS), 1)
    feats = jnp.where(lane < 3, pos // SLC_BLOCK, jnp.where(lane < 6, pos % SLC_BLOCK, 0))
    return jnp.concatenate([k, feats.astype(F32)], axis=1).astype(BF16)


def _dot_exact_rhs(x, m):
    return jnp.dot(jnp.concatenate(_split3(x), axis=1), jnp.concatenate([m, m, m], axis=0),
                   preferred_element_type=F32)


def _dot_exact_lhs(m, x):
    return jnp.dot(jnp.concatenate([m, m, m], axis=1), jnp.concatenate(_split3(x), axis=0),
                   preferred_element_type=F32)


def _ada_kernel(c_ref, w_ref, b_ref, o_ref):
    c = c_ref[...]
    cond = c * jax.nn.sigmoid(c)
    o_ref[0] = jnp.dot(cond, w_ref[0], precision=HI, preferred_element_type=F32) + b_ref[0]


def _ada_mod(c, ada_w, ada_b):
    B, D = c.shape
    depth, _, N = ada_w.shape
    rows = -(-B // SUBLANES) * SUBLANES
    cp = jnp.zeros((rows, D), F32).at[:B].set(c)
    tn = N // 4
    out = pl.pallas_call(
        _ada_kernel,
        grid=(depth, N // tn),
        in_specs=[pl.BlockSpec((rows, D), lambda i, j: (0, 0)),
                  pl.BlockSpec((1, D, tn), lambda i, j: (i, 0, j)),
                  pl.BlockSpec((1, 1, tn), lambda i, j: (i, 0, j))],
        out_specs=pl.BlockSpec((1, rows, tn), lambda i, j: (i, 0, j)),
        out_shape=jax.ShapeDtypeStruct((depth, rows, N), F32),
        compiler_params=_cparams("parallel", "parallel"),
        name="ada_mod",
    )(cp, ada_w, ada_b.reshape(depth, 1, N))
    return out[:, :B]


def _in_proj_kernel(x_ref, nw_ref, sc_ref, sh_ref, w_ref,
                    qt_ref, zg_ref, kcs_ref, vcs_ref, ks_ref, vst_ref, kw_ref, vwt_ref, zr_ref):
    G, HP, dk, Q = NSA_KV_HEADS, NSA_HPG, HEAD_DIM, Q_BLOCK
    h = (_rms(x_ref[0], nw_ref[...]) * (1.0 + sc_ref[0]) + sh_ref[0]).astype(BF16)
    tm = h.shape[0]
    zq = jnp.dot(h, w_ref[:, 0:NSA_Q_DIM], preferred_element_type=F32) * (dk ** -0.5 * LOG2E)
    for nq in range(tm // Q):
        blk = zq[nq * Q:(nq + 1) * Q]
        for g in range(G):
            heads = [jnp.transpose(blk[:, (g * HP + hh) * dk:(g * HP + hh + 1) * dk]) for hh in range(HP)]
            qt_ref[0, g, nq] = jnp.concatenate(heads, axis=1).astype(BF16)
    kv_w = NSA_COLS_PAD - NSA_Q_DIM
    zkv = jnp.dot(h, w_ref[:, NSA_Q_DIM:NSA_COLS_PAD], preferred_element_type=F32)
    sec = lambda i, g: zkv[:, i * NSA_KV_DIM + g * dk:i * NSA_KV_DIM + (g + 1) * dk]
    pos = pl.program_id(1) * tm + lax.broadcasted_iota(jnp.int32, (tm, 1), 0)
    for g in range(G):
        kcs_ref[0, g] = sec(0, g)
        vcs_ref[0, g] = sec(1, g)
        ks_ref[0, g] = _with_pos_features(sec(2, g), pos)
        vst_ref[0, g] = jnp.transpose(sec(3, g)).astype(BF16)
        kw_ref[0, g] = _with_pos_features(sec(4, g), pos)
        vwt_ref[0, g] = jnp.transpose(sec(5, g)).astype(BF16)
    zg_ref[0] = zkv[:, 6 * NSA_KV_DIM:kv_w]
    n = zr_ref.shape[2]
    for c0 in range(0, n, COL_CHUNK):
        zr_ref[0, :, c0:c0 + COL_CHUNK] = jnp.dot(h, w_ref[:, NSA_COLS_PAD + c0:NSA_COLS_PAD + c0 + COL_CHUNK],
                                                  preferred_element_type=F32)


def _in_proj(x, nw, sc, sh, w):
    B, T, D = x.shape
    G, HP, dk, Q = NSA_KV_HEADS, NSA_HPG, HEAD_DIM, Q_BLOCK
    N = w.shape[1]
    tm = min(ROW_TILE, T)
    nq = tm // Q
    row = lambda n: pl.BlockSpec((1, tm, n), lambda b, i: (b, i, 0))
    per_b = pl.BlockSpec((1, 1, D), lambda b, i: (b, 0, 0))
    nat = pl.BlockSpec((1, G, tm, dk), lambda b, i: (b, 0, i, 0))
    tr = pl.BlockSpec((1, G, dk, tm), lambda b, i: (b, 0, 0, i))
    nat_s = jax.ShapeDtypeStruct((B, G, T, dk), F32)
    key = pl.BlockSpec((1, G, tm, dk + POS_FEATS), lambda b, i: (b, 0, i, 0))
    key_s = jax.ShapeDtypeStruct((B, G, T, dk + POS_FEATS), BF16)
    tr_s = jax.ShapeDtypeStruct((B, G, dk, T), BF16)
    return pl.pallas_call(
        _in_proj_kernel,
        grid=(B, T // tm),
        in_specs=[row(D), _resident((1, D)), per_b, per_b, _resident((D, N))],
        out_specs=[pl.BlockSpec((1, G, nq, dk, HP * Q), lambda b, i: (b, 0, i, 0, 0)),
                   row(NSA_COLS_PAD - NSA_Q_DIM - 6 * NSA_KV_DIM), nat, nat, key, tr, key, tr, row(RWKV_COLS_PAD)],
        out_shape=[jax.ShapeDtypeStruct((B, G, T // Q, dk, HP * Q), BF16),
                   jax.ShapeDtypeStruct((B, T, NSA_COLS_PAD - NSA_Q_DIM - 6 * NSA_KV_DIM), F32),
                   nat_s, nat_s, key_s, tr_s, key_s, tr_s,
                   jax.ShapeDtypeStruct((B, T, RWKV_COLS_PAD), F32)],
        compiler_params=_cparams("parallel", "parallel"),
        name="in_proj",
    )(x, nw.reshape(1, D), sc, sh, w)


def _nsa_compress_kernel(xk_ref, xv_ref, pe_ref, w1_ref, w2_ref, kc_ref, vct_ref):
    half = CMP_STRIDE * HEAD_DIM

    def mlp(x, s):
        pe = pe_ref[s]
        xa = (x + pe[:, :half]).astype(BF16)
        xb = (x + pe[:, half:]).astype(BF16)
        w1 = w1_ref[s]
        p = jnp.dot(xa, w1[:half], preferred_element_type=F32)
        q = jnp.dot(xb, w1[half:], preferred_element_type=F32)
        hid = _gelu(p + pltpu.roll(q, x.shape[0] - 1, 0))
        return jnp.dot(hid.astype(BF16), w2_ref[s], preferred_element_type=F32)

    nc = xk_ref.shape[1]
    cmp_end = lax.broadcasted_iota(jnp.int32, (nc, 1), 0) * CMP_STRIDE + (CMP_LEN - 1)
    kc_ref[0] = _with_pos_features(mlp(xk_ref[0], 0), cmp_end)
    vct_ref[0] = jnp.transpose(mlp(xv_ref[0], 1).astype(BF16))


def _nsa_compress(xk, xv, pe, w1, w2):
    BG, NC, W = xk.shape
    src = pl.BlockSpec((1, NC, W), lambda b: (b, 0, 0))
    return pl.pallas_call(
        _nsa_compress_kernel,
        grid=(BG,),
        in_specs=[src, src, _resident((2, 1, 2 * W)), _resident((2, 2 * W, CMP_HIDDEN)),
                  _resident((2, CMP_HIDDEN, HEAD_DIM))],
        out_specs=[pl.BlockSpec((1, NC, HEAD_DIM + POS_FEATS), lambda b: (b, 0, 0)),
                   pl.BlockSpec((1, HEAD_DIM, NC), lambda b: (b, 0, 0))],
        out_shape=[jax.ShapeDtypeStruct((BG, NC, HEAD_DIM + POS_FEATS), BF16),
                   jax.ShapeDtypeStruct((BG, HEAD_DIM, NC), BF16)],
        compiler_params=_cparams("parallel"),
        name="nsa_compress",
    )(xk, xv, pe, w1, w2)


def _nsa_attn_kernel(qt_ref, glt_ref, kc_ref, vct_ref, ks_ref, vst_ref, kw_ref, vwt_ref, aggt_ref,
                     o_ref, selt_ref, sa_ref, sb_ref, pa_ref, pb_ref, *, n_slc, top_k):
    g = pl.program_id(1)
    n = pl.program_id(2)
    Q, HP, R = Q_BLOCK, NSA_HPG, NSA_HPG * Q_BLOCK
    t0 = pl.multiple_of(n * Q, Q)
    qt = qt_ref[0, 0, 0]

    def per_head(a):
        return jnp.concatenate([a] * HP, axis=1)

    slope_g = jnp.float32(1.0)
    for gi in range(1, NSA_KV_HEADS):
        slope_g = jnp.where(g == gi, jnp.float32(2.0 ** (-HP * gi)), slope_g)
    lane_h = lax.broadcasted_iota(jnp.int32, (1, R), 1) // Q
    slope = jnp.full((1, R), LOG2E * 2.0 ** -HP, F32)
    for h in range(HP - 1):
        slope = jnp.where(lane_h == h, jnp.float32(LOG2E * 2.0 ** -(h + 1)), slope)
    slope = slope * slope_g
    tq = t0 + lax.broadcasted_iota(jnp.int32, (1, Q), 1)
    s1 = slope.astype(BF16)
    s2 = (slope - s1.astype(F32)).astype(BF16)
    s3 = (slope - s1.astype(F32) - s2.astype(F32)).astype(BF16)
    pieces = [p_.astype(F32) for p_ in (s1, s2, s3)]
    srows = [p_ * float(SLC_BLOCK) for p_ in pieces] + pieces + [jnp.zeros((POS_FEATS - 6, R), F32)]
    qt = jnp.concatenate([qt, jnp.concatenate(srows, axis=0).astype(BF16)], axis=0)

    ncp = kc_ref.shape[2]
    s = jnp.dot(kc_ref[0, 0], qt, preferred_element_type=F32)
    cmp_end = lax.broadcasted_iota(jnp.int32, (ncp, 1), 0) * CMP_STRIDE + (CMP_LEN - 1)
    dist = tq - cmp_end
    bias = per_head(jnp.where(dist >= 0, 0.0, NEG_INF))
    valid = bias > 0.5 * NEG_INF
    s = s + bias
    e = jnp.where(valid, jnp.exp2(s - jnp.max(s, axis=0, keepdims=True)), 0.0)
    l = jnp.sum(e, axis=0, keepdims=True)
    p = e * (1.0 / jnp.where(l > 0.0, l, 1.0))
    o_c = jnp.dot(vct_ref[0, 0], p.astype(BF16), preferred_element_type=F32)

    WK = WINDOW + Q
    w0 = pl.multiple_of(jnp.maximum(t0 - WINDOW, 0), Q)
    sw = jnp.dot(kw_ref[0, 0, pl.ds(w0, WK), :], qt, preferred_element_type=F32)
    dw = tq - (w0 + lax.broadcasted_iota(jnp.int32, (WK, 1), 0))
    okw = (dw >= 0) & (dw < WINDOW)
    sw = sw + per_head(jnp.where(okw, 0.0, NEG_INF))
    ew = jnp.exp2(sw - jnp.max(sw, axis=0, keepdims=True))
    lw = jnp.sum(ew, axis=0, keepdims=True)
    o_w = jnp.dot(vwt_ref[0, 0, :, pl.ds(w0, WK)], ew.astype(BF16), preferred_element_type=F32) * (1.0 / lw)

    psum = p[:, 0:Q]
    for h in range(1, HP):
        psum = psum + p[:, h * Q:(h + 1) * Q]
    p_hi = psum.astype(BF16)
    p_lo = (psum - p_hi.astype(F32)).astype(BF16)
    aggt = aggt_ref[...]
    imp = jnp.dot(jnp.concatenate([aggt, aggt], axis=1), jnp.concatenate([p_hi, p_lo], axis=0),
                  preferred_element_type=F32)
    blk = lax.broadcasted_iota(jnp.int32, (NS_PAD, 1), 0)
    jt = tq // SLC_BLOCK
    forced = (blk == 0) | (blk == jt) | (blk == jt - 1)
    imp = jnp.where(blk > jt, -FORCE, jnp.where(forced, FORCE, imp))
    vals = jnp.where(blk < n_slc, imp, -jnp.inf)
    blk_f = blk.astype(F32)
    sel = jnp.zeros((NS_PAD, Q), F32)
    for _ in range(top_k):
        vmax = jnp.max(vals, axis=0, keepdims=True)
        first = jnp.min(jnp.where(vals == vmax, blk_f, float(NS_PAD)), axis=0, keepdims=True)
        pick = blk_f == first
        sel = jnp.where(pick, 1.0, sel)
        vals = jnp.where(pick, -jnp.inf, vals)
    selt_ref[...] = sel

    TK = SLC_KEY_TILE
    nb = TK // SLC_BLOCK
    n_tiles = (t0 + Q + TK - 1) // TK
    last_tile = ks_ref.shape[2] // TK - 1
    key_in_blk = lax.broadcasted_iota(jnp.int32, (SLC_BLOCK, 1), 0)

    def qk(kt, s_ref):
        k0 = pl.multiple_of(jnp.minimum(kt, last_tile) * TK, TK)
        s_ref[...] = jnp.dot(ks_ref[0, 0, pl.ds(k0, TK), :], qt, preferred_element_type=F32)

    def pv(kt, p_ref):
        k0 = pl.multiple_of(jnp.clip(kt, 0, last_tile) * TK, TK)
        return jnp.dot(vst_ref[0, 0, :, pl.ds(k0, TK)], p_ref[...], preferred_element_type=F32)

    def softmax_tile(kt, s_ref, p_ref, m_i, l_i):
        dms = []
        for j in range(nb):
            rel = (kt * TK + j * SLC_BLOCK + key_in_blk) - tq
            chosen = selt_ref[pl.ds(jnp.minimum(kt * nb + j, NS_PAD - 1), 1), :] > 0.5
            dms.append(jnp.where(chosen & (rel <= 0), 0.0, NEG_INF))
        sc = s_ref[...] + per_head(jnp.concatenate(dms, axis=0))
        m_new = jnp.maximum(m_i, jnp.max(sc, axis=0, keepdims=True))
        alpha = jnp.exp2(m_i - m_new)
        pe_ = jnp.exp2(sc - m_new)
        p_ref[...] = pe_.astype(BF16)
        return m_new, alpha, alpha * l_i + jnp.sum(pe_, axis=0, keepdims=True)

    def slc_pair(j, carry):
        m_i, l_i, acc = carry
        ka = 2 * j
        qk(ka + 1, sb_ref)
        acc = acc + pv(ka - 1, pb_ref)
        m_i, alpha, l_i = softmax_tile(ka, sa_ref, pa_ref, m_i, l_i)
        acc = alpha * acc
        qk(ka + 2, sa_ref)
        acc = acc + pv(ka, pa_ref)
        m_i, alpha, l_i = softmax_tile(ka + 1, sb_ref, pb_ref, m_i, l_i)
        return m_i, l_i, alpha * acc

    pb_ref[...] = jnp.zeros_like(pb_ref)
    qk(0, sa_ref)
    n_pairs = (n_tiles + 1) // 2
    init = (jnp.full((1, R), NEG_INF, F32), jnp.zeros((1, R), F32), jnp.zeros((HEAD_DIM, R), F32))
    _, l_s, acc_s = lax.fori_loop(0, n_pairs, slc_pair, init)
    acc_s = acc_s + pv(2 * n_pairs - 1, pb_ref)

    gates = jax.nn.sigmoid(glt_ref[0, 0, 0])
    o_ref[0, 0, 0] = gates[0:1] * o_c + gates[1:2] * (acc_s * (1.0 / l_s)) + gates[2:3] * o_w


def _nsa_attention(qt, glt, kc, vct, ks, vst, kw, vwt, aggt, *, n_slc, top_k):
    B, G, NQ, dk, R = qt.shape
    ncp = kc.shape[2]
    T = ks.shape[2]
    assert T >= WINDOW + Q_BLOCK
    per_bg = lambda b, g, n: (b, g, 0, 0)
    per_q = lambda rows: pl.BlockSpec((1, 1, 1, rows, R), lambda b, g, n: (b, g, n, 0, 0))
    return pl.pallas_call(
        functools.partial(_nsa_attn_kernel, n_slc=n_slc, top_k=top_k),
        grid=(B, G, NQ),
        in_specs=[per_q(dk), per_q(3),
                  pl.BlockSpec((1, 1, ncp, dk + POS_FEATS), per_bg),
                  pl.BlockSpec((1, 1, dk, ncp), per_bg),
                  pl.BlockSpec((1, 1, T, dk + POS_FEATS), per_bg),
                  pl.BlockSpec((1, 1, dk, T), per_bg),
                  pl.BlockSpec((1, 1, T, dk + POS_FEATS), per_bg),
                  pl.BlockSpec((1, 1, dk, T), per_bg),
                  _resident((NS_PAD, ncp))],
        out_specs=per_q(dk),
        out_shape=jax.ShapeDtypeStruct((B, G, NQ, dk, R), F32),
        scratch_shapes=[pltpu.VMEM((NS_PAD, Q_BLOCK), F32),
                        pltpu.VMEM((SLC_KEY_TILE, R), F32), pltpu.VMEM((SLC_KEY_TILE, R), F32),
                        pltpu.VMEM((SLC_KEY_TILE, R), BF16), pltpu.VMEM((SLC_KEY_TILE, R), BF16)],
        compiler_params=_cparams("parallel", "parallel", "arbitrary"),
        name="nsa_attention",
    )(qt, glt, kc, vct, ks, vst, kw, vwt, aggt)


def _rwkv_pre_kernel(z_ref, halo_ref, mu_ref, w0_ref, a0_ref, kk_ref, ka_ref, rk_ref, wup_ref, aup_ref, gup_ref,
                     ones_ref, csum_ref,
                     kap_ref, r_ref, v_ref, kend_ref, bend_ref, bt_ref, kt_ref, gend_ref, bonus_ref, g_ref):
    i = pl.program_id(1)
    z = z_ref[0]
    tm = z.shape[0]
    halo = jnp.where(i > 0, halo_ref[0], 0.0)
    zs = z + (_prev_rows(z, halo, 1) - z) * mu_ref[...]
    D = RWKV_DIM
    r, k, v = zs[:, 0:D], zs[:, D:2 * D], zs[:, 2 * D:3 * D]
    o1 = 3 * D
    o2 = o1 + RWKV_LORA_PAD
    o3 = o2 + RWKV_LORA_PAD
    wl, al, gl = zs[:, o1:o2], zs[:, o2:o3], zs[:, o3:]
    dot = functools.partial(jnp.dot, precision=HI, preferred_element_type=F32)
    lw = -DECAY_SCALE * jax.nn.sigmoid(w0_ref[...] + dot(jnp.tanh(wl), wup_ref[...]))
    a = jax.nn.sigmoid(a0_ref[...] + dot(al, aup_ref[...]))
    kt = k * (1.0 + (a - 1.0) * ka_ref[...])
    kk = k * kk_ref[...]
    head_sums = _dot_exact_rhs(jnp.concatenate([kk * kk, r * kt * rk_ref[...]], axis=0), ones_ref[...])
    kk = kk / jnp.maximum(jnp.sqrt(head_sums[:tm]), 1e-12)
    b = kk * a
    sums = _dot_exact_lhs(csum_ref[...], lw)
    cum, tot = sums[:tm], sums[tm:]
    g_inv = jnp.exp(-cum)
    tail = jnp.exp(tot - cum)
    kap_ref[0] = kk * jnp.exp(cum - lw)
    r_ref[0] = r * jnp.exp(cum)
    v_ref[0] = v
    kend_ref[0] = kt * tail
    bend_ref[0] = b * tail
    b_h, k_h, g_end = b * g_inv, kt * g_inv, jnp.exp(tot)
    C = RWKV_CHUNK
    for c in range(tm // C):
        bt_ref[0, c] = jnp.transpose(b_h[c * C:(c + 1) * C])
        kt_ref[0, c] = jnp.transpose(k_h[c * C:(c + 1) * C])
    gend_ref[0] = jnp.concatenate([g_end[c * C:c * C + 1] for c in range(tm // C)], axis=0)
    bonus_ref[0] = head_sums[tm:] * v
    g_ref[0] = dot(jax.nn.sigmoid(gl), gup_ref[...])


def _rwkv_pre(z, mu, w0, a0, k_k, k_a, r_k, w_up, a_up, g_up):
    B, T, W = z.shape
    D = RWKV_DIM
    tm = min(ROW_TILE, T)
    hb = tm // SUBLANES
    C = RWKV_CHUNK
    ones_bd = jnp.asarray(np.kron(np.eye(RWKV_HEADS), np.ones((HEAD_DIM, HEAD_DIM))), BF16)
    chunks = np.eye(tm // C)
    csum = jnp.asarray(np.concatenate([np.kron(chunks, np.tril(np.ones((C, C)))),
                                       np.kron(chunks, np.ones((C, C)))], axis=0), BF16)
    out = jax.ShapeDtypeStruct((B, T, D), F32)
    row = pl.BlockSpec((1, tm, D), lambda b, i: (b, i, 0))
    tr = pl.BlockSpec((1, tm // C, D, C), lambda b, i: (b, i, 0, 0))
    tr_s = jax.ShapeDtypeStruct((B, T // C, D, C), F32)
    return pl.pallas_call(
        _rwkv_pre_kernel,
        grid=(B, T // tm),
        in_specs=[pl.BlockSpec((1, tm, W), lambda b, i: (b, i, 0)),
                  pl.BlockSpec((1, SUBLANES, W), lambda b, i: (b, jnp.maximum(i * hb - 1, 0), 0)),
                  _resident((1, W)), _resident((1, D)), _resident((1, D)), _resident((1, D)), _resident((1, D)),
                  _resident((1, D)), _resident((RWKV_LORA_PAD, D)), _resident((RWKV_LORA_PAD, D)),
                  _resident((RWKV_GATE_PAD, D)), _resident((D, D)), _resident((2 * tm, tm))],
        out_specs=[row] * 5 + [tr, tr, pl.BlockSpec((1, tm // C, D), lambda b, i: (b, i, 0)), row, row],
        out_shape=[out] * 5 + [tr_s, tr_s, jax.ShapeDtypeStruct((B, T // C, D), F32), out, out],
        compiler_params=_cparams("parallel", "parallel"),
        name="rwkv_pre",
    )(z, z, mu.reshape(1, W), w0.reshape(1, D), a0.reshape(1, D), k_k.reshape(1, D), k_a.reshape(1, D),
      r_k.reshape(1, D), w_up, a_up, g_up, ones_bd, csum)


def _rwkv_core_kernel(kap_ref, r_ref, v_ref, kend_ref, bend_ref, bt_ref, kt_ref, gend_ref, o_ref, h_ref):
    C, N, P = RWKV_CHUNK, HEAD_DIM, RWKV_PACK
    W = P * N
    chains = [(bi, slice(p * W, (p + 1) * W)) for bi in range(kap_ref.shape[0]) for p in range(kap_ref.shape[2] // W)]

    @pl.when(pl.program_id(1) == 0)
    def _():
        h_ref[...] = jnp.zeros_like(h_ref)

    ri = lax.broadcasted_iota(jnp.int32, (C, W), 0)
    cj = lax.broadcasted_iota(jnp.int32, (C, W), 1) % N
    tril = (ri >= cj).astype(F32)
    stril = (ri > cj).astype(F32)
    eye = (ri == cj).astype(F32)
    diag_blk = ((ri // RWKV_SUB) == (cj // RWKV_SUB)).astype(F32)
    same_head = (lax.broadcasted_iota(jnp.int32, (W, W), 0) // N) == (lax.broadcasted_iota(jnp.int32, (W, W), 1) // N)

    def split(x):
        hi = x.astype(BF16)
        return hi, (x - hi.astype(F32)).astype(BF16)

    def lhs2(x):
        hi, lo = split(x)
        return jnp.concatenate([hi, lo], axis=0), hi

    def bd(x):
        blk = lambda a: jnp.where(same_head, jnp.concatenate([a] * P, axis=0), jnp.zeros((), BF16))
        hi, lo = split(x)
        return blk(hi), blk(lo)

    def bd_t(xt):
        return split(jnp.where(same_head, jnp.concatenate([xt] * P, axis=1), 0.0))

    def mmw(l2, w):
        (x2, x_hi), (w_hi, w_lo) = l2, w
        m = x_hi.shape[0]
        t = jnp.dot(x2, w_hi, preferred_element_type=F32)
        return t[:m] + t[m:] + jnp.dot(x_hi, w_lo, preferred_element_type=F32)

    def mmp(x, y):
        return mmw(lhs2(x), bd(y))

    def gm(f, *cols):
        return [f(*args) for args in zip(*cols)]

    def chunk(c, hs):
        rows = pl.ds(pl.multiple_of(c * C, C), C)
        v = [v_ref[bi, rows, ln] for bi, ln in chains]
        lhs = [lhs2(jnp.concatenate([kap_ref[bi, rows, ln], r_ref[bi, rows, ln]], axis=0)) for bi, ln in chains]
        ab = gm(lambda l, ch: mmw(l, bd_t(bt_ref[ch[0], c, ch[1], :])), lhs, chains)
        ak = gm(lambda l, ch: mmw(l, bd_t(kt_ref[ch[0], c, ch[1], :])), lhs, chains)
        a_kb = [stril * x[:C] for x in ab]
        a_rb = [tril * x[C:] for x in ab]
        a_kr = [jnp.concatenate([stril * x[:C], tril * x[C:]], axis=0) for x in ak]
        d = [x * diag_blk for x in a_kb]
        e = gm(lambda x, y: x - y, a_kb, d)
        d_inv = [eye - x for x in d]
        pw = gm(mmp, d, d)
        steps = int(np.log2(RWKV_SUB)) - 1
        for s_ in range(steps):
            d_inv = gm(lambda x, y: mmp(x, eye + y), d_inv, pw)
            if s_ + 1 < steps:
                pw = gm(mmp, pw, pw)
        nb = gm(mmp, d_inv, e)
        t_inv = [eye - x for x in nb]
        pw = gm(mmp, nb, nb)
        bsteps = int(np.log2(C // RWKV_SUB)) - 1
        for s_ in range(bsteps):
            t_inv = gm(lambda x, y: mmp(x, eye + y), t_inv, pw)
            if s_ + 1 < bsteps:
                pw = gm(mmp, pw, pw)
        t_inv = gm(mmp, t_inv, d_inv)
        sh = gm(lambda l, h: mmw(l, bd(h)), lhs, hs)
        av = gm(mmp, a_kr, v)
        u = gm(lambda t, x, y: mmp(t, x[:C] + y[:C]), t_inv, sh, av)
        y = gm(lambda x, z, a, uu: x[C:] + z[C:] - mmp(a, uu), sh, av, a_rb, u)
        for (bi, ln), yy in zip(chains, y):
            o_ref[bi, rows, ln] = yy
        zero = jnp.zeros((C, W), F32)
        left = [jnp.concatenate([eye * gend_ref[bi, 0, pl.ds(c, 1), ln], kend_ref[bi, rows, ln],
                                 bend_ref[bi, rows, ln], zero], axis=0) for bi, ln in chains]
        right = gm(lambda h, vv, uu: split(jnp.concatenate([h, vv, -uu, zero], axis=0)), hs, v, u)
        full = gm(lambda l, rr: jnp.where(same_head, mmw(lhs2(jnp.transpose(l)), rr), 0.0), left, right)
        return tuple(sum(f[s_ * N:(s_ + 1) * N] for s_ in range(1, P)) + f[0:N] for f in full)

    hs = lax.fori_loop(0, kap_ref.shape[1] // C, chunk, tuple(h_ref[p] for p in range(len(chains))))
    for p in range(len(chains)):
        h_ref[p] = hs[p]


def _rwkv_core(kap, r, v, kend, bend, bt, kt, gend):
    B, T, D = kap.shape
    W = RWKV_PACK * HEAD_DIM
    assert D % W == 0 and W == 4 * RWKV_CHUNK
    bb = RWKV_BATCH_PER_STEP if B % RWKV_BATCH_PER_STEP == 0 else 1
    tc = min(RWKV_STEP_ROWS, T)
    nch = tc // RWKV_CHUNK
    seq = pl.BlockSpec((bb, tc, D), lambda b, i: (b, i, 0))
    tr = pl.BlockSpec((bb, nch, D, RWKV_CHUNK), lambda b, i: (b, i, 0, 0))
    return pl.pallas_call(
        _rwkv_core_kernel,
        grid=(B // bb, T // tc),
        in_specs=[seq] * 5 + [tr] * 2 + [pl.BlockSpec((bb, 1, nch, D), lambda b, i: (b, i, 0, 0))],
        out_specs=seq,
        out_shape=jax.ShapeDtypeStruct((B, T, D), F32),
        scratch_shapes=[pltpu.VMEM((bb * (D // W), HEAD_DIM, W), F32)],
        compiler_params=_cparams("parallel", "arbitrary"),
        name="rwkv_core",
    )(kap, r, v, kend, bend, bt, kt, gend.reshape(B, T // tc, nch, D))


def _out_proj_kernel(ot_ref, y_ref, bonus_ref, gg_ref, gw_ref, gb_ref, ones_ref, wa_ref, wb_ref, x_ref, g_ref,
                     o_ref):
    G, HP, Q = NSA_KV_HEADS, NSA_HPG, Q_BLOCK
    blocks = []
    for nq in range(ot_ref.shape[2]):
        heads = [jnp.transpose(ot_ref[0, g, nq][:, hh * Q:(hh + 1) * Q]) for g in range(G) for hh in range(HP)]
        blocks.append(jnp.concatenate(heads, axis=1))
    ya = jnp.concatenate(blocks, axis=0).astype(BF16)
    y = y_ref[0]
    inv_n = 1.0 / HEAD_DIM
    dlt = y - _dot_exact_rhs(y, ones_ref[...]) * inv_n
    var = _dot_exact_rhs(dlt * dlt, ones_ref[...]) * inv_n
    yb = (dlt * lax.rsqrt(var + RWKV_GN_EPS) * gw_ref[...] + gb_ref[...] + bonus_ref[0]) * gg_ref[0]
    o = jnp.dot(ya, wa_ref[...], preferred_element_type=F32)
    o = o + jnp.dot(yb.astype(BF16), wb_ref[...], preferred_element_type=F32)
    o_ref[0] = x_ref[0] + g_ref[0] * o


def _out_proj(ot, y, bonus, gg, gn_w, gn_b, wa, wb, x, gate):
    B, T, D = x.shape
    G, NQ, dk, R = ot.shape[1:]
    tm = min(ROW_TILE, T)
    ka, kb = wa.shape[0], y.shape[-1]
    ones_bd = jnp.asarray(np.kron(np.eye(kb // HEAD_DIM), np.ones((HEAD_DIM, HEAD_DIM))), BF16)
    row = lambda w: pl.BlockSpec((1, tm, w), lambda b, i: (b, i, 0))
    return pl.pallas_call(
        _out_proj_kernel,
        grid=(B, T // tm),
        in_specs=[pl.BlockSpec((1, G, tm // Q_BLOCK, dk, R), lambda b, i: (b, 0, i, 0, 0)),
                  row(kb), row(kb), row(kb), _resident((1, kb)), _resident((1, kb)), _resident((kb, kb)),
                  _resident((ka, D)), _resident((kb, D)), row(D),
                  pl.BlockSpec((1, 1, D), lambda b, i: (b, 0, 0))],
        out_specs=row(D),
        out_shape=jax.ShapeDtypeStruct((B, T, D), F32),
        compiler_params=_cparams("parallel", "parallel"),
        name="out_proj",
    )(ot, y, bonus, gg, gn_w.reshape(1, kb), gn_b.reshape(1, kb), ones_bd, wa, wb, x, gate)


def _sgu_kernel(x_ref, nw_ref, sc_ref, sh_ref, g_ref, wi_ref, bi_ref, vnw_ref, vnb_ref, ws_ref, bst_ref,
                wo_ref, bo_ref, o_ref, u_ref, v_ref, gated_ref):
    x = x_ref[0]
    tm = x.shape[0]
    E = SGU_DIM
    h = (_rms(x, nw_ref[...]) * (1.0 + sc_ref[0]) + sh_ref[0]).astype(BF16)
    for c0 in range(0, 2 * E, COL_CHUNK):
        zc = _gelu(jnp.dot(h, wi_ref[:, c0:c0 + COL_CHUNK], preferred_element_type=F32) + bi_ref[:, c0:c0 + COL_CHUNK])
        if c0 < E:
            u_ref[:, c0:c0 + COL_CHUNK] = zc
        else:
            v_ref[:, c0 - E:c0 - E + COL_CHUNK] = zc
    v = v_ref[...]
    mu = jnp.mean(v, axis=-1, keepdims=True)
    var = jnp.mean(jnp.square(v - mu), axis=-1, keepdims=True)
    vn = ((v - mu) * lax.rsqrt(var + LN_EPS) * vnw_ref[...] + vnb_ref[...]).astype(BF16)
    S = SGU_CHUNK
    gw = E // SGU_GROUPS
    causal = lax.broadcasted_iota(jnp.int32, (S, S), 0) >= lax.broadcasted_iota(jnp.int32, (S, S), 1)
    for gi in range(SGU_GROUPS):
        ws = jnp.where(causal, ws_ref[gi], 0.0).astype(BF16)
        bs = bst_ref[:, gi:gi + 1]
        for n in range(tm // S):
            sv = jnp.dot(ws, vn[n * S:(n + 1) * S, gi * gw:(gi + 1) * gw], preferred_element_type=F32) + bs
            u = u_ref[n * S:(n + 1) * S, gi * gw:(gi + 1) * gw]
            gated_ref[n * S:(n + 1) * S, gi * gw:(gi + 1) * gw] = (u * sv).astype(BF16)
    y = jnp.dot(gated_ref[...], wo_ref[...], preferred_element_type=F32) + bo_ref[...]
    o_ref[0] = x + g_ref[0] * y


def _sgu(x, nw, sc, sh, gate, w_in, b_in, vn_w, vn_b, w_s, b_s_t, w_out, b_out):
    B, T, D = x.shape
    E = SGU_DIM
    tm = min(ROW_TILE, T)
    assert tm % SGU_CHUNK == 0
    row = pl.BlockSpec((1, tm, D), lambda b, i: (b, i, 0))
    per_b = pl.BlockSpec((1, 1, D), lambda b, i: (b, 0, 0))
    return pl.pallas_call(
        _sgu_kernel,
        grid=(B, T // tm),
        in_specs=[row, _resident((1, D)), per_b, per_b, per_b,
                  _resident((D, 2 * E)), _resident((1, 2 * E)), _resident((1, E)), _resident((1, E)),
                  _resident((SGU_GROUPS, SGU_CHUNK, SGU_CHUNK)), _resident((SGU_CHUNK, SGU_GROUPS)),
                  _resident((E, D)), _resident((1, D))],
        out_specs=row,
        out_shape=jax.ShapeDtypeStruct((B, T, D), F32),
        scratch_shapes=[pltpu.VMEM((tm, E), F32), pltpu.VMEM((tm, E), F32), pltpu.VMEM((tm, E), BF16)],
        compiler_params=_cparams("parallel", "parallel"),
        name="sgu",
    )(x, nw.reshape(1, D), sc, sh, gate, w_in, b_in.reshape(1, 2 * E), vn_w.reshape(1, E), vn_b.reshape(1, E),
      w_s, b_s_t, w_out, b_out.reshape(1, D))


def _ffn_kernel(x_ref, halo_ref, nw_ref, sc_ref, sh_ref, g_ref, wg_ref, wu_ref, cw_ref, cb_ref, wd_ref, fn_ref,
                o_ref, act_ref, *, final):
    i = pl.program_id(1)
    x = x_ref[0]
    tm = x.shape[0]
    F = wg_ref.shape[1]
    HR = FFN_HALO_ROWS
    hx = (_rms(jnp.concatenate([halo_ref[0], x], axis=0), nw_ref[...]) * (1.0 + sc_ref[0]) + sh_ref[0]).astype(BF16)
    h = hx[HR:]
    for c0 in range(0, F, FFN_COL_CHUNK):
        cols = slice(c0, c0 + FFN_COL_CHUNK)
        ext = jnp.dot(hx, wg_ref[:, cols], preferred_element_type=F32)
        ext = jnp.concatenate([jnp.where(i > 0, ext[:HR], 0.0), ext[HR:]], axis=0)
        cw = cw_ref[:, cols]
        a = (cw[0:1] * ext[HR - 2:HR - 2 + tm] + cw[1:2] * ext[HR - 1:HR - 1 + tm] + cw[2:3] * ext[HR:]
             + cb_ref[:, cols])
        up = jnp.dot(h, wu_ref[:, cols], preferred_element_type=F32)
        act_ref[:, cols] = (_gelu(a) * up).astype(BF16)
    acc = jnp.dot(act_ref[...], wd_ref[...], preferred_element_type=F32)
    xn = x + g_ref[0] * acc
    if final:
        xn = _rms(xn, fn_ref[...])
    o_ref[0] = xn


def _ffn(x, nw, sc, sh, gate, w_gate, w_up, conv_w, conv_b, w_down, final_w, *, final):
    B, T, D = x.shape
    F = w_down.shape[0]
    tm = min(ROW_TILE, T)
    hb = tm // FFN_HALO_ROWS
    row = pl.BlockSpec((1, tm, D), lambda b, i: (b, i, 0))
    per_b = pl.BlockSpec((1, 1, D), lambda b, i: (b, 0, 0))
    return pl.pallas_call(
        functools.partial(_ffn_kernel, final=final),
        grid=(B, T // tm),
        in_specs=[row,
                  pl.BlockSpec((1, FFN_HALO_ROWS, D), lambda b, i: (b, jnp.maximum(i * hb - 1, 0), 0)),
                  _resident((1, D)), per_b, per_b, per_b,
                  _resident((D, F)), _resident((D, F)), _resident((CONV_WIDTH, F)), _resident((1, F)),
                  _resident((F, D)), _resident((1, D))],
        out_specs=row,
        out_shape=jax.ShapeDtypeStruct((B, T, D), F32),
        scratch_shapes=[pltpu.VMEM((tm, F), BF16)],
        compiler_params=_cparams("parallel", "parallel"),
        name="ffn",
    )(x, x, nw.reshape(1, D), sc, sh, gate, w_gate, w_up, conv_w, conv_b.reshape(1, F), w_down,
      final_w.reshape(1, D))


def _pad_cols(w, sizes, padded):
    parts, o = [], 0
    for s, p in zip(sizes, padded):
        parts.append(jnp.pad(w[..., o:o + s], [(0, 0)] * (w.ndim - 1) + [(0, p - s)]))
        o += s
    return jnp.concatenate(parts, axis=-1)


def _pad_rows(w, rows):
    return jnp.pad(w, ((0, rows - w.shape[0]), (0, 0)))


def _nsa_agg_t(T):
    n_cmp_pad = T // CMP_STRIDE
    n_slc = T // SLC_BLOCK
    c = np.arange(n_cmp_pad)
    s = np.arange(NS_PAD)
    cs, ce, ss = c * CMP_STRIDE, c * CMP_STRIDE + CMP_LEN - 1, s * SLC_BLOCK
    agg = (cs[None, :] < ss[:, None] + SLC_BLOCK) & (ce[None, :] >= ss[:, None]) & (s[:, None] < n_slc)
    agg &= (c[None, :] < n_cmp_pad - 1)
    return jnp.asarray(agg, BF16)


def _nsa_rwkv_mixer(x, nw, sc, sh, gate, w_in, cmp_pe, cmp_w1, cmp_w2, mu, w0, w_up, a0, a_up, g_up,
                    k_k, k_a, r_k, gn_w, gn_b, w_out):
    B, T, D = x.shape
    G, HPG, dk = NSA_KV_HEADS, NSA_HPG, HEAD_DIM
    n_slc = T // SLC_BLOCK
    NQ = T // Q_BLOCK
    assert n_slc <= NS_PAD and T % (CMP_STRIDE * LANES) == 0 and T % (2 * SLC_KEY_TILE) == 0

    rw_sizes = [RWKV_DIM] * 3 + [DECAY_LORA, AAA_LORA, GATE_LORA]
    rw_pads = [RWKV_DIM] * 3 + [RWKV_LORA_PAD, RWKV_LORA_PAD, RWKV_GATE_PAD]
    w_all = jnp.concatenate([jnp.pad(w_in[:, :NSA_COLS], ((0, 0), (0, NSA_COLS_PAD - NSA_COLS))),
                             _pad_cols(w_in[:, NSA_COLS:], rw_sizes, rw_pads)], axis=1).astype(BF16)
    qt, zg, kcs, vcs, ks, vst, kw, vwt, z_rw = _in_proj(x, nw, sc, sh, w_all)

    glt = zg[..., :NSA_GATE_DIM].reshape(B, NQ, Q_BLOCK, G, HPG, 3).transpose(0, 3, 1, 5, 4, 2)
    glt = glt.reshape(B, G, NQ, 3, HPG * Q_BLOCK)
    strides = lambda a: a.reshape(B * G, T // CMP_STRIDE, CMP_STRIDE * dk)
    kc, vct = _nsa_compress(strides(kcs), strides(vcs), cmp_pe.reshape(2, 1, CMP_LEN * dk),
                            cmp_w1.astype(BF16), cmp_w2.astype(BF16))
    o_t = _nsa_attention(qt, glt, kc.reshape(B, G, T // CMP_STRIDE, -1), vct.reshape(B, G, dk, T // CMP_STRIDE),
                         ks, vst, kw, vwt, _nsa_agg_t(T), n_slc=n_slc, top_k=min(SLC_TOPK, n_slc))

    kap, r, v, kend, bend, bt, kt, gend, bonus, gg = _rwkv_pre(
        z_rw, _pad_cols(mu, rw_sizes, rw_pads), w0, a0, k_k, k_a, r_k,
        _pad_rows(w_up, RWKV_LORA_PAD), _pad_rows(a_up, RWKV_LORA_PAD), _pad_rows(g_up, RWKV_GATE_PAD))
    y_b = _rwkv_core(kap, r, v, kend, bend, bt, kt, gend)

    w_out = w_out.astype(BF16)
    return _out_proj(o_t, y_b, bonus, gg, gn_w, gn_b, w_out[:NSA_Q_DIM], w_out[NSA_Q_DIM:], x, gate)


def kernel(x, c, ada_w, ada_b, norm_mix, norm_ffn, ffn_w_gate, ffn_w_up, ffn_conv_w, ffn_conv_b, ffn_w_down, ab_w_in, nsa_cmp_pe, nsa_cmp_w1, nsa_cmp_w2, rwkv_mu, rwkv_w0, rwkv_w_up, rwkv_a0, rwkv_a_up, rwkv_g_up, rwkv_k_k, rwkv_k_a, rwkv_r_k, rwkv_gn_w, rwkv_gn_b, ab_w_out, sgu_w_in, sgu_b_in, sgu_vn_w, sgu_vn_b, sgu_w_s, sgu_b_s, sgu_w_out, sgu_b_out, final_norm):
    B, T, D = x.shape
    mod = _ada_mod(c, ada_w, ada_b)
    for i in range(DEPTH):
        sh1, sc1, g1, sh2, sc2, g2 = [m.reshape(B, 1, D) for m in jnp.split(mod[i], 6, axis=-1)]
        j = i // 2
        if i % 2 == 0:
            x = _nsa_rwkv_mixer(x, norm_mix[i], sc1, sh1, g1, ab_w_in[j], nsa_cmp_pe[j], nsa_cmp_w1[j],
                                nsa_cmp_w2[j], rwkv_mu[j], rwkv_w0[j], rwkv_w_up[j], rwkv_a0[j], rwkv_a_up[j],
                                rwkv_g_up[j], rwkv_k_k[j], rwkv_k_a[j], rwkv_r_k[j], rwkv_gn_w[j], rwkv_gn_b[j],
                                ab_w_out[j])
        else:
            x = _sgu(x, norm_mix[i], sc1, sh1, g1, sgu_w_in[j].astype(BF16), sgu_b_in[j], sgu_vn_w[j], sgu_vn_b[j],
                     sgu_w_s[j], sgu_b_s[j].T, sgu_w_out[j].astype(BF16), sgu_b_out[j])
        x = _ffn(x, norm_ffn[i], sc2, sh2, g2, ffn_w_gate[i].astype(BF16), ffn_w_up[i].astype(BF16), ffn_conv_w[i],
                 ffn_conv_b[i], ffn_w_down[i].astype(BF16), final_norm, final=(i == DEPTH - 1))
    return x
```

```python
import functools

import numpy as np
import jax
import jax.numpy as jnp
from jax import lax
from jax.experimental import pallas as pl
from jax.experimental.pallas import tpu as pltpu

F32 = jnp.float32
BF16 = jnp.bfloat16
HI = lax.Precision.HIGHEST

D_MODEL = 1024
DEPTH = 2
HEAD_DIM = 64
NSA_HEADS = 8
NSA_KV_HEADS = 2
NSA_HPG = NSA_HEADS // NSA_KV_HEADS
CMP_STRIDE = 16
CMP_LEN = 2 * CMP_STRIDE
CMP_HIDDEN = 128
SLC_BLOCK = 64
SLC_TOPK = 16
WINDOW = 512
Q_BLOCK = 128
FORCE = 1e4
RWKV_HEADS = 8
RWKV_DIM = RWKV_HEADS * HEAD_DIM
DECAY_LORA = 64
AAA_LORA = 64
GATE_LORA = 160
DECAY_SCALE = 0.6065306597126334
RWKV_GN_EPS = 64e-5
SGU_CHUNK = 128
SGU_GROUPS = 8
SGU_DIM = 2048
FFN_DIM = 2816
CONV_WIDTH = 3
NORM_EPS = 1e-6
LN_EPS = 1e-5
NEG_INF = -1e30
LOG2E = 1.4426950408889634
NSA_Q_DIM = NSA_HEADS * HEAD_DIM
NSA_KV_DIM = NSA_KV_HEADS * HEAD_DIM
NSA_GATE_DIM = NSA_HEADS * 3
NSA_COLS = NSA_Q_DIM + 6 * NSA_KV_DIM + NSA_GATE_DIM

LANES = 128
SUBLANES = 8
VMEM_LIMIT_BYTES = 52 * 1024 * 1024

ROW_TILE = 512
COL_CHUNK = 512
NSA_COLS_PAD = 1408
RWKV_LORA_PAD = 128
RWKV_GATE_PAD = 256
RWKV_COLS_PAD = 3 * RWKV_DIM + 2 * RWKV_LORA_PAD + RWKV_GATE_PAD
SLC_KEY_TILE = 256
NS_PAD = LANES
POS_FEATS = 16
RWKV_CHUNK = 64
RWKV_SUB = 16
RWKV_PACK = 4
RWKV_STEP_ROWS = 256
RWKV_BATCH_PER_STEP = 2
FFN_COL_CHUNK = 256
FFN_HALO_ROWS = 16


def _cparams(*sem):
    return pltpu.CompilerParams(dimension_semantics=sem, vmem_limit_bytes=VMEM_LIMIT_BYTES)


def _resident(shape):
    return pl.BlockSpec(shape, lambda *_: (0,) * len(shape), pipeline_mode=pl.Buffered(1))


def _gelu(x):
    return 0.5 * x * (1.0 + jnp.tanh(0.7978845608028654 * (x + 0.044715 * (x * x * x))))


def _rms(x, w):
    return x * lax.rsqrt(jnp.mean(x * x, axis=-1, keepdims=True) + NORM_EPS) * w


def _prev_rows(cur, halo, shift):
    tm = cur.shape[0]
    ext = jnp.concatenate([halo, cur], axis=0)
    return ext[SUBLANES - shift:SUBLANES - shift + tm]


def _split3(x):
    hi = x.astype(BF16)
    r1 = x - hi.astype(F32)
    mid = r1.astype(BF16)
    return hi, mid, (r1 - mid.astype(F32)).astype(BF16)


def _with_pos_features(k, pos):
    lane = lax.broadcasted_iota(jnp.int32, (k.shape[0], POS_FEATS), 1)
    feats = jnp.where(lane < 3, pos // SLC_BLOCK, jnp.where(lane < 6, pos % SLC_BLOCK, 0))
    return jnp.concatenate([k, feats.astype(F32)], axis=1).astype(BF16)


def _dot_bf16x3(x, w):
    x_hi, w_hi = x.astype(BF16), w.astype(BF16)
    x_lo, w_lo = (x - x_hi.astype(F32)).astype(BF16), (w - w_hi.astype(F32)).astype(BF16)
    m = x.shape[0]
    t = jnp.dot(jnp.concatenate([x_hi, x_lo], axis=0), w_hi, preferred_element_type=F32)
    return t[:m] + t[m:] + jnp.dot(x_hi, w_lo, preferred_element_type=F32)


def _dot_exact_rhs(x, m):
    return jnp.dot(jnp.concatenate(_split3(x), axis=1), jnp.concatenate([m, m, m], axis=0),
                   preferred_element_type=F32)


def _dot_exact_lhs(m, x):
    return jnp.dot(jnp.concatenate([m, m, m], axis=1), jnp.concatenate(_split3(x), axis=0),
                   preferred_element_type=F32)


def _ada_kernel(c_ref, w_ref, b_ref, o_ref):
    c = c_ref[...]
    cond = c * jax.nn.sigmoid(c)
    o_ref[0] = jnp.dot(cond, w_ref[0], precision=HI, preferred_element_type=F32) + b_ref[0]


def _ada_mod(c, ada_w, ada_b):
    B, D = c.shape
    depth, _, N = ada_w.shape
    rows = -(-B // SUBLANES) * SUBLANES
    cp = jnp.zeros((rows, D), F32).at[:B].set(c)
    tn = N // 4
    out = pl.pallas_call(
        _ada_kernel,
        grid=(depth, N // tn),
        in_specs=[pl.BlockSpec((rows, D), lambda i, j: (0, 0)),
                  pl.BlockSpec((1, D, tn), lambda i, j: (i, 0, j)),
                  pl.BlockSpec((1, 1, tn), lambda i, j: (i, 0, j))],
        out_specs=pl.BlockSpec((1, rows, tn), lambda i, j: (i, 0, j)),
        out_shape=jax.ShapeDtypeStruct((depth, rows, N), F32),
        compiler_params=_cparams("parallel", "parallel"),
        name="ada_mod",
    )(cp, ada_w, ada_b.reshape(depth, 1, N))
    return out[:, :B]


def _in_proj_kernel(x_ref, nw_ref, sc_ref, sh_ref, w_ref,
                    qt_ref, zg_ref, kcs_ref, vcs_ref, ks_ref, vst_ref, kw_ref, vwt_ref, zr_ref):
    G, HP, dk, Q = NSA_KV_HEADS, NSA_HPG, HEAD_DIM, Q_BLOCK
    h = (_rms(x_ref[0], nw_ref[...]) * (1.0 + sc_ref[0]) + sh_ref[0]).astype(BF16)
    tm = h.shape[0]
    zq = jnp.dot(h, w_ref[:, 0:NSA_Q_DIM], preferred_element_type=F32) * (dk ** -0.5 * LOG2E)
    for nq in range(tm // Q):
        blk = zq[nq * Q:(nq + 1) * Q]
        for g in range(G):
            heads = [jnp.transpose(blk[:, (g * HP + hh) * dk:(g * HP + hh + 1) * dk]) for hh in range(HP)]
            qt_ref[0, g, nq] = jnp.concatenate(heads, axis=1).astype(BF16)
    kv_w = NSA_COLS_PAD - NSA_Q_DIM
    zkv = jnp.dot(h, w_ref[:, NSA_Q_DIM:NSA_COLS_PAD], preferred_element_type=F32)
    sec = lambda i, g: zkv[:, i * NSA_KV_DIM + g * dk:i * NSA_KV_DIM + (g + 1) * dk]
    pos = pl.program_id(1) * tm + lax.broadcasted_iota(jnp.int32, (tm, 1), 0)
    for g in range(G):
        kcs_ref[0, g] = sec(0, g)
        vcs_ref[0, g] = sec(1, g)
        ks_ref[0, g] = _with_pos_features(sec(2, g), pos)
        vst_ref[0, g] = jnp.transpose(sec(3, g)).astype(BF16)
        kw_ref[0, g] = _with_pos_features(sec(4, g), pos)
        vwt_ref[0, g] = jnp.transpose(sec(5, g)).astype(BF16)
    zg_ref[0] = zkv[:, 6 * NSA_KV_DIM:kv_w]
    n = zr_ref.shape[2]
    for c0 in range(0, n, COL_CHUNK):
        zr_ref[0, :, c0:c0 + COL_CHUNK] = jnp.dot(h, w_ref[:, NSA_COLS_PAD + c0:NSA_COLS_PAD + c0 + COL_CHUNK],
                                                  preferred_element_type=F32)


def _in_proj(x, nw, sc, sh, w):
    B, T, D = x.shape
    G, HP, dk, Q = NSA_KV_HEADS, NSA_HPG, HEAD_DIM, Q_BLOCK
    N = w.shape[1]
    tm = min(ROW_TILE, T)
    nq = tm // Q
    row = lambda n: pl.BlockSpec((1, tm, n), lambda b, i: (b, i, 0))
    per_b = pl.BlockSpec((1, 1, D), lambda b, i: (b, 0, 0))
    nat = pl.BlockSpec((1, G, tm, dk), lambda b, i: (b, 0, i, 0))
    tr = pl.BlockSpec((1, G, dk, tm), lambda b, i: (b, 0, 0, i))
    nat_s = jax.ShapeDtypeStruct((B, G, T, dk), F32)
    key = pl.BlockSpec((1, G, tm, dk + POS_FEATS), lambda b, i: (b, 0, i, 0))
    key_s = jax.ShapeDtypeStruct((B, G, T, dk + POS_FEATS), BF16)
    tr_s = jax.ShapeDtypeStruct((B, G, dk, T), BF16)
    return pl.pallas_call(
        _in_proj_kernel,
        grid=(B, T // tm),
        in_specs=[row(D), _resident((1, D)), per_b, per_b, _resident((D, N))],
        out_specs=[pl.BlockSpec((1, G, nq, dk, HP * Q), lambda b, i: (b, 0, i, 0, 0)),
                   row(NSA_COLS_PAD - NSA_Q_DIM - 6 * NSA_KV_DIM), nat, nat, key, tr, key, tr, row(RWKV_COLS_PAD)],
        out_shape=[jax.ShapeDtypeStruct((B, G, T // Q, dk, HP * Q), BF16),
                   jax.ShapeDtypeStruct((B, T, NSA_COLS_PAD - NSA_Q_DIM - 6 * NSA_KV_DIM), F32),
                   nat_s, nat_s, key_s, tr_s, key_s, tr_s,
                   jax.ShapeDtypeStruct((B, T, RWKV_COLS_PAD), F32)],
        compiler_params=_cparams("parallel", "parallel"),
        name="in_proj",
    )(x, nw.reshape(1, D), sc, sh, w)


def _nsa_compress_kernel(xk_ref, xv_ref, pe_ref, w1_ref, w2_ref, kc_ref, vct_ref):
    half = CMP_STRIDE * HEAD_DIM

    def mlp(x, s):
        pe = pe_ref[s]
        xa = (x + pe[:, :half]).astype(BF16)
        xb = (x + pe[:, half:]).astype(BF16)
        w1 = w1_ref[s]
        p = jnp.dot(xa, w1[:half], preferred_element_type=F32)
        q = jnp.dot(xb, w1[half:], preferred_element_type=F32)
        hid = _gelu(p + pltpu.roll(q, x.shape[0] - 1, 0))
        return jnp.dot(hid.astype(BF16), w2_ref[s], preferred_element_type=F32)

    nc = xk_ref.shape[1]
    cmp_end = lax.broadcasted_iota(jnp.int32, (nc, 1), 0) * CMP_STRIDE + (CMP_LEN - 1)
    kc_ref[0] = _with_pos_features(mlp(xk_ref[0], 0), cmp_end)
    vct_ref[0] = jnp.transpose(mlp(xv_ref[0], 1).astype(BF16))


def _nsa_compress(xk, xv, pe, w1, w2):
    BG, NC, W = xk.shape
    src = pl.BlockSpec((1, NC, W), lambda b: (b, 0, 0))
    return pl.pallas_call(
        _nsa_compress_kernel,
        grid=(BG,),
        in_specs=[src, src, _resident((2, 1, 2 * W)), _resident((2, 2 * W, CMP_HIDDEN)),
                  _resident((2, CMP_HIDDEN, HEAD_DIM))],
        out_specs=[pl.BlockSpec((1, NC, HEAD_DIM + POS_FEATS), lambda b: (b, 0, 0)),
                   pl.BlockSpec((1, HEAD_DIM, NC), lambda b: (b, 0, 0))],
        out_shape=[jax.ShapeDtypeStruct((BG, NC, HEAD_DIM + POS_FEATS), BF16),
                   jax.ShapeDtypeStruct((BG, HEAD_DIM, NC), BF16)],
        compiler_params=_cparams("parallel"),
        name="nsa_compress",
    )(xk, xv, pe, w1, w2)


def _nsa_attn_kernel(qt_ref, glt_ref, kc_ref, vct_ref, ks_ref, vst_ref, kw_ref, vwt_ref, aggt_ref,
                     o_ref, selt_ref, sa_ref, sb_ref, pa_ref, pb_ref, *, n_slc, top_k):
    g = pl.program_id(1)
    n = pl.program_id(2)
    Q, HP, R = Q_BLOCK, NSA_HPG, NSA_HPG * Q_BLOCK
    t0 = pl.multiple_of(n * Q, Q)
    qt = qt_ref[0, 0, 0]

    def per_head(a):
        return jnp.concatenate([a] * HP, axis=1)

    slope_g = jnp.float32(1.0)
    for gi in range(1, NSA_KV_HEADS):
        slope_g = jnp.where(g == gi, jnp.float32(2.0 ** (-HP * gi)), slope_g)
    lane_h = lax.broadcasted_iota(jnp.int32, (1, R), 1) // Q
    slope = jnp.full((1, R), LOG2E * 2.0 ** -HP, F32)
    for h in range(HP - 1):
        slope = jnp.where(lane_h == h, jnp.float32(LOG2E * 2.0 ** -(h + 1)), slope)
    slope = slope * slope_g
    tq = t0 + lax.broadcasted_iota(jnp.int32, (1, Q), 1)
    s1 = slope.astype(BF16)
    s2 = (slope - s1.astype(F32)).astype(BF16)
    s3 = (slope - s1.astype(F32) - s2.astype(F32)).astype(BF16)
    pieces = [p_.astype(F32) for p_ in (s1, s2, s3)]
    srows = [p_ * float(SLC_BLOCK) for p_ in pieces] + pieces + [jnp.zeros((POS_FEATS - 6, R), F32)]
    qt = jnp.concatenate([qt, jnp.concatenate(srows, axis=0).astype(BF16)], axis=0)

    ncp = kc_ref.shape[2]
    s = jnp.dot(kc_ref[0, 0], qt, preferred_element_type=F32)
    cmp_end = lax.broadcasted_iota(jnp.int32, (ncp, 1), 0) * CMP_STRIDE + (CMP_LEN - 1)
    dist = tq - cmp_end
    bias = per_head(jnp.where(dist >= 0, 0.0, NEG_INF))
    valid = bias > 0.5 * NEG_INF
    s = s + bias
    e = jnp.where(valid, jnp.exp2(s - jnp.max(s, axis=0, keepdims=True)), 0.0)
    l = jnp.sum(e, axis=0, keepdims=True)
    p = e * (1.0 / jnp.where(l > 0.0, l, 1.0))
    o_c = jnp.dot(vct_ref[0, 0], p.astype(BF16), preferred_element_type=F32)

    psum = p[:, 0:Q]
    for h in range(1, HP):
        psum = psum + p[:, h * Q:(h + 1) * Q]
    p_hi = psum.astype(BF16)
    p_lo = (psum - p_hi.astype(F32)).astype(BF16)
    aggt = aggt_ref[...]
    imp = jnp.dot(jnp.concatenate([aggt, aggt], axis=1), jnp.concatenate([p_hi, p_lo], axis=0),
                  preferred_element_type=F32)
    blk = lax.broadcasted_iota(jnp.int32, (NS_PAD, 1), 0)
    jt = tq // SLC_BLOCK
    forced = (blk == 0) | (blk == jt) | (blk == jt - 1)
    imp = jnp.where(blk > jt, -FORCE, jnp.where(forced, FORCE, imp))
    sel = jnp.where(forced & (blk <= jt), 1.0, 0.0)
    vals = jnp.where((blk < n_slc) & (sel < 0.5), imp, -jnp.inf)
    blk_f = blk.astype(F32)
    for _ in range(max(top_k - 3, 0)):
        vmax = jnp.max(vals, axis=0, keepdims=True)
        first = jnp.min(jnp.where(vals == vmax, blk_f, float(NS_PAD)), axis=0, keepdims=True)
        pick = blk_f == first
        sel = jnp.where(pick, 1.0, sel)
        vals = jnp.where(pick, -jnp.inf, vals)
    selt_ref[...] = sel

    WK = WINDOW + Q
    w0 = pl.multiple_of(jnp.maximum(t0 - WINDOW, 0), Q)
    sw = jnp.dot(kw_ref[0, 0, pl.ds(w0, WK), :], qt, preferred_element_type=F32)
    dw = tq - (w0 + lax.broadcasted_iota(jnp.int32, (WK, 1), 0))
    okw = (dw >= 0) & (dw < WINDOW)
    sw = sw + per_head(jnp.where(okw, 0.0, NEG_INF))
    ew = jnp.exp2(sw - jnp.max(sw, axis=0, keepdims=True))
    lw = jnp.sum(ew, axis=0, keepdims=True)
    o_w = jnp.dot(vwt_ref[0, 0, :, pl.ds(w0, WK)], ew.astype(BF16), preferred_element_type=F32) * (1.0 / lw)

    TK = SLC_KEY_TILE
    nb = TK // SLC_BLOCK
    n_tiles = (t0 + Q + TK - 1) // TK
    last_tile = ks_ref.shape[2] // TK - 1
    key_in_blk = lax.broadcasted_iota(jnp.int32, (SLC_BLOCK, 1), 0)

    def qk(kt, s_ref):
        k0 = pl.multiple_of(jnp.minimum(kt, last_tile) * TK, TK)
        s_ref[...] = jnp.dot(ks_ref[0, 0, pl.ds(k0, TK), :], qt, preferred_element_type=F32)

    def pv(kt, p_ref):
        k0 = pl.multiple_of(jnp.clip(kt, 0, last_tile) * TK, TK)
        return jnp.dot(vst_ref[0, 0, :, pl.ds(k0, TK)], p_ref[...], preferred_element_type=F32)

    def softmax_tile(kt, s_ref, p_ref, m_i, l_i):
        dms = []
        for j in range(nb):
            rel = (kt * TK + j * SLC_BLOCK + key_in_blk) - tq
            chosen = selt_ref[pl.ds(jnp.minimum(kt * nb + j, NS_PAD - 1), 1), :] > 0.5
            dms.append(jnp.where(chosen & (rel <= 0), 0.0, NEG_INF))
        sc = s_ref[...] + per_head(jnp.concatenate(dms, axis=0))
        m_new = jnp.maximum(m_i, jnp.max(sc, axis=0, keepdims=True))
        alpha = jnp.exp2(m_i - m_new)
        pe_ = jnp.exp2(sc - m_new)
        p_ref[...] = pe_.astype(BF16)
        return m_new, alpha, alpha * l_i + jnp.sum(pe_, axis=0, keepdims=True)

    def slc_pair(j, carry):
        m_i, l_i, acc = carry
        ka = 2 * j
        qk(ka + 1, sb_ref)
        acc = acc + pv(ka - 1, pb_ref)
        m_i, alpha, l_i = softmax_tile(ka, sa_ref, pa_ref, m_i, l_i)
        acc = alpha * acc
        qk(ka + 2, sa_ref)
        acc = acc + pv(ka, pa_ref)
        m_i, alpha, l_i = softmax_tile(ka + 1, sb_ref, pb_ref, m_i, l_i)
        return m_i, l_i, alpha * acc

    pb_ref[...] = jnp.zeros_like(pb_ref)
    qk(0, sa_ref)
    n_pairs = (n_tiles + 1) // 2
    init = (jnp.full((1, R), NEG_INF, F32), jnp.zeros((1, R), F32), jnp.zeros((HEAD_DIM, R), F32))
    _, l_s, acc_s = lax.fori_loop(0, n_pairs, slc_pair, init)
    acc_s = acc_s + pv(2 * n_pairs - 1, pb_ref)

    gates = jax.nn.sigmoid(glt_ref[0, 0, 0])
    o_ref[0, 0, 0] = gates[0:1] * o_c + gates[1:2] * (acc_s * (1.0 / l_s)) + gates[2:3] * o_w


def _nsa_attention(qt, glt, kc, vct, ks, vst, kw, vwt, aggt, *, n_slc, top_k):
    B, G, NQ, dk, R = qt.shape
    ncp = kc.shape[2]
    T = ks.shape[2]
    assert T >= WINDOW + Q_BLOCK
    per_bg = lambda b, g, n: (b, g, 0, 0)
    per_q = lambda rows: pl.BlockSpec((1, 1, 1, rows, R), lambda b, g, n: (b, g, n, 0, 0))
    return pl.pallas_call(
        functools.partial(_nsa_attn_kernel, n_slc=n_slc, top_k=top_k),
        grid=(B, G, NQ),
        in_specs=[per_q(dk), per_q(3),
                  pl.BlockSpec((1, 1, ncp, dk + POS_FEATS), per_bg),
                  pl.BlockSpec((1, 1, dk, ncp), per_bg),
                  pl.BlockSpec((1, 1, T, dk + POS_FEATS), per_bg),
                  pl.BlockSpec((1, 1, dk, T), per_bg),
                  pl.BlockSpec((1, 1, T, dk + POS_FEATS), per_bg),
                  pl.BlockSpec((1, 1, dk, T), per_bg),
                  _resident((NS_PAD, ncp))],
        out_specs=per_q(dk),
        out_shape=jax.ShapeDtypeStruct((B, G, NQ, dk, R), F32),
        scratch_shapes=[pltpu.VMEM((NS_PAD, Q_BLOCK), F32),
                        pltpu.VMEM((SLC_KEY_TILE, R), F32), pltpu.VMEM((SLC_KEY_TILE, R), F32),
                        pltpu.VMEM((SLC_KEY_TILE, R), BF16), pltpu.VMEM((SLC_KEY_TILE, R), BF16)],
        compiler_params=_cparams("parallel", "parallel", "arbitrary"),
        name="nsa_attention",
    )(qt, glt, kc, vct, ks, vst, kw, vwt, aggt)


def _rwkv_pre_kernel(z_ref, halo_ref, mu_ref, w0_ref, a0_ref, kk_ref, ka_ref, rk_ref, wup_ref, aup_ref, gup_ref,
                     ones_ref, csum_ref,
                     kap_ref, r_ref, v_ref, kend_ref, bend_ref, bt_ref, kt_ref, gend_ref, bonus_ref, g_ref):
    i = pl.program_id(1)
    z = z_ref[0]
    tm = z.shape[0]
    halo = jnp.where(i > 0, halo_ref[0], 0.0)
    zs = z + (_prev_rows(z, halo, 1) - z) * mu_ref[...]
    D = RWKV_DIM
    r, k, v = zs[:, 0:D], zs[:, D:2 * D], zs[:, 2 * D:3 * D]
    o1 = 3 * D
    o2 = o1 + RWKV_LORA_PAD
    o3 = o2 + RWKV_LORA_PAD
    wl, al, gl = zs[:, o1:o2], zs[:, o2:o3], zs[:, o3:]
    lw = -DECAY_SCALE * jax.nn.sigmoid(w0_ref[...] + _dot_bf16x3(jnp.tanh(wl), wup_ref[...]))
    a = jax.nn.sigmoid(a0_ref[...] + _dot_bf16x3(al, aup_ref[...]))
    kt = k * (1.0 + (a - 1.0) * ka_ref[...])
    kk = k * kk_ref[...]
    head_sums = _dot_exact_rhs(jnp.concatenate([kk * kk, r * kt * rk_ref[...]], axis=0), ones_ref[...])
    kk = kk / jnp.maximum(jnp.sqrt(head_sums[:tm]), 1e-12)
    b = kk * a
    C = RWKV_CHUNK
    cum = _dot_exact_lhs(csum_ref[...], lw)
    tot = jnp.concatenate([jnp.broadcast_to(cum[c * C + C - 1:c * C + C], (C, cum.shape[1]))
                           for c in range(tm // C)], axis=0)
    g_inv = jnp.exp(-cum)
    tail = jnp.exp(tot - cum)
    kap_ref[0] = kk * jnp.exp(cum - lw)
    r_ref[0] = r * jnp.exp(cum)
    v_ref[0] = v
    kend_ref[0] = kt * tail
    bend_ref[0] = b * tail
    b_h, k_h, g_end = b * g_inv, kt * g_inv, jnp.exp(tot)
    for c in range(tm // C):
        bt_ref[0, c] = jnp.transpose(b_h[c * C:(c + 1) * C])
        kt_ref[0, c] = jnp.transpose(k_h[c * C:(c + 1) * C])
    gend_ref[0] = jnp.concatenate([g_end[c * C:c * C + 1] for c in range(tm // C)], axis=0)
    bonus_ref[0] = head_sums[tm:] * v
    g_ref[0] = _dot_bf16x3(jax.nn.sigmoid(gl), gup_ref[...])


def _rwkv_pre(z, mu, w0, a0, k_k, k_a, r_k, w_up, a_up, g_up):
    B, T, W = z.shape
    D = RWKV_DIM
    tm = min(ROW_TILE, T)
    hb = tm // SUBLANES
    C = RWKV_CHUNK
    ones_bd = jnp.asarray(np.kron(np.eye(RWKV_HEADS), np.ones((HEAD_DIM, HEAD_DIM))), BF16)
    csum = jnp.asarray(np.kron(np.eye(tm // C), np.tril(np.ones((C, C)))), BF16)
    out = jax.ShapeDtypeStruct((B, T, D), F32)
    row = pl.BlockSpec((1, tm, D), lambda b, i: (b, i, 0))
    tr = pl.BlockSpec((1, tm // C, D, C), lambda b, i: (b, i, 0, 0))
    tr_s = jax.ShapeDtypeStruct((B, T // C, D, C), F32)
    return pl.pallas_call(
        _rwkv_pre_kernel,
        grid=(B, T // tm),
        in_specs=[pl.BlockSpec((1, tm, W), lambda b, i: (b, i, 0)),
                  pl.BlockSpec((1, SUBLANES, W), lambda b, i: (b, jnp.maximum(i * hb - 1, 0), 0)),
                  _resident((1, W)), _resident((1, D)), _resident((1, D)), _resident((1, D)), _resident((1, D)),
                  _resident((1, D)), _resident((RWKV_LORA_PAD, D)), _resident((RWKV_LORA_PAD, D)),
                  _resident((RWKV_GATE_PAD, D)), _resident((D, D)), _resident((tm, tm))],
        out_specs=[row] * 5 + [tr, tr, pl.BlockSpec((1, tm // C, D), lambda b, i: (b, i, 0)), row, row],
        out_shape=[out] * 5 + [tr_s, tr_s, jax.ShapeDtypeStruct((B, T // C, D), F32), out, out],
        compiler_params=_cparams("parallel", "parallel"),
        name="rwkv_pre",
    )(z, z, mu.reshape(1, W), w0.reshape(1, D), a0.reshape(1, D), k_k.reshape(1, D), k_a.reshape(1, D),
      r_k.reshape(1, D), w_up, a_up, g_up, ones_bd, csum)


def _rwkv_core_kernel(kap_ref, r_ref, v_ref, kend_ref, bend_ref, bt_ref, kt_ref, gend_ref, o_ref, h_ref):
    C, N, P = RWKV_CHUNK, HEAD_DIM, RWKV_PACK
    W = P * N
    chains = [(bi, slice(p * W, (p + 1) * W)) for bi in range(kap_ref.shape[0]) for p in range(kap_ref.shape[2] // W)]

    @pl.when(pl.program_id(1) == 0)
    def _():
        h_ref[...] = jnp.zeros_like(h_ref)

    ri = lax.broadcasted_iota(jnp.int32, (C, W), 0)
    cj = lax.broadcasted_iota(jnp.int32, (C, W), 1) % N
    tril = (ri >= cj).astype(F32)
    stril = (ri > cj).astype(F32)
    eye = (ri == cj).astype(F32)
    diag_blk = ((ri // RWKV_SUB) == (cj // RWKV_SUB)).astype(F32)
    same_head = (lax.broadcasted_iota(jnp.int32, (W, W), 0) // N) == (lax.broadcasted_iota(jnp.int32, (W, W), 1) // N)

    def split(x):
        hi = x.astype(BF16)
        return hi, (x - hi.astype(F32)).astype(BF16)

    def lhs2(x):
        hi, lo = split(x)
        return jnp.concatenate([hi, lo], axis=0), hi

    def bd(x):
        blk = lambda a: jnp.where(same_head, jnp.concatenate([a] * P, axis=0), jnp.zeros((), BF16))
        hi, lo = split(x)
        return blk(hi), blk(lo)

    def bd_t(xt):
        return split(jnp.where(same_head, jnp.concatenate([xt] * P, axis=1), 0.0))

    def mmw(l2, w):
        (x2, x_hi), (w_hi, w_lo) = l2, w
        m = x_hi.shape[0]
        t = jnp.dot(x2, w_hi, preferred_element_type=F32)
        return t[:m] + t[m:] + jnp.dot(x_hi, w_lo, preferred_element_type=F32)

    def mmp(x, y):
        return mmw(lhs2(x), bd(y))

    def gm(f, *cols):
        return [f(*args) for args in zip(*cols)]

    def chunk(c, hs):
        rows = pl.ds(pl.multiple_of(c * C, C), C)
        v = [v_ref[bi, rows, ln] for bi, ln in chains]
        lhs = [lhs2(jnp.concatenate([kap_ref[bi, rows, ln], r_ref[bi, rows, ln]], axis=0)) for bi, ln in chains]
        ab = gm(lambda l, ch: mmw(l, bd_t(bt_ref[ch[0], c, ch[1], :])), lhs, chains)
        ak = gm(lambda l, ch: mmw(l, bd_t(kt_ref[ch[0], c, ch[1], :])), lhs, chains)
        a_kb = [stril * x[:C] for x in ab]
        a_rb = [tril * x[C:] for x in ab]
        a_kr = [jnp.concatenate([stril * x[:C], tril * x[C:]], axis=0) for x in ak]
        d = [x * diag_blk for x in a_kb]
        e = gm(lambda x, y: x - y, a_kb, d)
        d_inv = [eye - x for x in d]
        pw = gm(mmp, d, d)
        steps = int(np.log2(RWKV_SUB)) - 1
        for s_ in range(steps):
            d_inv = gm(lambda x, y: mmp(x, eye + y), d_inv, pw)
            if s_ + 1 < steps:
                pw = gm(mmp, pw, pw)
        nb = gm(mmp, d_inv, e)
        t_inv = [eye - x for x in nb]
        pw = gm(mmp, nb, nb)
        bsteps = int(np.log2(C // RWKV_SUB)) - 1
        for s_ in range(bsteps):
            t_inv = gm(lambda x, y: mmp(x, eye + y), t_inv, pw)
            if s_ + 1 < bsteps:
                pw = gm(mmp, pw, pw)
        t_inv = gm(mmp, t_inv, d_inv)
        sh = gm(lambda l, h: mmw(l, bd(h)), lhs, hs)
        av = gm(mmp, a_kr, v)
        u = gm(lambda t, x, y: mmp(t, x[:C] + y[:C]), t_inv, sh, av)
        y = gm(lambda x, z, a, uu: x[C:] + z[C:] - mmp(a, uu), sh, av, a_rb, u)
        for (bi, ln), yy in zip(chains, y):
            o_ref[bi, rows, ln] = yy
        zero = jnp.zeros((C, W), F32)
        left = [jnp.concatenate([eye * gend_ref[bi, 0, pl.ds(c, 1), ln], kend_ref[bi, rows, ln],
                                 bend_ref[bi, rows, ln], zero], axis=0) for bi, ln in chains]
        right = gm(lambda h, vv, uu: split(jnp.concatenate([h, vv, -uu, zero], axis=0)), hs, v, u)
        full = gm(lambda l, rr: jnp.where(same_head, mmw(lhs2(jnp.transpose(l)), rr), 0.0), left, right)
        return tuple(sum(f[s_ * N:(s_ + 1) * N] for s_ in range(1, P)) + f[0:N] for f in full)

    hs = lax.fori_loop(0, kap_ref.shape[1] // C, chunk, tuple(h_ref[p] for p in range(len(chains))))
    for p in range(len(chains)):
        h_ref[p] = hs[p]


def _rwkv_core(kap, r, v, kend, bend, bt, kt, gend):
    B, T, D = kap.shape
    W = RWKV_PACK * HEAD_DIM
    assert D % W == 0 and W == 4 * RWKV_CHUNK
    bb = RWKV_BATCH_PER_STEP if B % RWKV_BATCH_PER_STEP == 0 else 1
    tc = min(RWKV_STEP_ROWS, T)
    nch = tc // RWKV_CHUNK
    seq = pl.BlockSpec((bb, tc, D), lambda b, i: (b, i, 0))
    tr = pl.BlockSpec((bb, nch, D, RWKV_CHUNK), lambda b, i: (b, i, 0, 0))
    return pl.pallas_call(
        _rwkv_core_kernel,
        grid=(B // bb, T // tc),
        in_specs=[seq] * 5 + [tr] * 2 + [pl.BlockSpec((bb, 1, nch, D), lambda b, i: (b, i, 0, 0))],
        out_specs=seq,
        out_shape=jax.ShapeDtypeStruct((B, T, D), F32),
        scratch_shapes=[pltpu.VMEM((bb * (D // W), HEAD_DIM, W), F32)],
        compiler_params=_cparams("parallel", "arbitrary"),
        name="rwkv_core",
    )(kap, r, v, kend, bend, bt, kt, gend.reshape(B, T // tc, nch, D))


def _out_proj_kernel(ot_ref, y_ref, bonus_ref, gg_ref, gw_ref, gb_ref, ones_ref, wa_ref, wb_ref, x_ref, g_ref,
                     o_ref):
    G, HP, Q = NSA_KV_HEADS, NSA_HPG, Q_BLOCK
    blocks = []
    for nq in range(ot_ref.shape[2]):
        heads = [jnp.transpose(ot_ref[0, g, nq][:, hh * Q:(hh + 1) * Q]) for g in range(G) for hh in range(HP)]
        blocks.append(jnp.concatenate(heads, axis=1))
    ya = jnp.concatenate(blocks, axis=0).astype(BF16)
    y = y_ref[0]
    inv_n = 1.0 / HEAD_DIM
    dlt = y - _dot_exact_rhs(y, ones_ref[...]) * inv_n
    var = _dot_exact_rhs(dlt * dlt, ones_ref[...]) * inv_n
    yb = (dlt * lax.rsqrt(var + RWKV_GN_EPS) * gw_ref[...] + gb_ref[...] + bonus_ref[0]) * gg_ref[0]
    o = jnp.dot(ya, wa_ref[...], preferred_element_type=F32)
    o = o + jnp.dot(yb.astype(BF16), wb_ref[...], preferred_element_type=F32)
    o_ref[0] = x_ref[0] + g_ref[0] * o


def _out_proj(ot, y, bonus, gg, gn_w, gn_b, wa, wb, x, gate):
    B, T, D = x.shape
    G, NQ, dk, R = ot.shape[1:]
    tm = min(ROW_TILE, T)
    ka, kb = wa.shape[0], y.shape[-1]
    ones_bd = jnp.asarray(np.kron(np.eye(kb // HEAD_DIM), np.ones((HEAD_DIM, HEAD_DIM))), BF16)
    row = lambda w: pl.BlockSpec((1, tm, w), lambda b, i: (b, i, 0))
    return pl.pallas_call(
        _out_proj_kernel,
        grid=(B, T // tm),
        in_specs=[pl.BlockSpec((1, G, tm // Q_BLOCK, dk, R), lambda b, i: (b, 0, i, 0, 0)),
                  row(kb), row(kb), row(kb), _resident((1, kb)), _resident((1, kb)), _resident((kb, kb)),
                  _resident((ka, D)), _resident((kb, D)), row(D),
                  pl.BlockSpec((1, 1, D), lambda b, i: (b, 0, 0))],
        out_specs=row(D),
        out_shape=jax.ShapeDtypeStruct((B, T, D), F32),
        compiler_params=_cparams("parallel", "parallel"),
        name="out_proj",
    )(ot, y, bonus, gg, gn_w.reshape(1, kb), gn_b.reshape(1, kb), ones_bd, wa, wb, x, gate)


def _sgu_kernel(x_ref, nw_ref, sc_ref, sh_ref, g_ref, wi_ref, bi_ref, vnw_ref, vnb_ref, ws_ref, bst_ref,
                wo_ref, bo_ref, o_ref, u_ref, v_ref, gated_ref):
    x = x_ref[0]
    tm = x.shape[0]
    E = SGU_DIM
    h = (_rms(x, nw_ref[...]) * (1.0 + sc_ref[0]) + sh_ref[0]).astype(BF16)
    for c0 in range(0, 2 * E, COL_CHUNK):
        zc = _gelu(jnp.dot(h, wi_ref[:, c0:c0 + COL_CHUNK], preferred_element_type=F32) + bi_ref[:, c0:c0 + COL_CHUNK])
        if c0 < E:
            u_ref[:, c0:c0 + COL_CHUNK] = zc
        else:
            v_ref[:, c0 - E:c0 - E + COL_CHUNK] = zc
    v = v_ref[...]
    mu = jnp.mean(v, axis=-1, keepdims=True)
    var = jnp.mean(jnp.square(v - mu), axis=-1, keepdims=True)
    vn = ((v - mu) * lax.rsqrt(var + LN_EPS) * vnw_ref[...] + vnb_ref[...]).astype(BF16)
    S = SGU_CHUNK
    gw = E // SGU_GROUPS
    causal = lax.broadcasted_iota(jnp.int32, (S, S), 0) >= lax.broadcasted_iota(jnp.int32, (S, S), 1)
    for gi in range(SGU_GROUPS):
        ws = jnp.where(causal, ws_ref[gi], 0.0).astype(BF16)
        bs = bst_ref[:, gi:gi + 1]
        for n in range(tm // S):
            sv = jnp.dot(ws, vn[n * S:(n + 1) * S, gi * gw:(gi + 1) * gw], preferred_element_type=F32) + bs
            u = u_ref[n * S:(n + 1) * S, gi * gw:(gi + 1) * gw]
            gated_ref[n * S:(n + 1) * S, gi * gw:(gi + 1) * gw] = (u * sv).astype(BF16)
    y = jnp.dot(gated_ref[...], wo_ref[...], preferred_element_type=F32) + bo_ref[...]
    o_ref[0] = x + g_ref[0] * y


def _sgu(x, nw, sc, sh, gate, w_in, b_in, vn_w, vn_b, w_s, b_s_t, w_out, b_out):
    B, T, D = x.shape
    E = SGU_DIM
    tm = min(ROW_TILE, T)
    assert tm % SGU_CHUNK == 0
    row = pl.BlockSpec((1, tm, D), lambda b, i: (b, i, 0))
    per_b = pl.BlockSpec((1, 1, D), lambda b, i: (b, 0, 0))
    return pl.pallas_call(
        _sgu_kernel,
        grid=(B, T // tm),
        in_specs=[row, _resident((1, D)), per_b, per_b, per_b,
                  _resident((D, 2 * E)), _resident((1, 2 * E)), _resident((1, E)), _resident((1, E)),
                  _resident((SGU_GROUPS, SGU_CHUNK, SGU_CHUNK)), _resident((SGU_CHUNK, SGU_GROUPS)),
                  _resident((E, D)), _resident((1, D))],
        out_specs=row,
        out_shape=jax.ShapeDtypeStruct((B, T, D), F32),
        scratch_shapes=[pltpu.VMEM((tm, E), F32), pltpu.VMEM((tm, E), F32), pltpu.VMEM((tm, E), BF16)],
        compiler_params=_cparams("parallel", "parallel"),
        name="sgu",
    )(x, nw.reshape(1, D), sc, sh, gate, w_in, b_in.reshape(1, 2 * E), vn_w.reshape(1, E), vn_b.reshape(1, E),
      w_s, b_s_t, w_out, b_out.reshape(1, D))


def _ffn_kernel(x_ref, halo_ref, nw_ref, sc_ref, sh_ref, g_ref, wg_ref, wu_ref, cw_ref, cb_ref, wd_ref, fn_ref,
                o_ref, act_ref, *, final):
    i = pl.program_id(1)
    x = x_ref[0]
    tm = x.shape[0]
    F = wg_ref.shape[1]
    HR = FFN_HALO_ROWS
    hx = (_rms(jnp.concatenate([halo_ref[0], x], axis=0), nw_ref[...]) * (1.0 + sc_ref[0]) + sh_ref[0]).astype(BF16)
    h = hx[HR:]
    for c0 in range(0, F, FFN_COL_CHUNK):
        cols = slice(c0, c0 + FFN_COL_CHUNK)
        ext = jnp.dot(hx, wg_ref[:, cols], preferred_element_type=F32)
        ext = jnp.concatenate([jnp.where(i > 0, ext[:HR], 0.0), ext[HR:]], axis=0)
        cw = cw_ref[:, cols]
        a = (cw[0:1] * ext[HR - 2:HR - 2 + tm] + cw[1:2] * ext[HR - 1:HR - 1 + tm] + cw[2:3] * ext[HR:]
             + cb_ref[:, cols])
        up = jnp.dot(h, wu_ref[:, cols], preferred_element_type=F32)
        act_ref[:, cols] = (_gelu(a) * up).astype(BF16)
    acc = jnp.dot(act_ref[...], wd_ref[...], preferred_element_type=F32)
    xn = x + g_ref[0] * acc
    if final:
        xn = _rms(xn, fn_ref[...])
    o_ref[0] = xn


def _ffn(x, nw, sc, sh, gate, w_gate, w_up, conv_w, conv_b, w_down, final_w, *, final):
    B, T, D = x.shape
    F = w_down.shape[0]
    tm = min(ROW_TILE, T)
    hb = tm // FFN_HALO_ROWS
    row = pl.BlockSpec((1, tm, D), lambda b, i: (b, i, 0))
    per_b = pl.BlockSpec((1, 1, D), lambda b, i: (b, 0, 0))
    return pl.pallas_call(
        functools.partial(_ffn_kernel, final=final),
        grid=(B, T // tm),
        in_specs=[row,
                  pl.BlockSpec((1, FFN_HALO_ROWS, D), lambda b, i: (b, jnp.maximum(i * hb - 1, 0), 0)),
                  _resident((1, D)), per_b, per_b, per_b,
                  _resident((D, F)), _resident((D, F)), _resident((CONV_WIDTH, F)), _resident((1, F)),
                  _resident((F, D)), _resident((1, D))],
        out_specs=row,
        out_shape=jax.ShapeDtypeStruct((B, T, D), F32),
        scratch_shapes=[pltpu.VMEM((tm, F), BF16)],
        compiler_params=_cparams("parallel", "parallel"),
        name="ffn",
    )(x, x, nw.reshape(1, D), sc, sh, gate, w_gate, w_up, conv_w, conv_b.reshape(1, F), w_down,
      final_w.reshape(1, D))


def _pad_cols(w, sizes, padded):
    parts, o = [], 0
    for s, p in zip(sizes, padded):
        parts.append(jnp.pad(w[..., o:o + s], [(0, 0)] * (w.ndim - 1) + [(0, p - s)]))
        o += s
    return jnp.concatenate(parts, axis=-1)


def _pad_rows(w, rows):
    return jnp.pad(w, ((0, rows - w.shape[0]), (0, 0)))


def _nsa_agg_t(T):
    n_cmp_pad = T // CMP_STRIDE
    n_slc = T // SLC_BLOCK
    c = np.arange(n_cmp_pad)
    s = np.arange(NS_PAD)
    cs, ce, ss = c * CMP_STRIDE, c * CMP_STRIDE + CMP_LEN - 1, s * SLC_BLOCK
    agg = (cs[None, :] < ss[:, None] + SLC_BLOCK) & (ce[None, :] >= ss[:, None]) & (s[:, None] < n_slc)
    agg &= (c[None, :] < n_cmp_pad - 1)
    return jnp.asarray(agg, BF16)


def _nsa_rwkv_mixer(x, nw, sc, sh, gate, w_in, cmp_pe, cmp_w1, cmp_w2, mu, w0, w_up, a0, a_up, g_up,
                    k_k, k_a, r_k, gn_w, gn_b, w_out):
    B, T, D = x.shape
    G, HPG, dk = NSA_KV_HEADS, NSA_HPG, HEAD_DIM
    n_slc = T // SLC_BLOCK
    NQ = T // Q_BLOCK
    assert n_slc <= NS_PAD and T % (CMP_STRIDE * LANES) == 0 and T % (2 * SLC_KEY_TILE) == 0

    rw_sizes = [RWKV_DIM] * 3 + [DECAY_LORA, AAA_LORA, GATE_LORA]
    rw_pads = [RWKV_DIM] * 3 + [RWKV_LORA_PAD, RWKV_LORA_PAD, RWKV_GATE_PAD]
    w_all = jnp.concatenate([jnp.pad(w_in[:, :NSA_COLS], ((0, 0), (0, NSA_COLS_PAD - NSA_COLS))),
                             _pad_cols(w_in[:, NSA_COLS:], rw_sizes, rw_pads)], axis=1).astype(BF16)
    qt, zg, kcs, vcs, ks, vst, kw, vwt, z_rw = _in_proj(x, nw, sc, sh, w_all)

    glt = zg[..., :NSA_GATE_DIM].reshape(B, NQ, Q_BLOCK, G, HPG, 3).transpose(0, 3, 1, 5, 4, 2)
    glt = glt.reshape(B, G, NQ, 3, HPG * Q_BLOCK)
    strides = lambda a: a.reshape(B * G, T // CMP_STRIDE, CMP_STRIDE * dk)
    kc, vct = _nsa_compress(strides(kcs), strides(vcs), cmp_pe.reshape(2, 1, CMP_LEN * dk),
                            cmp_w1.astype(BF16), cmp_w2.astype(BF16))
    o_t = _nsa_attention(qt, glt, kc.reshape(B, G, T // CMP_STRIDE, -1), vct.reshape(B, G, dk, T // CMP_STRIDE),
                         ks, vst, kw, vwt, _nsa_agg_t(T), n_slc=n_slc, top_k=min(SLC_TOPK, n_slc))

    kap, r, v, kend, bend, bt, kt, gend, bonus, gg = _rwkv_pre(
        z_rw, _pad_cols(mu, rw_sizes, rw_pads), w0, a0, k_k, k_a, r_k,
        _pad_rows(w_up, RWKV_LORA_PAD), _pad_rows(a_up, RWKV_LORA_PAD), _pad_rows(g_up, RWKV_GATE_PAD))
    y_b = _rwkv_core(kap, r, v, kend, bend, bt, kt, gend)

    w_out = w_out.astype(BF16)
    return _out_proj(o_t, y_b, bonus, gg, gn_w, gn_b, w_out[:NSA_Q_DIM], w_out[NSA_Q_DIM:], x, gate)


def kernel(x, c, ada_w, ada_b, norm_mix, norm_ffn, ffn_w_gate, ffn_w_up, ffn_conv_w, ffn_conv_b, ffn_w_down, ab_w_in, nsa_cmp_pe, nsa_cmp_w1, nsa_cmp_w2, rwkv_mu, rwkv_w0, rwkv_w_up, rwkv_a0, rwkv_a_up, rwkv_g_up, rwkv_k_k, rwkv_k_a, rwkv_r_k, rwkv_gn_w, rwkv_gn_b, ab_w_out, sgu_w_in, sgu_b_in, sgu_vn_w, sgu_vn_b, sgu_w_s, sgu_b_s, sgu_w_out, sgu_b_out, final_norm):
    B, T, D = x.shape
    mod = _ada_mod(c, ada_w, ada_b)
    for i in range(DEPTH):
        sh1, sc1, g1, sh2, sc2, g2 = [m.reshape(B, 1, D) for m in jnp.split(mod[i], 6, axis=-1)]
        j = i // 2
        if i % 2 == 0:
            x = _nsa_rwkv_mixer(x, norm_mix[i], sc1, sh1, g1, ab_w_in[j], nsa_cmp_pe[j], nsa_cmp_w1[j],
                                nsa_cmp_w2[j], rwkv_mu[j], rwkv_w0[j], rwkv_w_up[j], rwkv_a0[j], rwkv_a_up[j],
                                rwkv_g_up[j], rwkv_k_k[j], rwkv_k_a[j], rwkv_r_k[j], rwkv_gn_w[j], rwkv_gn_b[j],
                                ab_w_out[j])
        else:
            x = _sgu(x, norm_mix[i], sc1, sh1, g1, sgu_w_in[j].astype(BF16), sgu_b_in[j], sgu_vn_w[j], sgu_vn_b[j],
                     sgu_w_s[j], sgu_b_s[j].T, sgu_w_out[j].astype(BF16), sgu_b_out[j])
        x = _ffn(x, norm_ffn[i], sc2, sh2, g2, ffn_w_gate[i].astype(BF16), ffn_w_up[i].astype(BF16), ffn_conv_w[i],
                 ffn_conv_b[i], ffn_w_down[i].astype(BF16), final_norm, final=(i == DEPTH - 1))
    return x
```

```python
import functools

import numpy as np
import jax
import jax.numpy as jnp
from jax import lax
from jax.experimental import pallas as pl
from jax.experimental.pallas import tpu as pltpu

F32 = jnp.float32
BF16 = jnp.bfloat16
HI = lax.Precision.HIGHEST

D_MODEL = 1024
DEPTH = 2
HEAD_DIM = 64
NSA_HEADS = 8
NSA_KV_HEADS = 2
NSA_HPG = NSA_HEADS // NSA_KV_HEADS
CMP_STRIDE = 16
CMP_LEN = 2 * CMP_STRIDE
CMP_HIDDEN = 128
SLC_BLOCK = 64
SLC_TOPK = 16
WINDOW = 512
Q_BLOCK = 128
FORCE = 1e4
RWKV_HEADS = 8
RWKV_DIM = RWKV_HEADS * HEAD_DIM
DECAY_LORA = 64
AAA_LORA = 64
GATE_LORA = 160
DECAY_SCALE = 0.6065306597126334
RWKV_GN_EPS = 64e-5
SGU_CHUNK = 128
SGU_GROUPS = 8
SGU_DIM = 2048
FFN_DIM = 2816
CONV_WIDTH = 3
NORM_EPS = 1e-6
LN_EPS = 1e-5
NEG_INF = -1e30
LOG2E = 1.4426950408889634
NSA_Q_DIM = NSA_HEADS * HEAD_DIM
NSA_KV_DIM = NSA_KV_HEADS * HEAD_DIM
NSA_GATE_DIM = NSA_HEADS * 3
NSA_COLS = NSA_Q_DIM + 6 * NSA_KV_DIM + NSA_GATE_DIM

LANES = 128
SUBLANES = 8
VMEM_LIMIT_BYTES = 52 * 1024 * 1024

ROW_TILE = 512
COL_CHUNK = 512
NSA_COLS_PAD = 1408
RWKV_LORA_PAD = 128
RWKV_GATE_PAD = 256
RWKV_COLS_PAD = 3 * RWKV_DIM + 2 * RWKV_LORA_PAD + RWKV_GATE_PAD
SLC_KEY_TILE = 256
NS_PAD = LANES
TILE_MASK_BITS = 16
POS_FEATS = 16
RWKV_CHUNK = 64
RWKV_SUB = 16
RWKV_PACK = 4
RWKV_STEP_ROWS = 256
RWKV_BATCH_PER_STEP = 2
FFN_COL_CHUNK = 256
FFN_HALO_ROWS = 16


def _cparams(*sem):
    return pltpu.CompilerParams(dimension_semantics=sem, vmem_limit_bytes=VMEM_LIMIT_BYTES)


def _resident(shape):
    return pl.BlockSpec(shape, lambda *_: (0,) * len(shape), pipeline_mode=pl.Buffered(1))


def _gelu(x):
    return 0.5 * x * (1.0 + jnp.tanh(0.7978845608028654 * (x + 0.044715 * (x * x * x))))


def _rms(x, w):
    return x * lax.rsqrt(jnp.mean(x * x, axis=-1, keepdims=True) + NORM_EPS) * w


def _prev_rows(cur, halo, shift):
    tm = cur.shape[0]
    ext = jnp.concatenate([halo, cur], axis=0)
    return ext[SUBLANES - shift:SUBLANES - shift + tm]


def _split3(x):
    hi = x.astype(BF16)
    r1 = x - hi.astype(F32)
    mid = r1.astype(BF16)
    return hi, mid, (r1 - mid.astype(F32)).astype(BF16)


def _with_pos_features(k, pos):
    lane = lax.broadcasted_iota(jnp.int32, (k.shape[0], POS_FEATS), 1)
    feats = jnp.where(lane < 3, pos // SLC_BLOCK, jnp.where(lane < 6, pos % SLC_BLOCK, 0))
    return jnp.concatenate([k, feats.astype(F32)], axis=1).astype(BF16)


def _dot_bf16x3(x, w):
    x_hi, w_hi = x.astype(BF16), w.astype(BF16)
    x_lo, w_lo = (x - x_hi.astype(F32)).astype(BF16), (w - w_hi.astype(F32)).astype(BF16)
    m = x.shape[0]
    t = jnp.dot(jnp.concatenate([x_hi, x_lo], axis=0), w_hi, preferred_element_type=F32)
    return t[:m] + t[m:] + jnp.dot(x_hi, w_lo, preferred_element_type=F32)


def _dot_exact_rhs(x, m):
    return jnp.dot(jnp.concatenate(_split3(x), axis=1), jnp.concatenate([m, m, m], axis=0),
                   preferred_element_type=F32)


def _dot_exact_lhs(m, x):
    return jnp.dot(jnp.concatenate([m, m, m], axis=1), jnp.concatenate(_split3(x), axis=0),
                   preferred_element_type=F32)


def _ada_kernel(c_ref, w_ref, b_ref, o_ref):
    c = c_ref[...]
    cond = c * jax.nn.sigmoid(c)
    o_ref[0] = jnp.dot(cond, w_ref[0], precision=HI, preferred_element_type=F32) + b_ref[0]


def _ada_mod(c, ada_w, ada_b):
    B, D = c.shape
    depth, _, N = ada_w.shape
    rows = -(-B // SUBLANES) * SUBLANES
    cp = jnp.zeros((rows, D), F32).at[:B].set(c)
    tn = N // 4
    out = pl.pallas_call(
        _ada_kernel,
        grid=(depth, N // tn),
        in_specs=[pl.BlockSpec((rows, D), lambda i, j: (0, 0)),
                  pl.BlockSpec((1, D, tn), lambda i, j: (i, 0, j)),
                  pl.BlockSpec((1, 1, tn), lambda i, j: (i, 0, j))],
        out_specs=pl.BlockSpec((1, rows, tn), lambda i, j: (i, 0, j)),
        out_shape=jax.ShapeDtypeStruct((depth, rows, N), F32),
        compiler_params=_cparams("parallel", "parallel"),
        name="ada_mod",
    )(cp, ada_w, ada_b.reshape(depth, 1, N))
    return out[:, :B]


def _in_proj_kernel(x_ref, nw_ref, sc_ref, sh_ref, w_ref,
                    qt_ref, zg_ref, kcs_ref, vcs_ref, ks_ref, vst_ref, kw_ref, vwt_ref, zr_ref):
    G, HP, dk, Q = NSA_KV_HEADS, NSA_HPG, HEAD_DIM, Q_BLOCK
    h = (_rms(x_ref[0], nw_ref[...]) * (1.0 + sc_ref[0]) + sh_ref[0]).astype(BF16)
    tm = h.shape[0]
    zq = jnp.dot(h, w_ref[:, 0:NSA_Q_DIM], preferred_element_type=F32) * (dk ** -0.5 * LOG2E)
    for nq in range(tm // Q):
        blk = zq[nq * Q:(nq + 1) * Q]
        for g in range(G):
            heads = [jnp.transpose(blk[:, (g * HP + hh) * dk:(g * HP + hh + 1) * dk]) for hh in range(HP)]
            qt_ref[0, g, nq] = jnp.concatenate(heads, axis=1).astype(BF16)
    kv_w = NSA_COLS_PAD - NSA_Q_DIM
    zkv = jnp.dot(h, w_ref[:, NSA_Q_DIM:NSA_COLS_PAD], preferred_element_type=F32)
    sec = lambda i, g: zkv[:, i * NSA_KV_DIM + g * dk:i * NSA_KV_DIM + (g + 1) * dk]
    pos = pl.program_id(1) * tm + lax.broadcasted_iota(jnp.int32, (tm, 1), 0)
    for g in range(G):
        kcs_ref[0, g] = sec(0, g)
        vcs_ref[0, g] = sec(1, g)
        ks_ref[0, g] = _with_pos_features(sec(2, g), pos)
        vst_ref[0, g] = jnp.transpose(sec(3, g)).astype(BF16)
        kw_ref[0, g] = _with_pos_features(sec(4, g), pos)
        vwt_ref[0, g] = jnp.transpose(sec(5, g)).astype(BF16)
    zg_ref[0] = zkv[:, 6 * NSA_KV_DIM:kv_w]
    n = zr_ref.shape[2]
    for c0 in range(0, n, COL_CHUNK):
        zr_ref[0, :, c0:c0 + COL_CHUNK] = jnp.dot(h, w_ref[:, NSA_COLS_PAD + c0:NSA_COLS_PAD + c0 + COL_CHUNK],
                                                  preferred_element_type=F32)


def _in_proj(x, nw, sc, sh, w):
    B, T, D = x.shape
    G, HP, dk, Q = NSA_KV_HEADS, NSA_HPG, HEAD_DIM, Q_BLOCK
    N = w.shape[1]
    tm = min(ROW_TILE, T)
    nq = tm // Q
    row = lambda n: pl.BlockSpec((1, tm, n), lambda b, i: (b, i, 0))
    per_b = pl.BlockSpec((1, 1, D), lambda b, i: (b, 0, 0))
    nat = pl.BlockSpec((1, G, tm, dk), lambda b, i: (b, 0, i, 0))
    tr = pl.BlockSpec((1, G, dk, tm), lambda b, i: (b, 0, 0, i))
    nat_s = jax.ShapeDtypeStruct((B, G, T, dk), F32)
    key = pl.BlockSpec((1, G, tm, dk + POS_FEATS), lambda b, i: (b, 0, i, 0))
    key_s = jax.ShapeDtypeStruct((B, G, T, dk + POS_FEATS), BF16)
    tr_s = jax.ShapeDtypeStruct((B, G, dk, T), BF16)
    return pl.pallas_call(
        _in_proj_kernel,
        grid=(B, T // tm),
        in_specs=[row(D), _resident((1, D)), per_b, per_b, _resident((D, N))],
        out_specs=[pl.BlockSpec((1, G, nq, dk, HP * Q), lambda b, i: (b, 0, i, 0, 0)),
                   row(NSA_COLS_PAD - NSA_Q_DIM - 6 * NSA_KV_DIM), nat, nat, key, tr, key, tr, row(RWKV_COLS_PAD)],
        out_shape=[jax.ShapeDtypeStruct((B, G, T // Q, dk, HP * Q), BF16),
                   jax.ShapeDtypeStruct((B, T, NSA_COLS_PAD - NSA_Q_DIM - 6 * NSA_KV_DIM), F32),
                   nat_s, nat_s, key_s, tr_s, key_s, tr_s,
                   jax.ShapeDtypeStruct((B, T, RWKV_COLS_PAD), F32)],
        compiler_params=_cparams("parallel", "parallel"),
        name="in_proj",
    )(x, nw.reshape(1, D), sc, sh, w)


def _nsa_compress_kernel(xk_ref, xv_ref, pe_ref, w1_ref, w2_ref, kc_ref, vct_ref):
    half = CMP_STRIDE * HEAD_DIM

    def mlp(x, s):
        pe = pe_ref[s]
        xa = (x + pe[:, :half]).astype(BF16)
        xb = (x + pe[:, half:]).astype(BF16)
        w1 = w1_ref[s]
        p = jnp.dot(xa, w1[:half], preferred_element_type=F32)
        q = jnp.dot(xb, w1[half:], preferred_element_type=F32)
        hid = _gelu(p + pltpu.roll(q, x.shape[0] - 1, 0))
        return jnp.dot(hid.astype(BF16), w2_ref[s], preferred_element_type=F32)

    nc = xk_ref.shape[1]
    cmp_end = lax.broadcasted_iota(jnp.int32, (nc, 1), 0) * CMP_STRIDE + (CMP_LEN - 1)
    kc_ref[0] = _with_pos_features(mlp(xk_ref[0], 0), cmp_end)
    vct_ref[0] = jnp.transpose(mlp(xv_ref[0], 1).astype(BF16))


def _nsa_compress(xk, xv, pe, w1, w2):
    BG, NC, W = xk.shape
    src = pl.BlockSpec((1, NC, W), lambda b: (b, 0, 0))
    return pl.pallas_call(
        _nsa_compress_kernel,
        grid=(BG,),
        in_specs=[src, src, _resident((2, 1, 2 * W)), _resident((2, 2 * W, CMP_HIDDEN)),
                  _resident((2, CMP_HIDDEN, HEAD_DIM))],
        out_specs=[pl.BlockSpec((1, NC, HEAD_DIM + POS_FEATS), lambda b: (b, 0, 0)),
                   pl.BlockSpec((1, HEAD_DIM, NC), lambda b: (b, 0, 0))],
        out_shape=[jax.ShapeDtypeStruct((BG, NC, HEAD_DIM + POS_FEATS), BF16),
                   jax.ShapeDtypeStruct((BG, HEAD_DIM, NC), BF16)],
        compiler_params=_cparams("parallel"),
        name="nsa_compress",
    )(xk, xv, pe, w1, w2)


def _nsa_attn_kernel(qt_ref, glt_ref, kc_ref, vct_ref, ks_ref, vst_ref, kw_ref, vwt_ref, aggt_ref,
                     o_ref, selt_ref, sa_ref, sb_ref, pa_ref, pb_ref, tiles_ref, *, n_slc, top_k):
    g = pl.program_id(1)
    n = pl.program_id(2)
    Q, HP, R = Q_BLOCK, NSA_HPG, NSA_HPG * Q_BLOCK
    t0 = pl.multiple_of(n * Q, Q)
    qt = qt_ref[0, 0, 0]

    def per_head(a):
        return jnp.concatenate([a] * HP, axis=1)

    slope_g = jnp.float32(1.0)
    for gi in range(1, NSA_KV_HEADS):
        slope_g = jnp.where(g == gi, jnp.float32(2.0 ** (-HP * gi)), slope_g)
    lane_h = lax.broadcasted_iota(jnp.int32, (1, R), 1) // Q
    slope = jnp.full((1, R), LOG2E * 2.0 ** -HP, F32)
    for h in range(HP - 1):
        slope = jnp.where(lane_h == h, jnp.float32(LOG2E * 2.0 ** -(h + 1)), slope)
    slope = slope * slope_g
    tq = t0 + lax.broadcasted_iota(jnp.int32, (1, Q), 1)
    s1 = slope.astype(BF16)
    s2 = (slope - s1.astype(F32)).astype(BF16)
    s3 = (slope - s1.astype(F32) - s2.astype(F32)).astype(BF16)
    pieces = [p_.astype(F32) for p_ in (s1, s2, s3)]
    srows = [p_ * float(SLC_BLOCK) for p_ in pieces] + pieces + [jnp.zeros((POS_FEATS - 6, R), F32)]
    qt = jnp.concatenate([qt, jnp.concatenate(srows, axis=0).astype(BF16)], axis=0)

    ncp = kc_ref.shape[2]
    s = jnp.dot(kc_ref[0, 0], qt, preferred_element_type=F32)
    cmp_end = lax.broadcasted_iota(jnp.int32, (ncp, 1), 0) * CMP_STRIDE + (CMP_LEN - 1)
    dist = tq - cmp_end
    bias = per_head(jnp.where(dist >= 0, 0.0, NEG_INF))
    valid = bias > 0.5 * NEG_INF
    s = s + bias
    e = jnp.where(valid, jnp.exp2(s - jnp.max(s, axis=0, keepdims=True)), 0.0)
    l = jnp.sum(e, axis=0, keepdims=True)
    p = e * (1.0 / jnp.where(l > 0.0, l, 1.0))
    o_c = jnp.dot(vct_ref[0, 0], p.astype(BF16), preferred_element_type=F32)

    psum = p[:, 0:Q]
    for h in range(1, HP):
        psum = psum + p[:, h * Q:(h + 1) * Q]
    p_hi = psum.astype(BF16)
    p_lo = (psum - p_hi.astype(F32)).astype(BF16)
    aggt = aggt_ref[...]
    imp = jnp.dot(jnp.concatenate([aggt, aggt], axis=1), jnp.concatenate([p_hi, p_lo], axis=0),
                  preferred_element_type=F32)
    blk = lax.broadcasted_iota(jnp.int32, (NS_PAD, 1), 0)
    jt = tq // SLC_BLOCK
    forced = (blk == 0) | (blk == jt) | (blk == jt - 1)
    imp = jnp.where(blk > jt, -FORCE, jnp.where(forced, FORCE, imp))
    sel = jnp.where(forced & (blk <= jt), 1.0, 0.0)
    vals = jnp.where((blk < n_slc) & (sel < 0.5), imp, -jnp.inf)
    blk_f = blk.astype(F32)
    for _ in range(max(top_k - 3, 0)):
        vmax = jnp.max(vals, axis=0, keepdims=True)
        first = jnp.min(jnp.where(vals == vmax, blk_f, float(NS_PAD)), axis=0, keepdims=True)
        pick = blk_f == first
        sel = jnp.where(pick, 1.0, sel)
        vals = jnp.where(pick, -jnp.inf, vals)
    selt_ref[...] = sel

    TK = SLC_KEY_TILE
    nb = TK // SLC_BLOCK
    n_tiles = (t0 + Q + TK - 1) // TK
    num_tiles = ks_ref.shape[2] // TK
    last_tile = num_tiles - 1
    per_blk = lax.dot_general(jnp.ones((SUBLANES, Q), BF16), sel.astype(BF16), (((1,), (1,)), ((), ())),
                              preferred_element_type=F32)
    tile_of = (lax.broadcasted_iota(jnp.int32, (NS_PAD, LANES), 0) // nb
               == lax.broadcasted_iota(jnp.int32, (NS_PAD, LANES), 1))
    per_tile = jnp.dot(per_blk.astype(BF16), jnp.where(tile_of, 1.0, 0.0).astype(BF16),
                       preferred_element_type=F32)
    lane = lax.broadcasted_iota(jnp.int32, (1, LANES), 1)
    flag = (per_tile[0:1, :] > 0.5) & (lane < n_tiles)
    bit = lax.shift_left(jnp.int32(1), lane & (TILE_MASK_BITS - 1)).astype(F32)
    words = [jnp.sum(jnp.where(flag & (lane // TILE_MASK_BITS == w_), bit, 0.0)).astype(jnp.int32)
             for w_ in range(-(-num_tiles // TILE_MASK_BITS))]
    n_used = jnp.int32(0)
    for kt in range(num_tiles):
        tiles_ref[n_used] = jnp.int32(kt)
        n_used = n_used + ((words[kt // TILE_MASK_BITS] >> (kt % TILE_MASK_BITS)) & 1)
    tiles_ref[n_used] = jnp.int32(num_tiles)
    tiles_ref[n_used + 1] = jnp.int32(num_tiles)

    WK = WINDOW + Q
    w0 = pl.multiple_of(jnp.maximum(t0 - WINDOW, 0), Q)
    sw = jnp.dot(kw_ref[0, 0, pl.ds(w0, WK), :], qt, preferred_element_type=F32)
    dw = tq - (w0 + lax.broadcasted_iota(jnp.int32, (WK, 1), 0))
    okw = (dw >= 0) & (dw < WINDOW)
    sw = sw + per_head(jnp.where(okw, 0.0, NEG_INF))
    ew = jnp.exp2(sw - jnp.max(sw, axis=0, keepdims=True))
    lw = jnp.sum(ew, axis=0, keepdims=True)
    o_w = jnp.dot(vwt_ref[0, 0, :, pl.ds(w0, WK)], ew.astype(BF16), preferred_element_type=F32) * (1.0 / lw)

    key_in_blk = lax.broadcasted_iota(jnp.int32, (SLC_BLOCK, 1), 0)

    def qk(kt, s_ref):
        k0 = pl.multiple_of(jnp.minimum(kt, last_tile) * TK, TK)
        s_ref[...] = jnp.dot(ks_ref[0, 0, pl.ds(k0, TK), :], qt, preferred_element_type=F32)

    def pv(kt, p_ref):
        k0 = pl.multiple_of(jnp.clip(kt, 0, last_tile) * TK, TK)
        return jnp.dot(vst_ref[0, 0, :, pl.ds(k0, TK)], p_ref[...], preferred_element_type=F32)

    def softmax_tile(kt, s_ref, p_ref, m_i, l_i):
        dms = []
        for j in range(nb):
            rel = (kt * TK + j * SLC_BLOCK + key_in_blk) - tq
            chosen = selt_ref[pl.ds(jnp.minimum(kt * nb + j, NS_PAD - 1), 1), :] > 0.5
            dms.append(jnp.where(chosen & (rel <= 0), 0.0, NEG_INF))
        sc = s_ref[...] + per_head(jnp.concatenate(dms, axis=0))
        m_new = jnp.maximum(m_i, jnp.max(sc, axis=0, keepdims=True))
        alpha = jnp.exp2(m_i - m_new)
        pe_ = jnp.exp2(sc - m_new)
        p_ref[...] = pe_.astype(BF16)
        return m_new, alpha, alpha * l_i + jnp.sum(pe_, axis=0, keepdims=True)

    def slc_pair(j, carry):
        m_i, l_i, acc = carry
        ka, kb = tiles_ref[2 * j], tiles_ref[2 * j + 1]
        qk(kb, sb_ref)
        acc = acc + pv(tiles_ref[jnp.maximum(2 * j - 1, 0)], pb_ref)
        m_i, alpha, l_i = softmax_tile(ka, sa_ref, pa_ref, m_i, l_i)
        acc = alpha * acc
        qk(tiles_ref[2 * j + 2], sa_ref)
        acc = acc + pv(ka, pa_ref)
        m_i, alpha, l_i = softmax_tile(kb, sb_ref, pb_ref, m_i, l_i)
        return m_i, l_i, alpha * acc

    pb_ref[...] = jnp.zeros_like(pb_ref)
    qk(tiles_ref[0], sa_ref)
    n_pairs = (n_used + 1) // 2
    init = (jnp.full((1, R), NEG_INF, F32), jnp.zeros((1, R), F32), jnp.zeros((HEAD_DIM, R), F32))
    _, l_s, acc_s = lax.fori_loop(0, n_pairs, slc_pair, init)
    acc_s = acc_s + pv(tiles_ref[2 * n_pairs - 1], pb_ref)

    gates = jax.nn.sigmoid(glt_ref[0, 0, 0])
    o_ref[0, 0, 0] = gates[0:1] * o_c + gates[1:2] * (acc_s * (1.0 / l_s)) + gates[2:3] * o_w


def _nsa_attention(qt, glt, kc, vct, ks, vst, kw, vwt, aggt, *, n_slc, top_k):
    B, G, NQ, dk, R = qt.shape
    ncp = kc.shape[2]
    T = ks.shape[2]
    assert T >= WINDOW + Q_BLOCK
    per_bg = lambda b, g, n: (b, g, 0, 0)
    per_q = lambda rows: pl.BlockSpec((1, 1, 1, rows, R), lambda b, g, n: (b, g, n, 0, 0))
    return pl.pallas_call(
        functools.partial(_nsa_attn_kernel, n_slc=n_slc, top_k=top_k),
        grid=(B, G, NQ),
        in_specs=[per_q(dk), per_q(3),
                  pl.BlockSpec((1, 1, ncp, dk + POS_FEATS), per_bg),
                  pl.BlockSpec((1, 1, dk, ncp), per_bg),
                  pl.BlockSpec((1, 1, T, dk + POS_FEATS), per_bg),
                  pl.BlockSpec((1, 1, dk, T), per_bg),
                  pl.BlockSpec((1, 1, T, dk + POS_FEATS), per_bg),
                  pl.BlockSpec((1, 1, dk, T), per_bg),
                  _resident((NS_PAD, ncp))],
        out_specs=per_q(dk),
        out_shape=jax.ShapeDtypeStruct((B, G, NQ, dk, R), F32),
        scratch_shapes=[pltpu.VMEM((NS_PAD, Q_BLOCK), F32),
                        pltpu.VMEM((SLC_KEY_TILE, R), F32), pltpu.VMEM((SLC_KEY_TILE, R), F32),
                        pltpu.VMEM((SLC_KEY_TILE, R), BF16), pltpu.VMEM((SLC_KEY_TILE, R), BF16),
                        pltpu.SMEM((T // SLC_KEY_TILE + 2,), jnp.int32)],
        compiler_params=_cparams("parallel", "parallel", "arbitrary"),
        name="nsa_attention",
    )(qt, glt, kc, vct, ks, vst, kw, vwt, aggt)


def _rwkv_pre_kernel(z_ref, halo_ref, mu_ref, w0_ref, a0_ref, kk_ref, ka_ref, rk_ref, wup_ref, aup_ref, gup_ref,
                     ones_ref, csum_ref,
                     kap_ref, r_ref, v_ref, kend_ref, bend_ref, bt_ref, kt_ref, gend_ref, bonus_ref, g_ref):
    i = pl.program_id(1)
    z = z_ref[0]
    tm = z.shape[0]
    halo = jnp.where(i > 0, halo_ref[0], 0.0)
    zs = z + (_prev_rows(z, halo, 1) - z) * mu_ref[...]
    D = RWKV_DIM
    r, k, v = zs[:, 0:D], zs[:, D:2 * D], zs[:, 2 * D:3 * D]
    o1 = 3 * D
    o2 = o1 + RWKV_LORA_PAD
    o3 = o2 + RWKV_LORA_PAD
    wl, al, gl = zs[:, o1:o2], zs[:, o2:o3], zs[:, o3:]
    lw = -DECAY_SCALE * jax.nn.sigmoid(w0_ref[...] + _dot_bf16x3(jnp.tanh(wl), wup_ref[...]))
    a = jax.nn.sigmoid(a0_ref[...] + _dot_bf16x3(al, aup_ref[...]))
    kt = k * (1.0 + (a - 1.0) * ka_ref[...])
    kk = k * kk_ref[...]
    head_sums = _dot_exact_rhs(jnp.concatenate([kk * kk, r * kt * rk_ref[...]], axis=0), ones_ref[...])
    kk = kk / jnp.maximum(jnp.sqrt(head_sums[:tm]), 1e-12)
    b = kk * a
    C = RWKV_CHUNK
    cum = _dot_exact_lhs(csum_ref[...], lw)
    tot = jnp.concatenate([jnp.broadcast_to(cum[c * C + C - 1:c * C + C], (C, cum.shape[1]))
                           for c in range(tm // C)], axis=0)
    g_inv = jnp.exp(-cum)
    tail = jnp.exp(tot - cum)
    kap_ref[0] = kk * jnp.exp(cum - lw)
    r_ref[0] = r * jnp.exp(cum)
    v_ref[0] = v
    kend_ref[0] = kt * tail
    bend_ref[0] = b * tail
    b_h, k_h, g_end = b * g_inv, kt * g_inv, jnp.exp(tot)
    for c in range(tm // C):
        bt_ref[0, c] = jnp.transpose(b_h[c * C:(c + 1) * C])
        kt_ref[0, c] = jnp.transpose(k_h[c * C:(c + 1) * C])
    gend_ref[0] = jnp.concatenate([g_end[c * C:c * C + 1] for c in range(tm // C)], axis=0)
    bonus_ref[0] = head_sums[tm:] * v
    g_ref[0] = _dot_bf16x3(jax.nn.sigmoid(gl), gup_ref[...])


def _rwkv_pre(z, mu, w0, a0, k_k, k_a, r_k, w_up, a_up, g_up):
    B, T, W = z.shape
    D = RWKV_DIM
    tm = min(ROW_TILE, T)
    hb = tm // SUBLANES
    C = RWKV_CHUNK
    ones_bd = jnp.asarray(np.kron(np.eye(RWKV_HEADS), np.ones((HEAD_DIM, HEAD_DIM))), BF16)
    csum = jnp.asarray(np.kron(np.eye(tm // C), np.tril(np.ones((C, C)))), BF16)
    out = jax.ShapeDtypeStruct((B, T, D), F32)
    row = pl.BlockSpec((1, tm, D), lambda b, i: (b, i, 0))
    tr = pl.BlockSpec((1, tm // C, D, C), lambda b, i: (b, i, 0, 0))
    tr_s = jax.ShapeDtypeStruct((B, T // C, D, C), F32)
    return pl.pallas_call(
        _rwkv_pre_kernel,
        grid=(B, T // tm),
        in_specs=[pl.BlockSpec((1, tm, W), lambda b, i: (b, i, 0)),
                  pl.BlockSpec((1, SUBLANES, W), lambda b, i: (b, jnp.maximum(i * hb - 1, 0), 0)),
                  _resident((1, W)), _resident((1, D)), _resident((1, D)), _resident((1, D)), _resident((1, D)),
                  _resident((1, D)), _resident((RWKV_LORA_PAD, D)), _resident((RWKV_LORA_PAD, D)),
                  _resident((RWKV_GATE_PAD, D)), _resident((D, D)), _resident((tm, tm))],
        out_specs=[row] * 5 + [tr, tr, pl.BlockSpec((1, tm // C, D), lambda b, i: (b, i, 0)), row, row],
        out_shape=[out] * 5 + [tr_s, tr_s, jax.ShapeDtypeStruct((B, T // C, D), F32), out, out],
        compiler_params=_cparams("parallel", "parallel"),
        name="rwkv_pre",
    )(z, z, mu.reshape(1, W), w0.reshape(1, D), a0.reshape(1, D), k_k.reshape(1, D), k_a.reshape(1, D),
      r_k.reshape(1, D), w_up, a_up, g_up, ones_bd, csum)


def _rwkv_core_kernel(kap_ref, r_ref, v_ref, kend_ref, bend_ref, bt_ref, kt_ref, gend_ref, o_ref, h_ref):
    C, N, P = RWKV_CHUNK, HEAD_DIM, RWKV_PACK
    W = P * N
    chains = [(bi, slice(p * W, (p + 1) * W)) for bi in range(kap_ref.shape[0]) for p in range(kap_ref.shape[2] // W)]

    @pl.when(pl.program_id(1) == 0)
    def _():
        h_ref[...] = jnp.zeros_like(h_ref)

    ri = lax.broadcasted_iota(jnp.int32, (C, W), 0)
    cj = lax.broadcasted_iota(jnp.int32, (C, W), 1) % N
    tril = (ri >= cj).astype(F32)
    stril = (ri > cj).astype(F32)
    eye = (ri == cj).astype(F32)
    diag_blk = ((ri // RWKV_SUB) == (cj // RWKV_SUB)).astype(F32)
    same_head = (lax.broadcasted_iota(jnp.int32, (W, W), 0) // N) == (lax.broadcasted_iota(jnp.int32, (W, W), 1) // N)

    def split(x):
        hi = x.astype(BF16)
        return hi, (x - hi.astype(F32)).astype(BF16)

    def lhs2(x):
        hi, lo = split(x)
        return jnp.concatenate([hi, lo], axis=0), hi

    def bd(x):
        blk = lambda a: jnp.where(same_head, jnp.concatenate([a] * P, axis=0), jnp.zeros((), BF16))
        hi, lo = split(x)
        return blk(hi), blk(lo)

    def bd_t(xt):
        return split(jnp.where(same_head, jnp.concatenate([xt] * P, axis=1), 0.0))

    def mmw(l2, w):
        (x2, x_hi), (w_hi, w_lo) = l2, w
        m = x_hi.shape[0]
        t = jnp.dot(x2, w_hi, preferred_element_type=F32)
        return t[:m] + t[m:] + jnp.dot(x_hi, w_lo, preferred_element_type=F32)

    def mmp(x, y):
        return mmw(lhs2(x), bd(y))

    def gm(f, *cols):
        return [f(*args) for args in zip(*cols)]

    def chunk(c, hs):
        rows = pl.ds(pl.multiple_of(c * C, C), C)
        v = [v_ref[bi, rows, ln] for bi, ln in chains]
        lhs = [lhs2(jnp.concatenate([kap_ref[bi, rows, ln], r_ref[bi, rows, ln]], axis=0)) for bi, ln in chains]
        ab = gm(lambda l, ch: mmw(l, bd_t(bt_ref[ch[0], c, ch[1], :])), lhs, chains)
        ak = gm(lambda l, ch: mmw(l, bd_t(kt_ref[ch[0], c, ch[1], :])), lhs, chains)
        a_kb = [stril * x[:C] for x in ab]
        a_rb = [tril * x[C:] for x in ab]
        a_kr = [jnp.concatenate([stril * x[:C], tril * x[C:]], axis=0) for x in ak]
        d = [x * diag_blk for x in a_kb]
        e = gm(lambda x, y: x - y, a_kb, d)
        d_inv = [eye - x for x in d]
        pw = gm(mmp, d, d)
        steps = int(np.log2(RWKV_SUB)) - 1
        for s_ in range(steps):
            d_inv = gm(lambda x, y: mmp(x, eye + y), d_inv, pw)
            if s_ + 1 < steps:
                pw = gm(mmp, pw, pw)
        nb = gm(mmp, d_inv, e)
        t_inv = [eye - x for x in nb]
        pw = gm(mmp, nb, nb)
        bsteps = int(np.log2(C // RWKV_SUB)) - 1
        for s_ in range(bsteps):
            t_inv = gm(lambda x, y: mmp(x, eye + y), t_inv, pw)
            if s_ + 1 < bsteps:
                pw = gm(mmp, pw, pw)
        t_inv = gm(mmp, t_inv, d_inv)
        sh = gm(lambda l, h: mmw(l, bd(h)), lhs, hs)
        av = gm(mmp, a_kr, v)
        u = gm(lambda t, x, y: mmp(t, x[:C] + y[:C]), t_inv, sh, av)
        y = gm(lambda x, z, a, uu: x[C:] + z[C:] - mmp(a, uu), sh, av, a_rb, u)
        for (bi, ln), yy in zip(chains, y):
            o_ref[bi, rows, ln] = yy
        zero = jnp.zeros((C, W), F32)
        left = [jnp.concatenate([eye * gend_ref[bi, 0, pl.ds(c, 1), ln], kend_ref[bi, rows, ln],
                                 bend_ref[bi, rows, ln], zero], axis=0) for bi, ln in chains]
        right = gm(lambda h, vv, uu: split(jnp.concatenate([h, vv, -uu, zero], axis=0)), hs, v, u)
        full = gm(lambda l, rr: jnp.where(same_head, mmw(lhs2(jnp.transpose(l)), rr), 0.0), left, right)
        return tuple(sum(f[s_ * N:(s_ + 1) * N] for s_ in range(1, P)) + f[0:N] for f in full)

    hs = lax.fori_loop(0, kap_ref.shape[1] // C, chunk, tuple(h_ref[p] for p in range(len(chains))))
    for p in range(len(chains)):
        h_ref[p] = hs[p]


def _rwkv_core(kap, r, v, kend, bend, bt, kt, gend):
    B, T, D = kap.shape
    W = RWKV_PACK * HEAD_DIM
    assert D % W == 0 and W == 4 * RWKV_CHUNK
    bb = RWKV_BATCH_PER_STEP if B % RWKV_BATCH_PER_STEP == 0 else 1
    tc = min(RWKV_STEP_ROWS, T)
    nch = tc // RWKV_CHUNK
    seq = pl.BlockSpec((bb, tc, D), lambda b, i: (b, i, 0))
    tr = pl.BlockSpec((bb, nch, D, RWKV_CHUNK), lambda b, i: (b, i, 0, 0))
    return pl.pallas_call(
        _rwkv_core_kernel,
        grid=(B // bb, T // tc),
        in_specs=[seq] * 5 + [tr] * 2 + [pl.BlockSpec((bb, 1, nch, D), lambda b, i: (b, i, 0, 0))],
        out_specs=seq,
        out_shape=jax.ShapeDtypeStruct((B, T, D), F32),
        scratch_shapes=[pltpu.VMEM((bb * (D // W), HEAD_DIM, W), F32)],
        compiler_params=_cparams("parallel", "arbitrary"),
        name="rwkv_core",
    )(kap, r, v, kend, bend, bt, kt, gend.reshape(B, T // tc, nch, D))


def _out_proj_kernel(ot_ref, y_ref, bonus_ref, gg_ref, gw_ref, gb_ref, ones_ref, wa_ref, wb_ref, x_ref, g_ref,
                     o_ref):
    G, HP, Q = NSA_KV_HEADS, NSA_HPG, Q_BLOCK
    blocks = []
    for nq in range(ot_ref.shape[2]):
        heads = [jnp.transpose(ot_ref[0, g, nq][:, hh * Q:(hh + 1) * Q]) for g in range(G) for hh in range(HP)]
        blocks.append(jnp.concatenate(heads, axis=1))
    ya = jnp.concatenate(blocks, axis=0).astype(BF16)
    y = y_ref[0]
    inv_n = 1.0 / HEAD_DIM
    dlt = y - _dot_exact_rhs(y, ones_ref[...]) * inv_n
    var = _dot_exact_rhs(dlt * dlt, ones_ref[...]) * inv_n
    yb = (dlt * lax.rsqrt(var + RWKV_GN_EPS) * gw_ref[...] + gb_ref[...] + bonus_ref[0]) * gg_ref[0]
    o = jnp.dot(ya, wa_ref[...], preferred_element_type=F32)
    o = o + jnp.dot(yb.astype(BF16), wb_ref[...], preferred_element_type=F32)
    o_ref[0] = x_ref[0] + g_ref[0] * o


def _out_proj(ot, y, bonus, gg, gn_w, gn_b, wa, wb, x, gate):
    B, T, D = x.shape
    G, NQ, dk, R = ot.shape[1:]
    tm = min(ROW_TILE, T)
    ka, kb = wa.shape[0], y.shape[-1]
    ones_bd = jnp.asarray(np.kron(np.eye(kb // HEAD_DIM), np.ones((HEAD_DIM, HEAD_DIM))), BF16)
    row = lambda w: pl.BlockSpec((1, tm, w), lambda b, i: (b, i, 0))
    return pl.pallas_call(
        _out_proj_kernel,
        grid=(B, T // tm),
        in_specs=[pl.BlockSpec((1, G, tm // Q_BLOCK, dk, R), lambda b, i: (b, 0, i, 0, 0)),
                  row(kb), row(kb), row(kb), _resident((1, kb)), _resident((1, kb)), _resident((kb, kb)),
                  _resident((ka, D)), _resident((kb, D)), row(D),
                  pl.BlockSpec((1, 1, D), lambda b, i: (b, 0, 0))],
        out_specs=row(D),
        out_shape=jax.ShapeDtypeStruct((B, T, D), F32),
        compiler_params=_cparams("parallel", "parallel"),
        name="out_proj",
    )(ot, y, bonus, gg, gn_w.reshape(1, kb), gn_b.reshape(1, kb), ones_bd, wa, wb, x, gate)


def _sgu_kernel(x_ref, nw_ref, sc_ref, sh_ref, g_ref, wi_ref, bi_ref, vnw_ref, vnb_ref, ws_ref, bst_ref,
                wo_ref, bo_ref, o_ref, u_ref, v_ref, gated_ref):
    x = x_ref[0]
    tm = x.shape[0]
    E = SGU_DIM
    h = (_rms(x, nw_ref[...]) * (1.0 + sc_ref[0]) + sh_ref[0]).astype(BF16)
    for c0 in range(0, 2 * E, COL_CHUNK):
        zc = _gelu(jnp.dot(h, wi_ref[:, c0:c0 + COL_CHUNK], preferred_element_type=F32) + bi_ref[:, c0:c0 + COL_CHUNK])
        if c0 < E:
            u_ref[:, c0:c0 + COL_CHUNK] = zc
        else:
            v_ref[:, c0 - E:c0 - E + COL_CHUNK] = zc
    v = v_ref[...]
    mu = jnp.mean(v, axis=-1, keepdims=True)
    var = jnp.mean(jnp.square(v - mu), axis=-1, keepdims=True)
    vn = ((v - mu) * lax.rsqrt(var + LN_EPS) * vnw_ref[...] + vnb_ref[...]).astype(BF16)
    S = SGU_CHUNK
    gw = E // SGU_GROUPS
    causal = lax.broadcasted_iota(jnp.int32, (S, S), 0) >= lax.broadcasted_iota(jnp.int32, (S, S), 1)
    for gi in range(SGU_GROUPS):
        ws = jnp.where(causal, ws_ref[gi], 0.0).astype(BF16)
        bs = bst_ref[:, gi:gi + 1]
        for n in range(tm // S):
            sv = jnp.dot(ws, vn[n * S:(n + 1) * S, gi * gw:(gi + 1) * gw], preferred_element_type=F32) + bs
            u = u_ref[n * S:(n + 1) * S, gi * gw:(gi + 1) * gw]
            gated_ref[n * S:(n + 1) * S, gi * gw:(gi + 1) * gw] = (u * sv).astype(BF16)
    y = jnp.dot(gated_ref[...], wo_ref[...], preferred_element_type=F32) + bo_ref[...]
    o_ref[0] = x + g_ref[0] * y


def _sgu(x, nw, sc, sh, gate, w_in, b_in, vn_w, vn_b, w_s, b_s_t, w_out, b_out):
    B, T, D = x.shape
    E = SGU_DIM
    tm = min(ROW_TILE, T)
    assert tm % SGU_CHUNK == 0
    row = pl.BlockSpec((1, tm, D), lambda b, i: (b, i, 0))
    per_b = pl.BlockSpec((1, 1, D), lambda b, i: (b, 0, 0))
    return pl.pallas_call(
        _sgu_kernel,
        grid=(B, T // tm),
        in_specs=[row, _resident((1, D)), per_b, per_b, per_b,
                  _resident((D, 2 * E)), _resident((1, 2 * E)), _resident((1, E)), _resident((1, E)),
                  _resident((SGU_GROUPS, SGU_CHUNK, SGU_CHUNK)), _resident((SGU_CHUNK, SGU_GROUPS)),
                  _resident((E, D)), _resident((1, D))],
        out_specs=row,
        out_shape=jax.ShapeDtypeStruct((B, T, D), F32),
        scratch_shapes=[pltpu.VMEM((tm, E), F32), pltpu.VMEM((tm, E), F32), pltpu.VMEM((tm, E), BF16)],
        compiler_params=_cparams("parallel", "parallel"),
        name="sgu",
    )(x, nw.reshape(1, D), sc, sh, gate, w_in, b_in.reshape(1, 2 * E), vn_w.reshape(1, E), vn_b.reshape(1, E),
      w_s, b_s_t, w_out, b_out.reshape(1, D))


def _ffn_kernel(x_ref, halo_ref, nw_ref, sc_ref, sh_ref, g_ref, wg_ref, wu_ref, cw_ref, cb_ref, wd_ref, fn_ref,
                o_ref, act_ref, *, final):
    i = pl.program_id(1)
    x = x_ref[0]
    tm = x.shape[0]
    F = wg_ref.shape[1]
    HR = FFN_HALO_ROWS
    hx = (_rms(jnp.concatenate([halo_ref[0], x], axis=0), nw_ref[...]) * (1.0 + sc_ref[0]) + sh_ref[0]).astype(BF16)
    h = hx[HR:]
    for c0 in range(0, F, FFN_COL_CHUNK):
        cols = slice(c0, c0 + FFN_COL_CHUNK)
        ext = jnp.dot(hx, wg_ref[:, cols], preferred_element_type=F32)
        ext = jnp.concatenate([jnp.where(i > 0, ext[:HR], 0.0), ext[HR:]], axis=0)
        cw = cw_ref[:, cols]
        a = (cw[0:1] * ext[HR - 2:HR - 2 + tm] + cw[1:2] * ext[HR - 1:HR - 1 + tm] + cw[2:3] * ext[HR:]
             + cb_ref[:, cols])
        up = jnp.dot(h, wu_ref[:, cols], preferred_element_type=F32)
        act_ref[:, cols] = (_gelu(a) * up).astype(BF16)
    acc = jnp.dot(act_ref[...], wd_ref[...], preferred_element_type=F32)
    xn = x + g_ref[0] * acc
    if final:
        xn = _rms(xn, fn_ref[...])
    o_ref[0] = xn


def _ffn(x, nw, sc, sh, gate, w_gate, w_up, conv_w, conv_b, w_down, final_w, *, final):
    B, T, D = x.shape
    F = w_down.shape[0]
    tm = min(ROW_TILE, T)
    hb = tm // FFN_HALO_ROWS
    row = pl.BlockSpec((1, tm, D), lambda b, i: (b, i, 0))
    per_b = pl.BlockSpec((1, 1, D), lambda b, i: (b, 0, 0))
    return pl.pallas_call(
        functools.partial(_ffn_kernel, final=final),
        grid=(B, T // tm),
        in_specs=[row,
                  pl.BlockSpec((1, FFN_HALO_ROWS, D), lambda b, i: (b, jnp.maximum(i * hb - 1, 0), 0)),
                  _resident((1, D)), per_b, per_b, per_b,
                  _resident((D, F)), _resident((D, F)), _resident((CONV_WIDTH, F)), _resident((1, F)),
                  _resident((F, D)), _resident((1, D))],
        out_specs=row,
        out_shape=jax.ShapeDtypeStruct((B, T, D), F32),
        scratch_shapes=[pltpu.VMEM((tm, F), BF16)],
        compiler_params=_cparams("parallel", "parallel"),
        name="ffn",
    )(x, x, nw.reshape(1, D), sc, sh, gate, w_gate, w_up, conv_w, conv_b.reshape(1, F), w_down,
      final_w.reshape(1, D))


def _pad_cols(w, sizes, padded):
    parts, o = [], 0
    for s, p in zip(sizes, padded):
        parts.append(jnp.pad(w[..., o:o + s], [(0, 0)] * (w.ndim - 1) + [(0, p - s)]))
        o += s
    return jnp.concatenate(parts, axis=-1)


def _pad_rows(w, rows):
    return jnp.pad(w, ((0, rows - w.shape[0]), (0, 0)))


def _nsa_agg_t(T):
    n_cmp_pad = T // CMP_STRIDE
    n_slc = T // SLC_BLOCK
    c = np.arange(n_cmp_pad)
    s = np.arange(NS_PAD)
    cs, ce, ss = c * CMP_STRIDE, c * CMP_STRIDE + CMP_LEN - 1, s * SLC_BLOCK
    agg = (cs[None, :] < ss[:, None] + SLC_BLOCK) & (ce[None, :] >= ss[:, None]) & (s[:, None] < n_slc)
    agg &= (c[None, :] < n_cmp_pad - 1)
    return jnp.asarray(agg, BF16)


def _nsa_rwkv_mixer(x, nw, sc, sh, gate, w_in, cmp_pe, cmp_w1, cmp_w2, mu, w0, w_up, a0, a_up, g_up,
                    k_k, k_a, r_k, gn_w, gn_b, w_out):
    B, T, D = x.shape
    G, HPG, dk = NSA_KV_HEADS, NSA_HPG, HEAD_DIM
    n_slc = T // SLC_BLOCK
    NQ = T // Q_BLOCK
    assert n_slc <= NS_PAD and T % (CMP_STRIDE * LANES) == 0 and T % (2 * SLC_KEY_TILE) == 0

    rw_sizes = [RWKV_DIM] * 3 + [DECAY_LORA, AAA_LORA, GATE_LORA]
    rw_pads = [RWKV_DIM] * 3 + [RWKV_LORA_PAD, RWKV_LORA_PAD, RWKV_GATE_PAD]
    w_all = jnp.concatenate([jnp.pad(w_in[:, :NSA_COLS], ((0, 0), (0, NSA_COLS_PAD - NSA_COLS))),
                             _pad_cols(w_in[:, NSA_COLS:], rw_sizes, rw_pads)], axis=1).astype(BF16)
    qt, zg, kcs, vcs, ks, vst, kw, vwt, z_rw = _in_proj(x, nw, sc, sh, w_all)

    glt = zg[..., :NSA_GATE_DIM].reshape(B, NQ, Q_BLOCK, G, HPG, 3).transpose(0, 3, 1, 5, 4, 2)
    glt = glt.reshape(B, G, NQ, 3, HPG * Q_BLOCK)
    strides = lambda a: a.reshape(B * G, T // CMP_STRIDE, CMP_STRIDE * dk)
    kc, vct = _nsa_compress(strides(kcs), strides(vcs), cmp_pe.reshape(2, 1, CMP_LEN * dk),
                            cmp_w1.astype(BF16), cmp_w2.astype(BF16))
    o_t = _nsa_attention(qt, glt, kc.reshape(B, G, T // CMP_STRIDE, -1), vct.reshape(B, G, dk, T // CMP_STRIDE),
                         ks, vst, kw, vwt, _nsa_agg_t(T), n_slc=n_slc, top_k=min(SLC_TOPK, n_slc))

    kap, r, v, kend, bend, bt, kt, gend, bonus, gg = _rwkv_pre(
        z_rw, _pad_cols(mu, rw_sizes, rw_pads), w0, a0, k_k, k_a, r_k,
        _pad_rows(w_up, RWKV_LORA_PAD), _pad_rows(a_up, RWKV_LORA_PAD), _pad_rows(g_up, RWKV_GATE_PAD))
    y_b = _rwkv_core(kap, r, v, kend, bend, bt, kt, gend)

    w_out = w_out.astype(BF16)
    return _out_proj(o_t, y_b, bonus, gg, gn_w, gn_b, w_out[:NSA_Q_DIM], w_out[NSA_Q_DIM:], x, gate)


def kernel(x, c, ada_w, ada_b, norm_mix, norm_ffn, ffn_w_gate, ffn_w_up, ffn_conv_w, ffn_conv_b, ffn_w_down, ab_w_in, nsa_cmp_pe, nsa_cmp_w1, nsa_cmp_w2, rwkv_mu, rwkv_w0, rwkv_w_up, rwkv_a0, rwkv_a_up, rwkv_g_up, rwkv_k_k, rwkv_k_a, rwkv_r_k, rwkv_gn_w, rwkv_gn_b, ab_w_out, sgu_w_in, sgu_b_in, sgu_vn_w, sgu_vn_b, sgu_w_s, sgu_b_s, sgu_w_out, sgu_b_out, final_norm):
    B, T, D = x.shape
    mod = _ada_mod(c, ada_w, ada_b)
    for i in range(DEPTH):
        sh1, sc1, g1, sh2, sc2, g2 = [m.reshape(B, 1, D) for m in jnp.split(mod[i], 6, axis=-1)]
        j = i // 2
        if i % 2 == 0:
            x = _nsa_rwkv_mixer(x, norm_mix[i], sc1, sh1, g1, ab_w_in[j], nsa_cmp_pe[j], nsa_cmp_w1[j],
                                nsa_cmp_w2[j], rwkv_mu[j], rwkv_w0[j], rwkv_w_up[j], rwkv_a0[j], rwkv_a_up[j],
                                rwkv_g_up[j], rwkv_k_k[j], rwkv_k_a[j], rwkv_r_k[j], rwkv_gn_w[j], rwkv_gn_b[j],
                                ab_w_out[j])
        else:
            x = _sgu(x, norm_mix[i], sc1, sh1, g1, sgu_w_in[j].astype(BF16), sgu_b_in[j], sgu_vn_w[j], sgu_vn_b[j],
                     sgu_w_s[j], sgu_b_s[j].T, sgu_w_out[j].astype(BF16), sgu_b_out[j])
        x = _ffn(x, norm_ffn[i], sc2, sh2, g2, ffn_w_gate[i].astype(BF16), ffn_w_up[i].astype(BF16), ffn_conv_w[i],
                 ffn_conv_b[i], ffn_w_down[i].astype(BF16), final_norm, final=(i == DEPTH - 1))
    return x
```

```python
import functools

import numpy as np
import jax
import jax.numpy as jnp
from jax import lax
from jax.experimental import pallas as pl
from jax.experimental.pallas import tpu as pltpu

F32 = jnp.float32
BF16 = jnp.bfloat16
HI = lax.Precision.HIGHEST

D_MODEL = 1024
DEPTH = 2
HEAD_DIM = 64
NSA_HEADS = 8
NSA_KV_HEADS = 2
NSA_HPG = NSA_HEADS // NSA_KV_HEADS
CMP_STRIDE = 16
CMP_LEN = 2 * CMP_STRIDE
CMP_HIDDEN = 128
SLC_BLOCK = 64
SLC_TOPK = 16
WINDOW = 512
Q_BLOCK = 128
FORCE = 1e4
RWKV_HEADS = 8
RWKV_DIM = RWKV_HEADS * HEAD_DIM
DECAY_LORA = 64
AAA_LORA = 64
GATE_LORA = 160
DECAY_SCALE = 0.6065306597126334
RWKV_GN_EPS = 64e-5
SGU_CHUNK = 128
SGU_GROUPS = 8
SGU_DIM = 2048
FFN_DIM = 2816
CONV_WIDTH = 3
NORM_EPS = 1e-6
LN_EPS = 1e-5
NEG_INF = -1e30
LOG2E = 1.4426950408889634
NSA_Q_DIM = NSA_HEADS * HEAD_DIM
NSA_KV_DIM = NSA_KV_HEADS * HEAD_DIM
NSA_GATE_DIM = NSA_HEADS * 3
NSA_COLS = NSA_Q_DIM + 6 * NSA_KV_DIM + NSA_GATE_DIM

LANES = 128
SUBLANES = 8
VMEM_LIMIT_BYTES = 52 * 1024 * 1024

ROW_TILE = 512
COL_CHUNK = 512
NSA_COLS_PAD = 1408
RWKV_LORA_PAD = 128
RWKV_GATE_PAD = 256
RWKV_COLS_PAD = 3 * RWKV_DIM + 2 * RWKV_LORA_PAD + RWKV_GATE_PAD
SLC_KEY_TILE = 256
NS_PAD = LANES
TILE_MASK_BITS = 16
POS_FEATS = 16
RWKV_CHUNK = 64
RWKV_SUB = 16
RWKV_PACK = 4
RWKV_STEP_ROWS = 128
RWKV_BATCH_PER_STEP = 4
FFN_COL_CHUNK = 256
FFN_HALO_ROWS = 16


def _cparams(*sem):
    return pltpu.CompilerParams(dimension_semantics=sem, vmem_limit_bytes=VMEM_LIMIT_BYTES)


def _resident(shape):
    return pl.BlockSpec(shape, lambda *_: (0,) * len(shape), pipeline_mode=pl.Buffered(1))


def _gelu(x):
    return 0.5 * x * (1.0 + jnp.tanh(0.7978845608028654 * (x + 0.044715 * (x * x * x))))


def _rms(x, w):
    return x * lax.rsqrt(jnp.mean(x * x, axis=-1, keepdims=True) + NORM_EPS) * w


def _prev_rows(cur, halo, shift):
    tm = cur.shape[0]
    ext = jnp.concatenate([halo, cur], axis=0)
    return ext[SUBLANES - shift:SUBLANES - shift + tm]


def _split3(x):
    hi = x.astype(BF16)
    r1 = x - hi.astype(F32)
    mid = r1.astype(BF16)
    return hi, mid, (r1 - mid.astype(F32)).astype(BF16)


def _with_pos_features(k, pos):
    lane = lax.broadcasted_iota(jnp.int32, (k.shape[0], POS_FEATS), 1)
    feats = jnp.where(lane < 3, pos // SLC_BLOCK, jnp.where(lane < 6, pos % SLC_BLOCK, 0))
    return jnp.concatenate([k, feats.astype(F32)], axis=1).astype(BF16)


def _dot_bf16x3(x, w):
    x_hi, w_hi = x.astype(BF16), w.astype(BF16)
    x_lo, w_lo = (x - x_hi.astype(F32)).astype(BF16), (w - w_hi.astype(F32)).astype(BF16)
    m = x.shape[0]
    t = jnp.dot(jnp.concatenate([x_hi, x_lo], axis=0), w_hi, preferred_element_type=F32)
    return t[:m] + t[m:] + jnp.dot(x_hi, w_lo, preferred_element_type=F32)


def _dot_exact_rhs(x, m):
    return jnp.dot(jnp.concatenate(_split3(x), axis=1), jnp.concatenate([m, m, m], axis=0),
                   preferred_element_type=F32)


def _dot_exact_lhs(m, x):
    return jnp.dot(jnp.concatenate([m, m, m], axis=1), jnp.concatenate(_split3(x), axis=0),
                   preferred_element_type=F32)


def _ada_kernel(c_ref, w_ref, b_ref, o_ref):
    c = c_ref[...]
    cond = c * jax.nn.sigmoid(c)
    o_ref[0] = jnp.dot(cond, w_ref[0], precision=HI, preferred_element_type=F32) + b_ref[0]


def _ada_mod(c, ada_w, ada_b):
    B, D = c.shape
    depth, _, N = ada_w.shape
    rows = -(-B // SUBLANES) * SUBLANES
    cp = jnp.zeros((rows, D), F32).at[:B].set(c)
    tn = N // 4
    out = pl.pallas_call(
        _ada_kernel,
        grid=(depth, N // tn),
        in_specs=[pl.BlockSpec((rows, D), lambda i, j: (0, 0)),
                  pl.BlockSpec((1, D, tn), lambda i, j: (i, 0, j)),
                  pl.BlockSpec((1, 1, tn), lambda i, j: (i, 0, j))],
        out_specs=pl.BlockSpec((1, rows, tn), lambda i, j: (i, 0, j)),
        out_shape=jax.ShapeDtypeStruct((depth, rows, N), F32),
        compiler_params=_cparams("parallel", "parallel"),
        name="ada_mod",
    )(cp, ada_w, ada_b.reshape(depth, 1, N))
    return out[:, :B]


def _in_proj_kernel(x_ref, nw_ref, sc_ref, sh_ref, w_ref,
                    qt_ref, zg_ref, kcs_ref, vcs_ref, ks_ref, vst_ref, kw_ref, vwt_ref, zr_ref):
    G, HP, dk, Q = NSA_KV_HEADS, NSA_HPG, HEAD_DIM, Q_BLOCK
    h = (_rms(x_ref[0], nw_ref[...]) * (1.0 + sc_ref[0]) + sh_ref[0]).astype(BF16)
    tm = h.shape[0]
    zq = jnp.dot(h, w_ref[:, 0:NSA_Q_DIM], preferred_element_type=F32) * (dk ** -0.5 * LOG2E)
    for nq in range(tm // Q):
        blk = zq[nq * Q:(nq + 1) * Q]
        for g in range(G):
            heads = [jnp.transpose(blk[:, (g * HP + hh) * dk:(g * HP + hh + 1) * dk]) for hh in range(HP)]
            qt_ref[0, g, nq] = jnp.concatenate(heads, axis=1).astype(BF16)
    kv_w = NSA_COLS_PAD - NSA_Q_DIM
    zkv = jnp.dot(h, w_ref[:, NSA_Q_DIM:NSA_COLS_PAD], preferred_element_type=F32)
    sec = lambda i, g: zkv[:, i * NSA_KV_DIM + g * dk:i * NSA_KV_DIM + (g + 1) * dk]
    pos = pl.program_id(1) * tm + lax.broadcasted_iota(jnp.int32, (tm, 1), 0)
    for g in range(G):
        kcs_ref[0, g] = sec(0, g)
        vcs_ref[0, g] = sec(1, g)
        ks_ref[0, g] = _with_pos_features(sec(2, g), pos)
        vst_ref[0, g] = jnp.transpose(sec(3, g)).astype(BF16)
        kw_ref[0, g] = _with_pos_features(sec(4, g), pos)
        vwt_ref[0, g] = jnp.transpose(sec(5, g)).astype(BF16)
    zg_ref[0] = zkv[:, 6 * NSA_KV_DIM:kv_w]
    n = zr_ref.shape[2]
    for c0 in range(0, n, COL_CHUNK):
        zr_ref[0, :, c0:c0 + COL_CHUNK] = jnp.dot(h, w_ref[:, NSA_COLS_PAD + c0:NSA_COLS_PAD + c0 + COL_CHUNK],
                                                  preferred_element_type=F32)


def _in_proj(x, nw, sc, sh, w):
    B, T, D = x.shape
    G, HP, dk, Q = NSA_KV_HEADS, NSA_HPG, HEAD_DIM, Q_BLOCK
    N = w.shape[1]
    tm = min(ROW_TILE, T)
    nq = tm // Q
    row = lambda n: pl.BlockSpec((1, tm, n), lambda b, i: (b, i, 0))
    per_b = pl.BlockSpec((1, 1, D), lambda b, i: (b, 0, 0))
    nat = pl.BlockSpec((1, G, tm, dk), lambda b, i: (b, 0, i, 0))
    tr = pl.BlockSpec((1, G, dk, tm), lambda b, i: (b, 0, 0, i))
    nat_s = jax.ShapeDtypeStruct((B, G, T, dk), F32)
    key = pl.BlockSpec((1, G, tm, dk + POS_FEATS), lambda b, i: (b, 0, i, 0))
    key_s = jax.ShapeDtypeStruct((B, G, T, dk + POS_FEATS), BF16)
    tr_s = jax.ShapeDtypeStruct((B, G, dk, T), BF16)
    return pl.pallas_call(
        _in_proj_kernel,
        grid=(B, T // tm),
        in_specs=[row(D), _resident((1, D)), per_b, per_b, _resident((D, N))],
        out_specs=[pl.BlockSpec((1, G, nq, dk, HP * Q), lambda b, i: (b, 0, i, 0, 0)),
                   row(NSA_COLS_PAD - NSA_Q_DIM - 6 * NSA_KV_DIM), nat, nat, key, tr, key, tr, row(RWKV_COLS_PAD)],
        out_shape=[jax.ShapeDtypeStruct((B, G, T // Q, dk, HP * Q), BF16),
                   jax.ShapeDtypeStruct((B, T, NSA_COLS_PAD - NSA_Q_DIM - 6 * NSA_KV_DIM), F32),
                   nat_s, nat_s, key_s, tr_s, key_s, tr_s,
                   jax.ShapeDtypeStruct((B, T, RWKV_COLS_PAD), F32)],
        compiler_params=_cparams("parallel", "parallel"),
        name="in_proj",
    )(x, nw.reshape(1, D), sc, sh, w)


def _nsa_compress_kernel(xk_ref, xv_ref, pe_ref, w1_ref, w2_ref, kc_ref, vct_ref):
    half = CMP_STRIDE * HEAD_DIM

    def mlp(x, s):
        pe = pe_ref[s]
        xa = (x + pe[:, :half]).astype(BF16)
        xb = (x + pe[:, half:]).astype(BF16)
        w1 = w1_ref[s]
        p = jnp.dot(xa, w1[:half], preferred_element_type=F32)
        q = jnp.dot(xb, w1[half:], preferred_element_type=F32)
        hid = _gelu(p + pltpu.roll(q, x.shape[0] - 1, 0))
        return jnp.dot(hid.astype(BF16), w2_ref[s], preferred_element_type=F32)

    nc = xk_ref.shape[1]
    cmp_end = lax.broadcasted_iota(jnp.int32, (nc, 1), 0) * CMP_STRIDE + (CMP_LEN - 1)
    kc_ref[0] = _with_pos_features(mlp(xk_ref[0], 0), cmp_end)
    vct_ref[0] = jnp.transpose(mlp(xv_ref[0], 1).astype(BF16))


def _nsa_compress(xk, xv, pe, w1, w2):
    BG, NC, W = xk.shape
    src = pl.BlockSpec((1, NC, W), lambda b: (b, 0, 0))
    return pl.pallas_call(
        _nsa_compress_kernel,
        grid=(BG,),
        in_specs=[src, src, _resident((2, 1, 2 * W)), _resident((2, 2 * W, CMP_HIDDEN)),
                  _resident((2, CMP_HIDDEN, HEAD_DIM))],
        out_specs=[pl.BlockSpec((1, NC, HEAD_DIM + POS_FEATS), lambda b: (b, 0, 0)),
                   pl.BlockSpec((1, HEAD_DIM, NC), lambda b: (b, 0, 0))],
        out_shape=[jax.ShapeDtypeStruct((BG, NC, HEAD_DIM + POS_FEATS), BF16),
                   jax.ShapeDtypeStruct((BG, HEAD_DIM, NC), BF16)],
        compiler_params=_cparams("parallel"),
        name="nsa_compress",
    )(xk, xv, pe, w1, w2)


def _nsa_attn_kernel(qt_ref, glt_ref, kc_ref, vct_ref, ks_ref, vst_ref, kw_ref, vwt_ref, aggt_ref,
                     o_ref, selt_ref, sa_ref, sb_ref, pa_ref, pb_ref, tiles_ref, *, n_slc, top_k):
    g = pl.program_id(1)
    n = pl.program_id(2)
    Q, HP, R = Q_BLOCK, NSA_HPG, NSA_HPG * Q_BLOCK
    t0 = pl.multiple_of(n * Q, Q)
    qt = qt_ref[0, 0, 0]

    def per_head(a):
        return jnp.concatenate([a] * HP, axis=1)

    slope_g = jnp.float32(1.0)
    for gi in range(1, NSA_KV_HEADS):
        slope_g = jnp.where(g == gi, jnp.float32(2.0 ** (-HP * gi)), slope_g)
    lane_h = lax.broadcasted_iota(jnp.int32, (1, R), 1) // Q
    slope = jnp.full((1, R), LOG2E * 2.0 ** -HP, F32)
    for h in range(HP - 1):
        slope = jnp.where(lane_h == h, jnp.float32(LOG2E * 2.0 ** -(h + 1)), slope)
    slope = slope * slope_g
    tq = t0 + lax.broadcasted_iota(jnp.int32, (1, Q), 1)
    s1 = slope.astype(BF16)
    s2 = (slope - s1.astype(F32)).astype(BF16)
    s3 = (slope - s1.astype(F32) - s2.astype(F32)).astype(BF16)
    pieces = [p_.astype(F32) for p_ in (s1, s2, s3)]
    srows = [p_ * float(SLC_BLOCK) for p_ in pieces] + pieces + [jnp.zeros((POS_FEATS - 6, R), F32)]
    qt = jnp.concatenate([qt, jnp.concatenate(srows, axis=0).astype(BF16)], axis=0)

    ncp = kc_ref.shape[2]
    s = jnp.dot(kc_ref[0, 0], qt, preferred_element_type=F32)
    cmp_end = lax.broadcasted_iota(jnp.int32, (ncp, 1), 0) * CMP_STRIDE + (CMP_LEN - 1)
    dist = tq - cmp_end
    bias = per_head(jnp.where(dist >= 0, 0.0, NEG_INF))
    valid = bias > 0.5 * NEG_INF
    s = s + bias
    e = jnp.where(valid, jnp.exp2(s - jnp.max(s, axis=0, keepdims=True)), 0.0)
    l = jnp.sum(e, axis=0, keepdims=True)
    p = e * (1.0 / jnp.where(l > 0.0, l, 1.0))
    o_c = jnp.dot(vct_ref[0, 0], p.astype(BF16), preferred_element_type=F32)

    psum = p[:, 0:Q]
    for h in range(1, HP):
        psum = psum + p[:, h * Q:(h + 1) * Q]
    p_hi = psum.astype(BF16)
    p_lo = (psum - p_hi.astype(F32)).astype(BF16)
    aggt = aggt_ref[...]
    imp = jnp.dot(jnp.concatenate([aggt, aggt], axis=1), jnp.concatenate([p_hi, p_lo], axis=0),
                  preferred_element_type=F32)
    blk = lax.broadcasted_iota(jnp.int32, (NS_PAD, 1), 0)
    jt = tq // SLC_BLOCK
    forced = (blk == 0) | (blk == jt) | (blk == jt - 1)
    imp = jnp.where(blk > jt, -FORCE, jnp.where(forced, FORCE, imp))
    sel = jnp.where(forced & (blk <= jt), 1.0, 0.0)
    vals = jnp.where((blk < n_slc) & (sel < 0.5), imp, -jnp.inf)
    blk_f = blk.astype(F32)
    for _ in range(max(top_k - 3, 0)):
        vmax = jnp.max(vals, axis=0, keepdims=True)
        first = jnp.min(jnp.where(vals == vmax, blk_f, float(NS_PAD)), axis=0, keepdims=True)
        pick = blk_f == first
        sel = jnp.where(pick, 1.0, sel)
        vals = jnp.where(pick, -jnp.inf, vals)
    selt_ref[...] = sel

    TK = SLC_KEY_TILE
    nb = TK // SLC_BLOCK
    n_tiles = (t0 + Q + TK - 1) // TK
    num_tiles = ks_ref.shape[2] // TK
    last_tile = num_tiles - 1
    per_blk = lax.dot_general(jnp.ones((SUBLANES, Q), BF16), sel.astype(BF16), (((1,), (1,)), ((), ())),
                              preferred_element_type=F32)
    tile_of = (lax.broadcasted_iota(jnp.int32, (NS_PAD, LANES), 0) // nb
               == lax.broadcasted_iota(jnp.int32, (NS_PAD, LANES), 1))
    per_tile = jnp.dot(per_blk.astype(BF16), jnp.where(tile_of, 1.0, 0.0).astype(BF16),
                       preferred_element_type=F32)
    lane = lax.broadcasted_iota(jnp.int32, (1, LANES), 1)
    flag = (per_tile[0:1, :] > 0.5) & (lane < n_tiles)
    bit = lax.shift_left(jnp.int32(1), lane & (TILE_MASK_BITS - 1)).astype(F32)
    words = [jnp.sum(jnp.where(flag & (lane // TILE_MASK_BITS == w_), bit, 0.0)).astype(jnp.int32)
             for w_ in range(-(-num_tiles // TILE_MASK_BITS))]
    n_used = jnp.int32(0)
    for kt in range(num_tiles):
        tiles_ref[n_used] = jnp.int32(kt)
        n_used = n_used + ((words[kt // TILE_MASK_BITS] >> (kt % TILE_MASK_BITS)) & 1)
    tiles_ref[n_used] = jnp.int32(num_tiles)
    tiles_ref[n_used + 1] = jnp.int32(num_tiles)

    WK = WINDOW + Q
    w0 = pl.multiple_of(jnp.maximum(t0 - WINDOW, 0), Q)
    sw = jnp.dot(kw_ref[0, 0, pl.ds(w0, WK), :], qt, preferred_element_type=F32)
    dw = tq - (w0 + lax.broadcasted_iota(jnp.int32, (WK, 1), 0))
    okw = (dw >= 0) & (dw < WINDOW)
    sw = sw + per_head(jnp.where(okw, 0.0, NEG_INF))
    ew = jnp.exp2(sw - jnp.max(sw, axis=0, keepdims=True))
    lw = jnp.sum(ew, axis=0, keepdims=True)
    o_w = jnp.dot(vwt_ref[0, 0, :, pl.ds(w0, WK)], ew.astype(BF16), preferred_element_type=F32) * (1.0 / lw)

    key_in_blk = lax.broadcasted_iota(jnp.int32, (SLC_BLOCK, 1), 0)

    def qk(kt, s_ref):
        k0 = pl.multiple_of(jnp.minimum(kt, last_tile) * TK, TK)
        s_ref[...] = jnp.dot(ks_ref[0, 0, pl.ds(k0, TK), :], qt, preferred_element_type=F32)

    def pv(kt, p_ref):
        k0 = pl.multiple_of(jnp.clip(kt, 0, last_tile) * TK, TK)
        return jnp.dot(vst_ref[0, 0, :, pl.ds(k0, TK)], p_ref[...], preferred_element_type=F32)

    def softmax_tile(kt, s_ref, p_ref, m_i, l_i):
        dms = []
        for j in range(nb):
            rel = (kt * TK + j * SLC_BLOCK + key_in_blk) - tq
            chosen = selt_ref[pl.ds(jnp.minimum(kt * nb + j, NS_PAD - 1), 1), :] > 0.5
            dms.append(jnp.where(chosen & (rel <= 0), 0.0, NEG_INF))
        sc = s_ref[...] + per_head(jnp.concatenate(dms, axis=0))
        m_new = jnp.maximum(m_i, jnp.max(sc, axis=0, keepdims=True))
        alpha = jnp.exp2(m_i - m_new)
        pe_ = jnp.exp2(sc - m_new)
        p_ref[...] = pe_.astype(BF16)
        return m_new, alpha, alpha * l_i + jnp.sum(pe_, axis=0, keepdims=True)

    def slc_pair(j, carry):
        m_i, l_i, acc = carry
        ka, kb = tiles_ref[2 * j], tiles_ref[2 * j + 1]
        qk(kb, sb_ref)
        acc = acc + pv(tiles_ref[jnp.maximum(2 * j - 1, 0)], pb_ref)
        m_i, alpha, l_i = softmax_tile(ka, sa_ref, pa_ref, m_i, l_i)
        acc = alpha * acc
        qk(tiles_ref[2 * j + 2], sa_ref)
        acc = acc + pv(ka, pa_ref)
        m_i, alpha, l_i = softmax_tile(kb, sb_ref, pb_ref, m_i, l_i)
        return m_i, l_i, alpha * acc

    pb_ref[...] = jnp.zeros_like(pb_ref)
    qk(tiles_ref[0], sa_ref)
    n_pairs = (n_used + 1) // 2
    init = (jnp.full((1, R), NEG_INF, F32), jnp.zeros((1, R), F32), jnp.zeros((HEAD_DIM, R), F32))
    _, l_s, acc_s = lax.fori_loop(0, n_pairs, slc_pair, init)
    acc_s = acc_s + pv(tiles_ref[2 * n_pairs - 1], pb_ref)

    gates = jax.nn.sigmoid(glt_ref[0, 0, 0])
    o_ref[0, 0, 0] = gates[0:1] * o_c + gates[1:2] * (acc_s * (1.0 / l_s)) + gates[2:3] * o_w


def _nsa_attention(qt, glt, kc, vct, ks, vst, kw, vwt, aggt, *, n_slc, top_k):
    B, G, NQ, dk, R = qt.shape
    ncp = kc.shape[2]
    T = ks.shape[2]
    assert T >= WINDOW + Q_BLOCK
    per_bg = lambda b, g, n: (b, g, 0, 0)
    per_q = lambda rows: pl.BlockSpec((1, 1, 1, rows, R), lambda b, g, n: (b, g, n, 0, 0))
    return pl.pallas_call(
        functools.partial(_nsa_attn_kernel, n_slc=n_slc, top_k=top_k),
        grid=(B, G, NQ),
        in_specs=[per_q(dk), per_q(3),
                  pl.BlockSpec((1, 1, ncp, dk + POS_FEATS), per_bg),
                  pl.BlockSpec((1, 1, dk, ncp), per_bg),
                  pl.BlockSpec((1, 1, T, dk + POS_FEATS), per_bg),
                  pl.BlockSpec((1, 1, dk, T), per_bg),
                  pl.BlockSpec((1, 1, T, dk + POS_FEATS), per_bg),
                  pl.BlockSpec((1, 1, dk, T), per_bg),
                  _resident((NS_PAD, ncp))],
        out_specs=per_q(dk),
        out_shape=jax.ShapeDtypeStruct((B, G, NQ, dk, R), F32),
        scratch_shapes=[pltpu.VMEM((NS_PAD, Q_BLOCK), F32),
                        pltpu.VMEM((SLC_KEY_TILE, R), F32), pltpu.VMEM((SLC_KEY_TILE, R), F32),
                        pltpu.VMEM((SLC_KEY_TILE, R), BF16), pltpu.VMEM((SLC_KEY_TILE, R), BF16),
                        pltpu.SMEM((T // SLC_KEY_TILE + 2,), jnp.int32)],
        compiler_params=_cparams("parallel", "parallel", "arbitrary"),
        name="nsa_attention",
    )(qt, glt, kc, vct, ks, vst, kw, vwt, aggt)


def _rwkv_pre_kernel(z_ref, halo_ref, mu_ref, w0_ref, a0_ref, kk_ref, ka_ref, rk_ref, wup_ref, aup_ref, gup_ref,
                     ones_ref, csum_ref,
                     kap_ref, r_ref, v_ref, kend_ref, bend_ref, bt_ref, kt_ref, gend_ref, bonus_ref, g_ref):
    i = pl.program_id(1)
    z = z_ref[0]
    tm = z.shape[0]
    halo = jnp.where(i > 0, halo_ref[0], 0.0)
    zs = z + (_prev_rows(z, halo, 1) - z) * mu_ref[...]
    D = RWKV_DIM
    r, k, v = zs[:, 0:D], zs[:, D:2 * D], zs[:, 2 * D:3 * D]
    o1 = 3 * D
    o2 = o1 + RWKV_LORA_PAD
    o3 = o2 + RWKV_LORA_PAD
    wl, al, gl = zs[:, o1:o2], zs[:, o2:o3], zs[:, o3:]
    lw = -DECAY_SCALE * jax.nn.sigmoid(w0_ref[...] + _dot_bf16x3(jnp.tanh(wl), wup_ref[...]))
    a = jax.nn.sigmoid(a0_ref[...] + _dot_bf16x3(al, aup_ref[...]))
    kt = k * (1.0 + (a - 1.0) * ka_ref[...])
    kk = k * kk_ref[...]
    head_sums = _dot_exact_rhs(jnp.concatenate([kk * kk, r * kt * rk_ref[...]], axis=0), ones_ref[...])
    kk = kk / jnp.maximum(jnp.sqrt(head_sums[:tm]), 1e-12)
    b = kk * a
    C = RWKV_CHUNK
    cum = _dot_exact_lhs(csum_ref[...], lw)
    tot = jnp.concatenate([jnp.broadcast_to(cum[c * C + C - 1:c * C + C], (C, cum.shape[1]))
                           for c in range(tm // C)], axis=0)
    g_inv = jnp.exp(-cum)
    tail = jnp.exp(tot - cum)
    kap_ref[0] = kk * jnp.exp(cum - lw)
    r_ref[0] = r * jnp.exp(cum)
    v_ref[0] = v
    kend_ref[0] = kt * tail
    bend_ref[0] = b * tail
    b_h, k_h, g_end = b * g_inv, kt * g_inv, jnp.exp(tot)
    for c in range(tm // C):
        bt_ref[0, c] = jnp.transpose(b_h[c * C:(c + 1) * C])
        kt_ref[0, c] = jnp.transpose(k_h[c * C:(c + 1) * C])
    gend_ref[0] = jnp.concatenate([g_end[c * C:c * C + 1] for c in range(tm // C)], axis=0)
    bonus_ref[0] = head_sums[tm:] * v
    g_ref[0] = _dot_bf16x3(jax.nn.sigmoid(gl), gup_ref[...])


def _rwkv_pre(z, mu, w0, a0, k_k, k_a, r_k, w_up, a_up, g_up):
    B, T, W = z.shape
    D = RWKV_DIM
    tm = min(ROW_TILE, T)
    hb = tm // SUBLANES
    C = RWKV_CHUNK
    ones_bd = jnp.asarray(np.kron(np.eye(RWKV_HEADS), np.ones((HEAD_DIM, HEAD_DIM))), BF16)
    csum = jnp.asarray(np.kron(np.eye(tm // C), np.tril(np.ones((C, C)))), BF16)
    out = jax.ShapeDtypeStruct((B, T, D), F32)
    row = pl.BlockSpec((1, tm, D), lambda b, i: (b, i, 0))
    tr = pl.BlockSpec((1, tm // C, D, C), lambda b, i: (b, i, 0, 0))
    tr_s = jax.ShapeDtypeStruct((B, T // C, D, C), F32)
    return pl.pallas_call(
        _rwkv_pre_kernel,
        grid=(B, T // tm),
        in_specs=[pl.BlockSpec((1, tm, W), lambda b, i: (b, i, 0)),
                  pl.BlockSpec((1, SUBLANES, W), lambda b, i: (b, jnp.maximum(i * hb - 1, 0), 0)),
                  _resident((1, W)), _resident((1, D)), _resident((1, D)), _resident((1, D)), _resident((1, D)),
                  _resident((1, D)), _resident((RWKV_LORA_PAD, D)), _resident((RWKV_LORA_PAD, D)),
                  _resident((RWKV_GATE_PAD, D)), _resident((D, D)), _resident((tm, tm))],
        out_specs=[row] * 5 + [tr, tr, pl.BlockSpec((1, tm // C, D), lambda b, i: (b, i, 0)), row, row],
        out_shape=[out] * 5 + [tr_s, tr_s, jax.ShapeDtypeStruct((B, T // C, D), F32), out, out],
        compiler_params=_cparams("parallel", "parallel"),
        name="rwkv_pre",
    )(z, z, mu.reshape(1, W), w0.reshape(1, D), a0.reshape(1, D), k_k.reshape(1, D), k_a.reshape(1, D),
      r_k.reshape(1, D), w_up, a_up, g_up, ones_bd, csum)


def _rwkv_core_kernel(kap_ref, r_ref, v_ref, kend_ref, bend_ref, bt_ref, kt_ref, gend_ref, o_ref, h_ref):
    C, N, P = RWKV_CHUNK, HEAD_DIM, RWKV_PACK
    W = P * N
    chains = [(bi, slice(p * W, (p + 1) * W)) for bi in range(kap_ref.shape[0]) for p in range(kap_ref.shape[2] // W)]

    @pl.when(pl.program_id(1) == 0)
    def _():
        h_ref[...] = jnp.zeros_like(h_ref)

    ri = lax.broadcasted_iota(jnp.int32, (C, W), 0)
    cj = lax.broadcasted_iota(jnp.int32, (C, W), 1) % N
    tril = (ri >= cj).astype(F32)
    stril = (ri > cj).astype(F32)
    eye = (ri == cj).astype(F32)
    diag_blk = ((ri // RWKV_SUB) == (cj // RWKV_SUB)).astype(F32)
    same_head = (lax.broadcasted_iota(jnp.int32, (W, W), 0) // N) == (lax.broadcasted_iota(jnp.int32, (W, W), 1) // N)

    def split(x):
        hi = x.astype(BF16)
        return hi, (x - hi.astype(F32)).astype(BF16)

    def lhs2(x):
        hi, lo = split(x)
        return jnp.concatenate([hi, lo], axis=0), hi

    def bd(x):
        blk = lambda a: jnp.where(same_head, jnp.concatenate([a] * P, axis=0), jnp.zeros((), BF16))
        hi, lo = split(x)
        return blk(hi), blk(lo)

    def bd_t(xt):
        return split(jnp.where(same_head, jnp.concatenate([xt] * P, axis=1), 0.0))

    def mmw(l2, w):
        (x2, x_hi), (w_hi, w_lo) = l2, w
        m = x_hi.shape[0]
        t = jnp.dot(x2, w_hi, preferred_element_type=F32)
        return t[:m] + t[m:] + jnp.dot(x_hi, w_lo, preferred_element_type=F32)

    def mmp(x, y):
        return mmw(lhs2(x), bd(y))

    def gm(f, *cols):
        return [f(*args) for args in zip(*cols)]

    def chunk(c, hs):
        rows = pl.ds(pl.multiple_of(c * C, C), C)
        v = [v_ref[bi, rows, ln] for bi, ln in chains]
        lhs = [lhs2(jnp.concatenate([kap_ref[bi, rows, ln], r_ref[bi, rows, ln]], axis=0)) for bi, ln in chains]
        ab = gm(lambda l, ch: mmw(l, bd_t(bt_ref[ch[0], c, ch[1], :])), lhs, chains)
        ak = gm(lambda l, ch: mmw(l, bd_t(kt_ref[ch[0], c, ch[1], :])), lhs, chains)
        a_kb = [stril * x[:C] for x in ab]
        a_rb = [tril * x[C:] for x in ab]
        a_kr = [jnp.concatenate([stril * x[:C], tril * x[C:]], axis=0) for x in ak]
        d = [x * diag_blk for x in a_kb]
        e = gm(lambda x, y: x - y, a_kb, d)
        d_inv = [eye - x for x in d]
        pw = gm(mmp, d, d)
        steps = int(np.log2(RWKV_SUB)) - 1
        for s_ in range(steps):
            d_inv = gm(lambda x, y: mmp(x, eye + y), d_inv, pw)
            if s_ + 1 < steps:
                pw = gm(mmp, pw, pw)
        nb = gm(mmp, d_inv, e)
        t_inv = [eye - x for x in nb]
        pw = gm(mmp, nb, nb)
        bsteps = int(np.log2(C // RWKV_SUB)) - 1
        for s_ in range(bsteps):
            t_inv = gm(lambda x, y: mmp(x, eye + y), t_inv, pw)
            if s_ + 1 < bsteps:
                pw = gm(mmp, pw, pw)
        t_inv = gm(mmp, t_inv, d_inv)
        sh = gm(lambda l, h: mmw(l, bd(h)), lhs, hs)
        av = gm(mmp, a_kr, v)
        u = gm(lambda t, x, y: mmp(t, x[:C] + y[:C]), t_inv, sh, av)
        y = gm(lambda x, z, a, uu: x[C:] + z[C:] - mmp(a, uu), sh, av, a_rb, u)
        for (bi, ln), yy in zip(chains, y):
            o_ref[bi, rows, ln] = yy
        zero = jnp.zeros((C, W), F32)
        left = [jnp.concatenate([eye * gend_ref[bi, 0, pl.ds(c, 1), ln], kend_ref[bi, rows, ln],
                                 bend_ref[bi, rows, ln], zero], axis=0) for bi, ln in chains]
        right = gm(lambda h, vv, uu: split(jnp.concatenate([h, vv, -uu, zero], axis=0)), hs, v, u)
        full = gm(lambda l, rr: jnp.where(same_head, mmw(lhs2(jnp.transpose(l)), rr), 0.0), left, right)
        return tuple(sum(f[s_ * N:(s_ + 1) * N] for s_ in range(1, P)) + f[0:N] for f in full)

    hs = lax.fori_loop(0, kap_ref.shape[1] // C, chunk, tuple(h_ref[p] for p in range(len(chains))))
    for p in range(len(chains)):
        h_ref[p] = hs[p]


def _rwkv_core(kap, r, v, kend, bend, bt, kt, gend):
    B, T, D = kap.shape
    W = RWKV_PACK * HEAD_DIM
    assert D % W == 0 and W == 4 * RWKV_CHUNK
    bb = RWKV_BATCH_PER_STEP if B % RWKV_BATCH_PER_STEP == 0 else 1
    tc = min(RWKV_STEP_ROWS, T)
    nch = tc // RWKV_CHUNK
    seq = pl.BlockSpec((bb, tc, D), lambda b, i: (b, i, 0))
    tr = pl.BlockSpec((bb, nch, D, RWKV_CHUNK), lambda b, i: (b, i, 0, 0))
    return pl.pallas_call(
        _rwkv_core_kernel,
        grid=(B // bb, T // tc),
        in_specs=[seq] * 5 + [tr] * 2 + [pl.BlockSpec((bb, 1, nch, D), lambda b, i: (b, i, 0, 0))],
        out_specs=seq,
        out_shape=jax.ShapeDtypeStruct((B, T, D), F32),
        scratch_shapes=[pltpu.VMEM((bb * (D // W), HEAD_DIM, W), F32)],
        compiler_params=_cparams("parallel", "arbitrary"),
        name="rwkv_core",
    )(kap, r, v, kend, bend, bt, kt, gend.reshape(B, T // tc, nch, D))


def _out_proj_kernel(ot_ref, y_ref, bonus_ref, gg_ref, gw_ref, gb_ref, ones_ref, wa_ref, wb_ref, x_ref, g_ref,
                     o_ref):
    G, HP, Q = NSA_KV_HEADS, NSA_HPG, Q_BLOCK
    blocks = []
    for nq in range(ot_ref.shape[2]):
        heads = [jnp.transpose(ot_ref[0, g, nq][:, hh * Q:(hh + 1) * Q]) for g in range(G) for hh in range(HP)]
        blocks.append(jnp.concatenate(heads, axis=1))
    ya = jnp.concatenate(blocks, axis=0).astype(BF16)
    y = y_ref[0]
    inv_n = 1.0 / HEAD_DIM
    dlt = y - _dot_exact_rhs(y, ones_ref[...]) * inv_n
    var = _dot_exact_rhs(dlt * dlt, ones_ref[...]) * inv_n
    yb = (dlt * lax.rsqrt(var + RWKV_GN_EPS) * gw_ref[...] + gb_ref[...] + bonus_ref[0]) * gg_ref[0]
    o = jnp.dot(ya, wa_ref[...], preferred_element_type=F32)
    o = o + jnp.dot(yb.astype(BF16), wb_ref[...], preferred_element_type=F32)
    o_ref[0] = x_ref[0] + g_ref[0] * o


def _out_proj(ot, y, bonus, gg, gn_w, gn_b, wa, wb, x, gate):
    B, T, D = x.shape
    G, NQ, dk, R = ot.shape[1:]
    tm = min(ROW_TILE, T)
    ka, kb = wa.shape[0], y.shape[-1]
    ones_bd = jnp.asarray(np.kron(np.eye(kb // HEAD_DIM), np.ones((HEAD_DIM, HEAD_DIM))), BF16)
    row = lambda w: pl.BlockSpec((1, tm, w), lambda b, i: (b, i, 0))
    return pl.pallas_call(
        _out_proj_kernel,
        grid=(B, T // tm),
        in_specs=[pl.BlockSpec((1, G, tm // Q_BLOCK, dk, R), lambda b, i: (b, 0, i, 0, 0)),
                  row(kb), row(kb), row(kb), _resident((1, kb)), _resident((1, kb)), _resident((kb, kb)),
                  _resident((ka, D)), _resident((kb, D)), row(D),
                  pl.BlockSpec((1, 1, D), lambda b, i: (b, 0, 0))],
        out_specs=row(D),
        out_shape=jax.ShapeDtypeStruct((B, T, D), F32),
        compiler_params=_cparams("parallel", "parallel"),
        name="out_proj",
    )(ot, y, bonus, gg, gn_w.reshape(1, kb), gn_b.reshape(1, kb), ones_bd, wa, wb, x, gate)


def _sgu_kernel(x_ref, nw_ref, sc_ref, sh_ref, g_ref, wi_ref, bi_ref, vnw_ref, vnb_ref, ws_ref, bst_ref,
                wo_ref, bo_ref, o_ref, u_ref, v_ref, gated_ref):
    x = x_ref[0]
    tm = x.shape[0]
    E = SGU_DIM
    h = (_rms(x, nw_ref[...]) * (1.0 + sc_ref[0]) + sh_ref[0]).astype(BF16)
    for c0 in range(0, 2 * E, COL_CHUNK):
        zc = _gelu(jnp.dot(h, wi_ref[:, c0:c0 + COL_CHUNK], preferred_element_type=F32) + bi_ref[:, c0:c0 + COL_CHUNK])
        if c0 < E:
            u_ref[:, c0:c0 + COL_CHUNK] = zc
        else:
            v_ref[:, c0 - E:c0 - E + COL_CHUNK] = zc
    v = v_ref[...]
    mu = jnp.mean(v, axis=-1, keepdims=True)
    var = jnp.mean(jnp.square(v - mu), axis=-1, keepdims=True)
    vn = ((v - mu) * lax.rsqrt(var + LN_EPS) * vnw_ref[...] + vnb_ref[...]).astype(BF16)
    S = SGU_CHUNK
    gw = E // SGU_GROUPS
    causal = lax.broadcasted_iota(jnp.int32, (S, S), 0) >= lax.broadcasted_iota(jnp.int32, (S, S), 1)
    for gi in range(SGU_GROUPS):
        ws = jnp.where(causal, ws_ref[gi], 0.0).astype(BF16)
        bs = bst_ref[:, gi:gi + 1]
        for n in range(tm // S):
            sv = jnp.dot(ws, vn[n * S:(n + 1) * S, gi * gw:(gi + 1) * gw], preferred_element_type=F32) + bs
            u = u_ref[n * S:(n + 1) * S, gi * gw:(gi + 1) * gw]
            gated_ref[n * S:(n + 1) * S, gi * gw:(gi + 1) * gw] = (u * sv).astype(BF16)
    y = jnp.dot(gated_ref[...], wo_ref[...], preferred_element_type=F32) + bo_ref[...]
    o_ref[0] = x + g_ref[0] * y


def _sgu(x, nw, sc, sh, gate, w_in, b_in, vn_w, vn_b, w_s, b_s_t, w_out, b_out):
    B, T, D = x.shape
    E = SGU_DIM
    tm = min(ROW_TILE, T)
    assert tm % SGU_CHUNK == 0
    row = pl.BlockSpec((1, tm, D), lambda b, i: (b, i, 0))
    per_b = pl.BlockSpec((1, 1, D), lambda b, i: (b, 0, 0))
    return pl.pallas_call(
        _sgu_kernel,
        grid=(B, T // tm),
        in_specs=[row, _resident((1, D)), per_b, per_b, per_b,
                  _resident((D, 2 * E)), _resident((1, 2 * E)), _resident((1, E)), _resident((1, E)),
                  _resident((SGU_GROUPS, SGU_CHUNK, SGU_CHUNK)), _resident((SGU_CHUNK, SGU_GROUPS)),
                  _resident((E, D)), _resident((1, D))],
        out_specs=row,
        out_shape=jax.ShapeDtypeStruct((B, T, D), F32),
        scratch_shapes=[pltpu.VMEM((tm, E), F32), pltpu.VMEM((tm, E), F32), pltpu.VMEM((tm, E), BF16)],
        compiler_params=_cparams("parallel", "parallel"),
        name="sgu",
    )(x, nw.reshape(1, D), sc, sh, gate, w_in, b_in.reshape(1, 2 * E), vn_w.reshape(1, E), vn_b.reshape(1, E),
      w_s, b_s_t, w_out, b_out.reshape(1, D))


def _ffn_kernel(x_ref, halo_ref, nw_ref, sc_ref, sh_ref, g_ref, wg_ref, wu_ref, cw_ref, cb_ref, wd_ref, fn_ref,
                o_ref, act_ref, *, final):
    i = pl.program_id(1)
    x = x_ref[0]
    tm = x.shape[0]
    F = wg_ref.shape[1]
    HR = FFN_HALO_ROWS
    hx = (_rms(jnp.concatenate([halo_ref[0], x], axis=0), nw_ref[...]) * (1.0 + sc_ref[0]) + sh_ref[0]).astype(BF16)
    h = hx[HR:]
    for c0 in range(0, F, FFN_COL_CHUNK):
        cols = slice(c0, c0 + FFN_COL_CHUNK)
        ext = jnp.dot(hx, wg_ref[:, cols], preferred_element_type=F32)
        ext = jnp.concatenate([jnp.where(i > 0, ext[:HR], 0.0), ext[HR:]], axis=0)
        cw = cw_ref[:, cols]
        a = (cw[0:1] * ext[HR - 2:HR - 2 + tm] + cw[1:2] * ext[HR - 1:HR - 1 + tm] + cw[2:3] * ext[HR:]
             + cb_ref[:, cols])
        up = jnp.dot(h, wu_ref[:, cols], preferred_element_type=F32)
        act_ref[:, cols] = (_gelu(a) * up).astype(BF16)
    acc = jnp.dot(act_ref[...], wd_ref[...], preferred_element_type=F32)
    xn = x + g_ref[0] * acc
    if final:
        xn = _rms(xn, fn_ref[...])
    o_ref[0] = xn


def _ffn(x, nw, sc, sh, gate, w_gate, w_up, conv_w, conv_b, w_down, final_w, *, final):
    B, T, D = x.shape
    F = w_down.shape[0]
    tm = min(ROW_TILE, T)
    hb = tm // FFN_HALO_ROWS
    row = pl.BlockSpec((1, tm, D), lambda b, i: (b, i, 0))
    per_b = pl.BlockSpec((1, 1, D), lambda b, i: (b, 0, 0))
    return pl.pallas_call(
        functools.partial(_ffn_kernel, final=final),
        grid=(B, T // tm),
        in_specs=[row,
                  pl.BlockSpec((1, FFN_HALO_ROWS, D), lambda b, i: (b, jnp.maximum(i * hb - 1, 0), 0)),
                  _resident((1, D)), per_b, per_b, per_b,
                  _resident((D, F)), _resident((D, F)), _resident((CONV_WIDTH, F)), _resident((1, F)),
                  _resident((F, D)), _resident((1, D))],
        out_specs=row,
        out_shape=jax.ShapeDtypeStruct((B, T, D), F32),
        scratch_shapes=[pltpu.VMEM((tm, F), BF16)],
        compiler_params=_cparams("parallel", "parallel"),
        name="ffn",
    )(x, x, nw.reshape(1, D), sc, sh, gate, w_gate, w_up, conv_w, conv_b.reshape(1, F), w_down,
      final_w.reshape(1, D))


def _pad_cols(w, sizes, padded):
    parts, o = [], 0
    for s, p in zip(sizes, padded):
        parts.append(jnp.pad(w[..., o:o + s], [(0, 0)] * (w.ndim - 1) + [(0, p - s)]))
        o += s
    return jnp.concatenate(parts, axis=-1)


def _pad_rows(w, rows):
    return jnp.pad(w, ((0, rows - w.shape[0]), (0, 0)))


def _nsa_agg_t(T):
    n_cmp_pad = T // CMP_STRIDE
    n_slc = T // SLC_BLOCK
    c = np.arange(n_cmp_pad)
    s = np.arange(NS_PAD)
    cs, ce, ss = c * CMP_STRIDE, c * CMP_STRIDE + CMP_LEN - 1, s * SLC_BLOCK
    agg = (cs[None, :] < ss[:, None] + SLC_BLOCK) & (ce[None, :] >= ss[:, None]) & (s[:, None] < n_slc)
    agg &= (c[None, :] < n_cmp_pad - 1)
    return jnp.asarray(agg, BF16)


def _nsa_rwkv_mixer(x, nw, sc, sh, gate, w_in, cmp_pe, cmp_w1, cmp_w2, mu, w0, w_up, a0, a_up, g_up,
                    k_k, k_a, r_k, gn_w, gn_b, w_out):
    B, T, D = x.shape
    G, HPG, dk = NSA_KV_HEADS, NSA_HPG, HEAD_DIM
    n_slc = T // SLC_BLOCK
    NQ = T // Q_BLOCK
    assert n_slc <= NS_PAD and T % (CMP_STRIDE * LANES) == 0 and T % (2 * SLC_KEY_TILE) == 0

    rw_sizes = [RWKV_DIM] * 3 + [DECAY_LORA, AAA_LORA, GATE_LORA]
    rw_pads = [RWKV_DIM] * 3 + [RWKV_LORA_PAD, RWKV_LORA_PAD, RWKV_GATE_PAD]
    w_all = jnp.concatenate([jnp.pad(w_in[:, :NSA_COLS], ((0, 0), (0, NSA_COLS_PAD - NSA_COLS))),
                             _pad_cols(w_in[:, NSA_COLS:], rw_sizes, rw_pads)], axis=1).astype(BF16)
    qt, zg, kcs, vcs, ks, vst, kw, vwt, z_rw = _in_proj(x, nw, sc, sh, w_all)

    glt = zg[..., :NSA_GATE_DIM].reshape(B, NQ, Q_BLOCK, G, HPG, 3).transpose(0, 3, 1, 5, 4, 2)
    glt = glt.reshape(B, G, NQ, 3, HPG * Q_BLOCK)
    strides = lambda a: a.reshape(B * G, T // CMP_STRIDE, CMP_STRIDE * dk)
    kc, vct = _nsa_compress(strides(kcs), strides(vcs), cmp_pe.reshape(2, 1, CMP_LEN * dk),
                            cmp_w1.astype(BF16), cmp_w2.astype(BF16))
    o_t = _nsa_attention(qt, glt, kc.reshape(B, G, T // CMP_STRIDE, -1), vct.reshape(B, G, dk, T // CMP_STRIDE),
                         ks, vst, kw, vwt, _nsa_agg_t(T), n_slc=n_slc, top_k=min(SLC_TOPK, n_slc))

    kap, r, v, kend, bend, bt, kt, gend, bonus, gg = _rwkv_pre(
        z_rw, _pad_cols(mu, rw_sizes, rw_pads), w0, a0, k_k, k_a, r_k,
        _pad_rows(w_up, RWKV_LORA_PAD), _pad_rows(a_up, RWKV_LORA_PAD), _pad_rows(g_up, RWKV_GATE_PAD))
    y_b = _rwkv_core(kap, r, v, kend, bend, bt, kt, gend)

    w_out = w_out.astype(BF16)
    return _out_proj(o_t, y_b, bonus, gg, gn_w, gn_b, w_out[:NSA_Q_DIM], w_out[NSA_Q_DIM:], x, gate)


def kernel(x, c, ada_w, ada_b, norm_mix, norm_ffn, ffn_w_gate, ffn_w_up, ffn_conv_w, ffn_conv_b, ffn_w_down, ab_w_in, nsa_cmp_pe, nsa_cmp_w1, nsa_cmp_w2, rwkv_mu, rwkv_w0, rwkv_w_up, rwkv_a0, rwkv_a_up, rwkv_g_up, rwkv_k_k, rwkv_k_a, rwkv_r_k, rwkv_gn_w, rwkv_gn_b, ab_w_out, sgu_w_in, sgu_b_in, sgu_vn_w, sgu_vn_b, sgu_w_s, sgu_b_s, sgu_w_out, sgu_b_out, final_norm):
    B, T, D = x.shape
    mod = _ada_mod(c, ada_w, ada_b)
    for i in range(DEPTH):
        sh1, sc1, g1, sh2, sc2, g2 = [m.reshape(B, 1, D) for m in jnp.split(mod[i], 6, axis=-1)]
        j = i // 2
        if i % 2 == 0:
            x = _nsa_rwkv_mixer(x, norm_mix[i], sc1, sh1, g1, ab_w_in[j], nsa_cmp_pe[j], nsa_cmp_w1[j],
                                nsa_cmp_w2[j], rwkv_mu[j], rwkv_w0[j], rwkv_w_up[j], rwkv_a0[j], rwkv_a_up[j],
                                rwkv_g_up[j], rwkv_k_k[j], rwkv_k_a[j], rwkv_r_k[j], rwkv_gn_w[j], rwkv_gn_b[j],
                                ab_w_out[j])
        else:
            x = _sgu(x, norm_mix[i], sc1, sh1, g1, sgu_w_in[j].astype(BF16), sgu_b_in[j], sgu_vn_w[j], sgu_vn_b[j],
                     sgu_w_s[j], sgu_b_s[j].T, sgu_w_out[j].astype(BF16), sgu_b_out[j])
        x = _ffn(x, norm_ffn[i], sc2, sh2, g2, ffn_w_gate[i].astype(BF16), ffn_w_up[i].astype(BF16), ffn_conv_w[i],
                 ffn_conv_b[i], ffn_w_down[i].astype(BF16), final_norm, final=(i == DEPTH - 1))
    return x
```
